```python
import math
import jax, jax.numpy as jnp
from jax import lax
import numpy as np

D_MODEL = 1024
BATCH = 2
SEQ = 8192
DEPTH = 1
DEC_BATCH = 8
DEC_SEQ = 64
PAST_LEN = 2048

CHUNK = 64
Q_BLOCK = 128
EPS = 1e-6
DIFF_HEADS = 8
DIFF_HD = 64
DIFF_QK = DIFF_HEADS * 2 * DIFF_HD
DIFF_V = DIFF_HEADS * 2 * DIFF_HD
DIFF_SCALE = DIFF_HD ** -0.5
GLA_HEADS = 4
GLA_DK = 128
GLA_DV = 256
GLA_K = GLA_HEADS * GLA_DK
GLA_V = GLA_HEADS * GLA_DV
GLA_SCALE = GLA_DK ** -0.5
GK_RANK = 16
GK_NORM = 16.0
MEM_TOKENS = 256
MEM_HEADS = 4
MEM_HD = 256
MEM_W = MEM_HEADS * MEM_HD
MEM_SCALE = MEM_HD ** -0.5
D_FF = 2816
CONV_W = 3
IN_SIZES = (DIFF_QK, DIFF_QK, DIFF_V, GLA_K, GLA_K, GLA_V, GLA_V, GK_RANK, MEM_W, D_MODEL, D_MODEL, D_MODEL)
IN_WIDTH = sum(IN_SIZES)

kernel_name = 'hybrid_diffattn_gla_mem_convffn_stream_step'


def _rms(x, g):
    xf = x.astype(jnp.float32)
    y = xf * lax.rsqrt(jnp.mean(xf * xf, axis=-1, keepdims=True) + EPS)
    return (y * g.astype(jnp.float32)).astype(x.dtype)


def _split_cols(p):
    idx = [int(i) for i in np.cumsum(IN_SIZES)[:-1]]
    return jnp.split(p, idx, axis=-1)


def _heads(t, n, d):
    b, l = t.shape[:2]
    return t.reshape(b, l, n, d).transpose(0, 2, 1, 3).astype(jnp.float32)


def _diff_core(q, k, v, qpos, kpos, lam):
    s = jnp.einsum('bqhmd,bkhmd->bhmqk', q, k).astype(jnp.float32) * DIFF_SCALE
    mask = (kpos[None, :] // CHUNK) <= (qpos[:, None] // CHUNK)
    p = jax.nn.softmax(jnp.where(mask, s, -jnp.inf), axis=-1)
    a = p[:, :, 0] - lam * p[:, :, 1]
    return jnp.einsum('bhqk,bkhe->bqhe', a.astype(v.dtype), v)


def _diff_prompt(q, k, v, lam):
    b, s = q.shape[:2]
    nb = s // Q_BLOCK
    qb = q.reshape(b, nb, Q_BLOCK, DIFF_HEADS, 2, DIFF_HD).transpose(1, 0, 2, 3, 4, 5)
    kpos = jnp.arange(s)

    def blk(args):
        i, qi = args
        return _diff_core(qi, k, v, i * Q_BLOCK + jnp.arange(Q_BLOCK), kpos, lam)

    o = lax.map(blk, (jnp.arange(nb), qb))
    return o.transpose(1, 0, 2, 3, 4).reshape(b, s, DIFF_HEADS, 2 * DIFF_HD)


def _gla_chunk(state, q, k, v, g):
    l = q.shape[2]
    b = jnp.cumsum(g, axis=2)
    causal = jnp.tril(jnp.ones((l, l), dtype=bool))
    dec = jnp.exp(jnp.where(causal[None, None, :, :, None], b[:, :, :, None, :] - b[:, :, None, :, :], -jnp.inf))
    att = jnp.einsum('bhid,bhjd,bhijd->bhij', q, k, dec)
    o = jnp.einsum('bhid,bhdv->bhiv', q * jnp.exp(b), state) + jnp.einsum('bhij,bhjv->bhiv', att, v)
    b_last = b[:, :, -1:, :]
    new_state = jnp.exp(b_last[:, :, 0, :])[..., None] * state + jnp.einsum('bhjd,bhjv->bhdv', k * jnp.exp(b_last - b), v)
    return new_state, o


def _gla_prompt(q, k, v, g):
    b, h, s, _ = q.shape
    nc = s // CHUNK

    def to_chunks(t):
        return t.reshape(b, h, nc, CHUNK, t.shape[-1]).transpose(2, 0, 1, 3, 4)

    s0 = jnp.zeros((b, h, GLA_DK, GLA_DV), jnp.float32)
    s_fin, o = lax.scan(lambda st, xs: _gla_chunk(st, *xs), s0, (to_chunks(q), to_chunks(k), to_chunks(v), to_chunks(g)))
    return o.transpose(1, 2, 0, 3, 4).reshape(b, h, s, GLA_DV), s_fin


def _mem_kv(mem, g_mem, w_mem_kv, kn_mem):
    b, m, _ = mem.shape
    k, v = jnp.split(_rms(mem, g_mem) @ w_mem_kv, 2, axis=-1)
    k = _rms(k.reshape(b, m, MEM_HEADS, MEM_HD), kn_mem)
    return k, v.reshape(b, m, MEM_HEADS, MEM_HD)


def _mem_attn(q, k, v):
    b, l = q.shape[:2]
    s = jnp.einsum('bqhd,bkhd->bhqk', q, k).astype(jnp.float32) * MEM_SCALE
    p = jax.nn.softmax(s, axis=-1)
    return jnp.einsum('bhqk,bkhd->bqhd', p.astype(v.dtype), v).reshape(b, l, MEM_W)


def _mixer_block(x, k_past, v_past, mem_k, mem_v, gla_state, lam, lam_init, p):
    (g_attn, w_in, w_gk2, b_gk, qn_diff, kn_diff, subln_diff, subln_gla, qn_mem,
     w_proj_diff, w_proj_gla, w_proj_mem, w_out) = p
    b, l, _ = x.shape
    h = _rms(x, g_attn)
    qa, ka, va, qb, kb, vb, rb, gk_lr, qm, ga, gb, gm = _split_cols(h @ w_in)
    qa = _rms(qa.reshape(b, l, DIFF_HEADS, 2, DIFF_HD), qn_diff)
    ka = _rms(ka.reshape(b, l, DIFF_HEADS, 2, DIFF_HD), kn_diff)
    va = va.reshape(b, l, DIFF_HEADS, 2 * DIFF_HD)
    qm = _rms(qm.reshape(b, l, MEM_HEADS, MEM_HD), qn_mem)
    gk = jax.nn.log_sigmoid((gk_lr @ w_gk2 + b_gk).astype(jnp.float32)) / GK_NORM
    qh = _heads(qb, GLA_HEADS, GLA_DK) * GLA_SCALE
    kh = _heads(kb, GLA_HEADS, GLA_DK)
    vh = _heads(vb, GLA_HEADS, GLA_DV)
    gh = _heads(gk, GLA_HEADS, GLA_DK)
    if k_past is None:
        oa = _diff_prompt(qa, ka, va, lam)
        ob, gla_new = _gla_prompt(qh, kh, vh, gh)
    else:
        n_past = k_past.shape[1]
        k_all = jnp.concatenate([k_past.astype(ka.dtype), ka], axis=1)
        v_all = jnp.concatenate([v_past.astype(va.dtype), va], axis=1)
        oa = _diff_core(qa, k_all, v_all, n_past + jnp.arange(l), jnp.arange(n_past + l), lam)
        gla_new, ob = _gla_chunk(gla_state.astype(jnp.float32), qh, kh, vh, gh)
    oa = (_rms(oa, subln_diff) * (1.0 - lam_init)).reshape(b, l, DIFF_V)
    ob = _rms(ob.transpose(0, 2, 1, 3), subln_gla).reshape(b, l, GLA_V).astype(x.dtype) * jax.nn.silu(rb)
    om = _mem_attn(qm, mem_k.astype(qm.dtype), mem_v.astype(qm.dtype))
    m = (jax.nn.sigmoid(ga) * (oa @ w_proj_diff)
         + jax.nn.sigmoid(gb) * (ob @ w_proj_gla)
         + jax.nn.sigmoid(gm) * (om @ w_proj_mem))
    return x + m @ w_out, ka, va, gla_new


def _conv_ffn(x, conv_state, g_ffn, w_up, conv_w, conv_b, w_down):
    l = x.shape[1]
    u, v = jnp.split(_rms(x, g_ffn) @ w_up, 2, axis=-1)
    ext = jnp.concatenate([conv_state.astype(u.dtype), u], axis=1)
    uc = conv_b + sum(conv_w[j] * ext[:, j:j + l] for j in range(CONV_W))
    y = (jax.nn.gelu(uc) * v) @ w_down
    return x + y, ext[:, -(CONV_W - 1):]


def setup_inputs(seed: int = 0) -> dict:
    key = jax.random.key(seed)
    ks = iter(jax.random.split(key, 48))

    def nrm(shape, scale):
        return jax.random.normal(next(ks), shape, jnp.float32) * scale

    def gain(shape):
        return 1.0 + nrm(shape, 0.02)

    L = DEPTH
    return {
        'x_prompt': nrm((BATCH, SEQ, D_MODEL), 1.0),
        'x_sample': nrm((DEC_BATCH, DEC_SEQ, D_MODEL), 1.0),
        'mem_prompt': nrm((BATCH, MEM_TOKENS, D_MODEL), 1.0),
        'cache_diff_k': nrm((L, DEC_BATCH, PAST_LEN, DIFF_HEADS, 2, DIFF_HD), 1.0),
        'cache_diff_v': nrm((L, DEC_BATCH, PAST_LEN, DIFF_HEADS, 2 * DIFF_HD), 1.0),
        'cache_mem_k': nrm((L, DEC_BATCH, MEM_TOKENS, MEM_HEADS, MEM_HD), 1.0),
        'cache_mem_v': nrm((L, DEC_BATCH, MEM_TOKENS, MEM_HEADS, MEM_HD), 1.0),
        'state_gla': nrm((L, DEC_BATCH, GLA_HEADS, GLA_DK, GLA_DV), 1.0),
        'state_conv': nrm((L, DEC_BATCH, CONV_W - 1, D_FF), 1.0),
        'g_attn': gain((L, D_MODEL)),
        'w_in': nrm((L, D_MODEL, IN_WIDTH), D_MODEL ** -0.5),
        'w_gk2': nrm((L, GK_RANK, GLA_K), GK_RANK ** -0.5),
        'b_gk': nrm((L, GLA_K), 0.1),
        'qn_diff': gain((L, DIFF_HD)),
        'kn_diff': gain((L, DIFF_HD)),
        'lam_q1': nrm((L, DIFF_HD), 0.1),
        'lam_k1': nrm((L, DIFF_HD), 0.1),
        'lam_q2': nrm((L, DIFF_HD), 0.1),
        'lam_k2': nrm((L, DIFF_HD), 0.1),
        'subln_diff': gain((L, 2 * DIFF_HD)),
        'subln_gla': gain((L, GLA_DV)),
        'g_mem': gain((L, D_MODEL)),
        'w_mem_kv': nrm((L, D_MODEL, 2 * MEM_W), D_MODEL ** -0.5),
        'qn_mem': gain((L, MEM_HD)),
        'kn_mem': gain((L, MEM_HD)),
        'w_proj_diff': nrm((L, DIFF_V, D_MODEL), DIFF_V ** -0.5),
        'w_proj_gla': nrm((L, GLA_V, D_MODEL), GLA_V ** -0.5),
        'w_proj_mem': nrm((L, MEM_W, D_MODEL), MEM_W ** -0.5),
        'w_out': nrm((L, D_MODEL, D_MODEL), D_MODEL ** -0.5),
        'g_ffn': gain((L, D_MODEL)),
        'w_up': nrm((L, D_MODEL, 2 * D_FF), D_MODEL ** -0.5),
        'conv_w': nrm((L, CONV_W, D_FF), CONV_W ** -0.5),
        'conv_b': nrm((L, D_FF), 0.02),
        'w_down': nrm((L, D_FF, D_MODEL), D_FF ** -0.5),
    }


def reference(x_prompt, x_sample, mem_prompt, cache_diff_k, cache_diff_v, cache_mem_k, cache_mem_v,
              state_gla, state_conv, g_attn, w_in, w_gk2, b_gk, qn_diff, kn_diff, lam_q1, lam_k1,
              lam_q2, lam_k2, subln_diff, subln_gla, g_mem, w_mem_kv, qn_mem, kn_mem, w_proj_diff,
              w_proj_gla, w_proj_mem, w_out, g_ffn, w_up, conv_w, conv_b, w_down):
    xp, xs = x_prompt, x_sample
    dkp, dvp, mkp, mvp, gsp, csp = [], [], [], [], [], []
    dks, dvs, gss, css = [], [], [], []
    for l in range(DEPTH):
        lam_init = 0.8 - 0.6 * math.exp(-0.3 * l)
        lam = (jnp.exp(jnp.sum(lam_q1[l].astype(jnp.float32) * lam_k1[l].astype(jnp.float32)))
               - jnp.exp(jnp.sum(lam_q2[l].astype(jnp.float32) * lam_k2[l].astype(jnp.float32)))
               + lam_init)
        p = (g_attn[l], w_in[l], w_gk2[l], b_gk[l], qn_diff[l], kn_diff[l], subln_diff[l], subln_gla[l],
             qn_mem[l], w_proj_diff[l], w_proj_gla[l], w_proj_mem[l], w_out[l])
        fp = (g_ffn[l], w_up[l], conv_w[l], conv_b[l], w_down[l])
        mk, mv = _mem_kv(mem_prompt, g_mem[l], w_mem_kv[l], kn_mem[l])
        xp, kp, vp, sp = _mixer_block(xp, None, None, mk, mv, None, lam, lam_init, p)
        xp, cp = _conv_ffn(xp, jnp.zeros((xp.shape[0], CONV_W - 1, D_FF), xp.dtype), *fp)
        xs, ks_, vs_, ss = _mixer_block(xs, cache_diff_k[l], cache_diff_v[l], cache_mem_k[l], cache_mem_v[l],
                                        state_gla[l], lam, lam_init, p)
        xs, cs = _conv_ffn(xs, state_conv[l], *fp)
        dkp.append(kp); dvp.append(vp); mkp.append(mk); mvp.append(mv); gsp.append(sp); csp.append(cp)
        dks.append(ks_); dvs.append(vs_); gss.append(ss); css.append(cs)
    return (xp, xs, jnp.stack(dkp), jnp.stack(dvp), jnp.stack(mkp), jnp.stack(mvp), jnp.stack(gsp),
            jnp.stack(csp), jnp.stack(dks), jnp.stack(dvs), jnp.stack(gss), jnp.stack(css))
```

```python
import functools
import math

import jax
import jax.numpy as jnp
from jax import lax
from jax.experimental import pallas as pl
from jax.experimental.pallas import tpu as pltpu

F32 = jnp.float32
BF16 = jnp.bfloat16

D_MODEL = 1024
CHUNK = 64
EPS = 1e-6
DIFF_HEADS = 8
DIFF_HD = 64
DIFF_SCALE = DIFF_HD ** -0.5
GLA_HEADS = 4
GLA_DK = 128
GLA_DV = 256
GLA_SCALE = GLA_DK ** -0.5
GK_RANK = 16
GK_NORM = 16.0
MEM_HEADS = 4
MEM_HD = 256
MEM_SCALE = MEM_HD ** -0.5
D_FF = 2816
CONV_W = 3

LANES = 128
SUBLANES = 8
VMEM_LIMIT_BYTES = 48 * 1024 * 1024

T_QA, T_KA, T_VA, T_QKB, T_VB, T_RB, T_QM, T_GATE = 0, 1, 2, 3, 4, 5, 6, 7
N_PTILES = 10
SUB_BLOCK = 16
NEG_BIG = -1e30


def _dot(a, b):
    return jnp.dot(a, b, preferred_element_type=F32)


def _dot_nt(a, b):
    return lax.dot_general(a, b, (((1,), (1,)), ((), ())), preferred_element_type=F32)


def _sigmoid(x):
    return 1.0 / (1.0 + jnp.exp(-x))


def _pick(n, pref):
    t = min(n, pref)
    while n % t:
        t -= 1
    return t


def _rms_rows(x, gain):
    ms = jnp.mean(x * x, axis=-1, keepdims=True)
    return x * lax.rsqrt(ms + EPS) * gain


def _group_rms(y, gmat, gain):
    slab = gmat.shape[0]
    outs = []
    for c in range(y.shape[-1] // slab):
        ys = y[:, c * slab:(c + 1) * slab]
        ms = _dot((ys * ys).astype(BF16), gmat)
        outs.append(ys * lax.rsqrt(ms + EPS))
    return jnp.concatenate(outs, axis=-1) * gain


def _group_matrix(slab, group):
    r = jnp.arange(slab) // group
    return jnp.where(r[:, None] == r[None, :], 1.0 / group, 0.0).astype(BF16)


def _inproj_kernel(x_ref, g_ref, w_ref, wg1_ref, wg2_ref, bgk_ref, qn_ref, kn_ref, qmn_ref,
                   g64_ref, g256_ref, p_ref, kaf_ref, vaf_ref, gk_ref, h_scr):
    j = pl.program_id(1)

    @pl.when(j == 0)
    def _():
        h_scr[...] = _rms_rows(x_ref[...], g_ref[...]).astype(BF16)

    h = h_scr[...]
    acc = _dot(h, w_ref[...])

    @pl.when(j == T_QA)
    def _():
        p_ref[...] = (_group_rms(acc, g64_ref[...], qn_ref[...]) * DIFF_SCALE).astype(BF16)

    @pl.when(j == T_KA)
    def _():
        kn = _group_rms(acc, g64_ref[...], kn_ref[...])
        kaf_ref[...] = kn
        p_ref[...] = kn.astype(BF16)

    @pl.when(j == T_VA)
    def _():
        vaf_ref[...] = acc
        p_ref[...] = acc.astype(BF16)

    @pl.when(j == T_QKB)
    def _():
        half = acc.shape[-1] // 2
        p_ref[...] = jnp.concatenate([acc[:, :half] * GLA_SCALE, acc[:, half:]], axis=-1).astype(BF16)
        lr = _dot(h, wg1_ref[...])
        z = _dot(lr.astype(BF16), wg2_ref[...]) + bgk_ref[...]
        log_sig = jnp.minimum(z, 0.0) - jnp.log(1.0 + jnp.exp(-jnp.abs(z)))
        gk_ref[...] = log_sig * (1.0 / GK_NORM)

    @pl.when(j == T_VB)
    def _():
        p_ref[...] = acc.astype(BF16)

    @pl.when(j == T_RB)
    def _():
        p_ref[...] = (acc * _sigmoid(acc)).astype(BF16)

    @pl.when(j == T_QM)
    def _():
        p_ref[...] = (_group_rms(acc, g256_ref[...], qmn_ref[...]) * MEM_SCALE).astype(BF16)

    @pl.when(j >= T_GATE)
    def _():
        p_ref[...] = _sigmoid(acc).astype(BF16)


def _inproj(x2, g_attn, w_all, wg1, wg2, bgk, qn_t, kn_t, qmn_t, g64, g256, tm):
    t = x2.shape[0]
    d = D_MODEL
    full = lambda i, j: (0, 0)
    row = lambda i, j: (i, 0)
    return pl.pallas_call(
        _inproj_kernel,
        grid=(t // tm, N_PTILES),
        in_specs=[
            pl.BlockSpec((tm, d), row),
            pl.BlockSpec((1, d), full),
            pl.BlockSpec((d, d), lambda i, j: (0, j)),
            pl.BlockSpec(wg1.shape, full),
            pl.BlockSpec(wg2.shape, full),
            pl.BlockSpec(bgk.shape, full),
            pl.BlockSpec((1, d), full),
            pl.BlockSpec((1, d), full),
            pl.BlockSpec((1, d), full),
            pl.BlockSpec(g64.shape, full),
            pl.BlockSpec(g256.shape, full),
        ],
        out_specs=[
            pl.BlockSpec((tm, d), lambda i, j: (i, j)),
            pl.BlockSpec((tm, d), row),
            pl.BlockSpec((tm, d), row),
            pl.BlockSpec((tm, GLA_HEADS * GLA_DK), row),
        ],
        out_shape=[
            jax.ShapeDtypeStruct((t, N_PTILES * d), BF16),
            jax.ShapeDtypeStruct((t, d), F32),
            jax.ShapeDtypeStruct((t, d), F32),
            jax.ShapeDtypeStruct((t, GLA_HEADS * GLA_DK), F32),
        ],
        scratch_shapes=[pltpu.VMEM((tm, d), BF16)],
        compiler_params=pltpu.CompilerParams(
            dimension_semantics=("parallel", "arbitrary"), vmem_limit_bytes=VMEM_LIMIT_BYTES),
        name="inproj",
    )(x2, g_attn, w_all, wg1, wg2, bgk, qn_t, kn_t, qmn_t, g64, g256)


def _memkv_kernel(m_ref, g_ref, w_ref, kn_ref, g256_ref, k_ref, v_ref):
    j = pl.program_id(0)
    h = _rms_rows(m_ref[...], g_ref[...]).astype(BF16)
    acc = _dot(h, w_ref[...])

    @pl.when(j == 0)
    def _():
        k_ref[...] = _group_rms(acc, g256_ref[...], kn_ref[...])

    @pl.when(j == 1)
    def _():
        v_ref[...] = acc


def _memkv(mem2, g_mem, w_kv, knm_t, g256):
    t = mem2.shape[0]
    d = D_MODEL
    full = lambda j: (0, 0)
    return pl.pallas_call(
        _memkv_kernel,
        grid=(2,),
        in_specs=[
            pl.BlockSpec((t, d), full),
            pl.BlockSpec((1, d), full),
            pl.BlockSpec((d, d), lambda j: (0, j)),
            pl.BlockSpec((1, d), full),
            pl.BlockSpec(g256.shape, full),
        ],
        out_specs=[pl.BlockSpec((t, d), full), pl.BlockSpec((t, d), full)],
        out_shape=[jax.ShapeDtypeStruct((t, d), F32), jax.ShapeDtypeStruct((t, d), F32)],
        compiler_params=pltpu.CompilerParams(
            dimension_semantics=("arbitrary",), vmem_limit_bytes=VMEM_LIMIT_BYTES),
        name="memkv",
    )(mem2, g_mem, w_kv, knm_t, g256)


def _lambda_value(lamp, lam_init):
    a = jnp.sum(lamp[0:1, :] * lamp[1:2, :], axis=-1, keepdims=True)
    b = jnp.sum(lamp[2:3, :] * lamp[3:4, :], axis=-1, keepdims=True)
    return jnp.exp(a) - jnp.exp(b) + lam_init


def _stack_maps(q):
    lane = lax.broadcasted_iota(jnp.int32, q.shape, 1)
    zero = jnp.zeros_like(q)
    return jnp.concatenate([jnp.where(lane < DIFF_HD, q, zero), jnp.where(lane >= DIFF_HD, q, zero)], axis=0)


def _diff_finish(acc, l, lam, sub, lam_init, tq):
    o2 = acc / l
    o = o2[:tq] - lam * o2[tq:]
    return _rms_rows(o, sub) * (1.0 - lam_init)


def _diff_prompt_kernel(q_ref, k_ref, v_ref, lamp_ref, sub_ref, o_ref, m_scr, l_scr, acc_scr,
                        *, tq, lam_init):
    qi = pl.program_id(2)
    q2 = _stack_maps(q_ref[...])
    m_scr[...] = jnp.full(m_scr.shape, -jnp.inf, F32)
    l_scr[...] = jnp.zeros(l_scr.shape, F32)
    acc_scr[...] = jnp.zeros(acc_scr.shape, F32)

    def block(kstart, masked):
        k = k_ref[pl.ds(kstart, tq), :]
        v = v_ref[pl.ds(kstart, tq), :]
        s = _dot_nt(q2, k)
        if masked:
            r = lax.broadcasted_iota(jnp.int32, s.shape, 0)
            c = lax.broadcasted_iota(jnp.int32, s.shape, 1)
            r = jnp.where(r >= tq, r - tq, r)
            s = jnp.where((c // CHUNK) <= (r // CHUNK), s, -jnp.inf)
        m_prev = m_scr[...]
        m_new = jnp.maximum(m_prev, jnp.max(s, axis=-1, keepdims=True))
        alpha = jnp.exp(m_prev - m_new)
        p = jnp.exp(s - m_new)
        l_scr[...] = alpha * l_scr[...] + jnp.sum(p, axis=-1, keepdims=True)
        acc_scr[...] = alpha * acc_scr[...] + _dot(p.astype(BF16), v)
        m_scr[...] = m_new

    def body(kb, carry):
        block(pl.multiple_of(kb * tq, tq), False)
        return carry

    lax.fori_loop(0, qi, body, 0)
    block(pl.multiple_of(qi * tq, tq), True)

    lam = _lambda_value(lamp_ref[...], lam_init)
    o_ref[...] = _diff_finish(acc_scr[...], l_scr[...], lam, sub_ref[...], lam_init, tq).astype(o_ref.dtype)


def _diff_prompt(p3, lamp, sub, lam_init, tq):
    b, s, _ = p3.shape
    hd2 = 2 * DIFF_HD
    kern = functools.partial(_diff_prompt_kernel, tq=tq, lam_init=lam_init)
    per_d = D_MODEL // hd2
    return pl.pallas_call(
        kern,
        grid=(b, DIFF_HEADS, s // tq),
        in_specs=[
            pl.BlockSpec((None, tq, hd2), lambda bi, h, qi: (bi, qi, T_QA * per_d + h)),
            pl.BlockSpec((None, s, hd2), lambda bi, h, qi: (bi, 0, T_KA * per_d + h)),
            pl.BlockSpec((None, s, hd2), lambda bi, h, qi: (bi, 0, T_VA * per_d + h)),
            pl.BlockSpec(lamp.shape, lambda bi, h, qi: (0, 0)),
            pl.BlockSpec(sub.shape, lambda bi, h, qi: (0, 0)),
        ],
        out_specs=pl.BlockSpec((None, tq, hd2), lambda bi, h, qi: (bi, qi, h)),
        out_shape=jax.ShapeDtypeStruct((b, s, D_MODEL), BF16),
        scratch_shapes=[
            pltpu.VMEM((2 * tq, 1), F32),
            pltpu.VMEM((2 * tq, 1), F32),
            pltpu.VMEM((2 * tq, hd2), F32),
        ],
        compiler_params=pltpu.CompilerParams(
            dimension_semantics=("parallel", "parallel", "arbitrary"), vmem_limit_bytes=VMEM_LIMIT_BYTES),
        name="diff_attn_prompt",
    )(p3, p3, p3, lamp, sub)


def _diff_sample_kernel(q_ref, kn_ref, vn_ref, kc_ref, vc_ref, lamp_ref, sub_ref, o_ref, *, lam_init):
    tq = q_ref.shape[0]
    q2 = _stack_maps(q_ref[...])
    s_c = _dot_nt(q2, kc_ref[...].astype(BF16))
    s_n = _dot_nt(q2, kn_ref[...])
    m = jnp.maximum(jnp.max(s_c, axis=-1, keepdims=True), jnp.max(s_n, axis=-1, keepdims=True))
    p_c = jnp.exp(s_c - m)
    p_n = jnp.exp(s_n - m)
    l = jnp.sum(p_c, axis=-1, keepdims=True) + jnp.sum(p_n, axis=-1, keepdims=True)
    acc = _dot(p_c.astype(BF16), vc_ref[...].astype(BF16)) + _dot(p_n.astype(BF16), vn_ref[...])
    lam = _lambda_value(lamp_ref[...], lam_init)
    o_ref[...] = _diff_finish(acc, l, lam, sub_ref[...], lam_init, tq).astype(o_ref.dtype)


def _diff_sample(p3, kc, vc, lamp, sub, lam_init):
    b, l, _ = p3.shape
    n_past = kc.shape[1]
    hd2 = 2 * DIFF_HD
    per_d = D_MODEL // hd2
    kern = functools.partial(_diff_sample_kernel, lam_init=lam_init)
    return pl.pallas_call(
        kern,
        grid=(b, DIFF_HEADS),
        in_specs=[
            pl.BlockSpec((None, l, hd2), lambda bi, h: (bi, 0, T_QA * per_d + h)),
            pl.BlockSpec((None, l, hd2), lambda bi, h: (bi, 0, T_KA * per_d + h)),
            pl.BlockSpec((None, l, hd2), lambda bi, h: (bi, 0, T_VA * per_d + h)),
            pl.BlockSpec((None, n_past, hd2), lambda bi, h: (bi, 0, h)),
            pl.BlockSpec((None, n_past, hd2), lambda bi, h: (bi, 0, h)),
            pl.BlockSpec(lamp.shape, lambda bi, h: (0, 0)),
            pl.BlockSpec(sub.shape, lambda bi, h: (0, 0)),
        ],
        out_specs=pl.BlockSpec((None, l, hd2), lambda bi, h: (bi, 0, h)),
        out_shape=jax.ShapeDtypeStruct((b, l, D_MODEL), BF16),
        compiler_params=pltpu.CompilerParams(
            dimension_semantics=("parallel", "parallel"), vmem_limit_bytes=VMEM_LIMIT_BYTES),
        name="diff_attn_sample",
    )(p3, p3, p3, kc, vc, lamp, sub)


def _bcast_rows(x, period, row):
    r, c = x.shape
    x3 = x.reshape(r // period, period, c)
    return jnp.broadcast_to(x3[:, row:row + 1, :], x3.shape).reshape(r, c)


def _gla_kernel(*refs, rows, has_init):
    if has_init:
        q_ref, k_ref, v_ref, r_ref, g_ref, sub_ref, s0_ref, o_ref, sout_ref, st_scr, kf_scr, b_scr = refs
    else:
        q_ref, k_ref, v_ref, r_ref, g_ref, sub_ref, o_ref, sout_ref, st_scr, kf_scr, b_scr = refs
        s0_ref = None
    step = pl.program_id(2)
    n_chunks = rows // CHUNK
    n_sub = CHUNK // SUB_BLOCK

    @pl.when(step == 0)
    def _():
        if has_init:
            st_scr[...] = s0_ref[...].T
        else:
            st_scr[...] = jnp.zeros(st_scr.shape, F32)

    q = q_ref[...].astype(F32)
    k = k_ref[...].astype(F32)
    g = g_ref[...]
    v = v_ref[...]

    ri = lax.broadcasted_iota(jnp.int32, (rows, rows), 0)
    ci = lax.broadcasted_iota(jnp.int32, (rows, rows), 1)
    tri = jnp.where((ci <= ri) & ((ri // CHUNK) == (ci // CHUNK)), 1.0, 0.0).astype(BF16)
    g1 = g.astype(BF16)
    rem = g - g1.astype(F32)
    g2 = rem.astype(BF16)
    g3 = (rem - g2.astype(F32)).astype(BF16)
    b = _dot(tri, g1) + _dot(tri, g2) + _dot(tri, g3)
    bex = b - g

    b_last = _bcast_rows(b, CHUNK, CHUNK - 1)
    b_blk = _bcast_rows(bex, SUB_BLOCK, 0)
    q_blk = q * jnp.exp(b - b_blk)
    q_chk = q * jnp.exp(b)
    k_end = k * jnp.exp(b_last - b)

    rowc = lax.broadcasted_iota(jnp.int32, (rows, GLA_DK), 0) % CHUNK
    lane = lax.broadcasted_iota(jnp.int32, (rows, GLA_DK), 1)
    zero = jnp.zeros_like(q)

    lhs_parts, rhs_parts = [], []
    for blk in range(1, n_sub):
        b_ref_blk = _bcast_rows(bex, CHUNK, blk * SUB_BLOCK)
        k_blk = k * jnp.exp(jnp.where(rowc < blk * SUB_BLOCK, b_ref_blk - b, NEG_BIG))
        lhs_parts.append(jnp.where((rowc // SUB_BLOCK) == blk, q_blk, zero))
        rhs_parts.append(k_blk)
    lhs = jnp.concatenate(lhs_parts, axis=-1).astype(BF16)
    rhs = jnp.concatenate(rhs_parts, axis=-1).astype(BF16)

    pad = SUB_BLOCK
    kf_scr[0:pad, :] = jnp.zeros((pad, GLA_DK), F32)
    b_scr[0:pad, :] = jnp.zeros((pad, GLA_DK), F32)
    kf_scr[pad:pad + rows, :] = k
    b_scr[pad:pad + rows, :] = b
    ones = jnp.ones((GLA_DK, LANES), BF16)
    delta = rowc - lane
    row_sub = rowc % SUB_BLOCK
    a_diag = jnp.zeros((rows, LANES), F32)
    for d in range(SUB_BLOCK):
        kd = kf_scr[pad - d:pad - d + rows, :]
        bd = b_scr[pad - d:pad - d + rows, :]
        e = jnp.exp(jnp.where(row_sub >= d, b - bd, NEG_BIG))
        rd = _dot((q * kd * e).astype(BF16), ones)
        a_diag = a_diag + jnp.where(delta == d, rd, 0.0)

    outs = []
    for c in range(n_chunks):
        sl = slice(c * CHUNK, (c + 1) * CHUNK)
        att = _dot_nt(lhs[sl], rhs[sl]) + a_diag[sl, :CHUNK]
        st = st_scr[...]
        o = _dot(att.astype(BF16), v[sl]) + _dot_nt(q_chk[sl].astype(BF16), st.astype(BF16))
        d_st = _dot(v[sl].astype(F32).T.astype(BF16), k_end[sl].astype(BF16))
        dec = jnp.exp(b[(c + 1) * CHUNK - 1:(c + 1) * CHUNK, :])
        st_scr[...] = st * dec + d_st
        outs.append(o)
    o = jnp.concatenate(outs, axis=0) if n_chunks > 1 else outs[0]
    o_ref[...] = (_rms_rows(o, sub_ref[...]) * r_ref[...].astype(F32)).astype(o_ref.dtype)

    @pl.when(step == pl.num_programs(2) - 1)
    def _():
        sout_ref[...] = st_scr[...].T


def _gla(p3, gk3, sub, s0, rows):
    b, s, _ = p3.shape
    has_init = s0 is not None
    kq = D_MODEL // GLA_DK
    kv = D_MODEL // GLA_DV
    in_specs = [
        pl.BlockSpec((None, rows, GLA_DK), lambda bi, h, r: (bi, r, T_QKB * kq + h)),
        pl.BlockSpec((None, rows, GLA_DK), lambda bi, h, r: (bi, r, T_QKB * kq + GLA_HEADS + h)),
        pl.BlockSpec((None, rows, GLA_DV), lambda bi, h, r: (bi, r, T_VB * kv + h)),
        pl.BlockSpec((None, rows, GLA_DV), lambda bi, h, r: (bi, r, T_RB * kv + h)),
        pl.BlockSpec((None, rows, GLA_DK), lambda bi, h, r: (bi, r, h)),
        pl.BlockSpec(sub.shape, lambda bi, h, r: (0, 0)),
    ]
    args = [p3, p3, p3, p3, gk3, sub]
    if has_init:
        in_specs.append(pl.BlockSpec((None, None, GLA_DK, GLA_DV), lambda bi, h, r: (bi, h, 0, 0)))
        args.append(s0)
    kern = functools.partial(_gla_kernel, rows=rows, has_init=has_init)
    return pl.pallas_call(
        kern,
        grid=(b, GLA_HEADS, s // rows),
        in_specs=in_specs,
        out_specs=[
            pl.BlockSpec((None, rows, GLA_DV), lambda bi, h, r: (bi, r, h)),
            pl.BlockSpec((None, None, GLA_DK, GLA_DV), lambda bi, h, r: (bi, h, 0, 0)),
        ],
        out_shape=[
            jax.ShapeDtypeStruct((b, s, GLA_HEADS * GLA_DV), BF16),
            jax.ShapeDtypeStruct((b, GLA_HEADS, GLA_DK, GLA_DV), F32),
        ],
        scratch_shapes=[
            pltpu.VMEM((GLA_DV, GLA_DK), F32),
            pltpu.VMEM((SUB_BLOCK + rows, GLA_DK), F32),
            pltpu.VMEM((SUB_BLOCK + rows, GLA_DK), F32),
        ],
        compiler_params=pltpu.CompilerParams(
            dimension_semantics=("parallel", "parallel", "arbitrary"), vmem_limit_bytes=VMEM_LIMIT_BYTES),
        name="gla",
    )(*args)


def _mem_attn_kernel(q_ref, k_ref, v_ref, o_ref):
    outs = []
    for h in range(MEM_HEADS):
        sl = slice(h * MEM_HD, (h + 1) * MEM_HD)
        s = _dot_nt(q_ref[:, sl], k_ref[:, sl].astype(BF16))
        m = jnp.max(s, axis=-1, keepdims=True)
        p = jnp.exp(s - m)
        l = jnp.sum(p, axis=-1, keepdims=True)
        outs.append(_dot(p.astype(BF16), v_ref[:, sl].astype(BF16)) / l)
    o_ref[...] = jnp.concatenate(outs, axis=-1).astype(o_ref.dtype)


def _mem_attn(p3, mk, mv, tq):
    b, s, _ = p3.shape
    m = mk.shape[1]
    d = D_MODEL
    return pl.pallas_call(
        _mem_attn_kernel,
        grid=(b, s // tq),
        in_specs=[
            pl.BlockSpec((None, tq, d), lambda bi, i: (bi, i, T_QM)),
            pl.BlockSpec((None, m, d), lambda bi, i: (bi, 0, 0)),
            pl.BlockSpec((None, m, d), lambda bi, i: (bi, 0, 0)),
        ],
        out_specs=pl.BlockSpec((None, tq, d), lambda bi, i: (bi, i, 0)),
        out_shape=jax.ShapeDtypeStruct((b, s, d), BF16),
        compiler_params=pltpu.CompilerParams(
            dimension_semantics=("parallel", "parallel"), vmem_limit_bytes=VMEM_LIMIT_BYTES),
        name="mem_attn",
    )(p3, mk, mv)


def _mix_kernel(x_ref, oa_ref, ob_ref, om_ref, ga_ref, gb_ref, gm_ref, wd_ref, wg_ref, wm_ref, wo_ref,
                y_ref):
    m = (ga_ref[...].astype(F32) * _dot(oa_ref[...], wd_ref[...])
         + gb_ref[...].astype(F32) * _dot(ob_ref[...], wg_ref[...])
         + gm_ref[...].astype(F32) * _dot(om_ref[...], wm_ref[...]))
    y_ref[...] = x_ref[...] + _dot(m.astype(BF16), wo_ref[...])


def _mix(x2, oa, ob, om, p2, wd, wg, wm, wo, tm):
    t = x2.shape[0]
    d = D_MODEL
    row = lambda i: (i, 0)
    full = lambda i: (0, 0)
    wspec = pl.BlockSpec((d, d), full)
    return pl.pallas_call(
        _mix_kernel,
        grid=(t // tm,),
        in_specs=[
            pl.BlockSpec((tm, d), row), pl.BlockSpec((tm, d), row), pl.BlockSpec((tm, d), row),
            pl.BlockSpec((tm, d), row),
            pl.BlockSpec((tm, d), lambda i: (i, T_GATE)),
            pl.BlockSpec((tm, d), lambda i: (i, T_GATE + 1)),
            pl.BlockSpec((tm, d), lambda i: (i, T_GATE + 2)),
            wspec, wspec, wspec, wspec,
        ],
        out_specs=pl.BlockSpec((tm, d), row),
        out_shape=jax.ShapeDtypeStruct((t, d), F32),
        compiler_params=pltpu.CompilerParams(
            dimension_semantics=("parallel",), vmem_limit_bytes=VMEM_LIMIT_BYTES),
        name="mix_out",
    )(x2, oa, ob, om, p2, p2, p2, wd, wg, wm, wo)


def _ffn_kernel(x_ref, g_ref, wu_ref, wv_ref, cw_ref, cb_ref, wd_ref, cs_ref, y_ref, cso_ref,
                h_scr, acc_scr, u_scr, carry_scr, *, n_seq, seq_rows, tiles_per_seq):
    i = pl.program_id(0)
    f = pl.program_id(1)
    nf = pl.num_programs(1)
    tf = wu_ref.shape[1]
    gap = SUBLANES
    stride = seq_rows + gap
    tail = CONV_W - 1

    @pl.when(f == 0)
    def _():
        h_scr[...] = _rms_rows(x_ref[...], g_ref[...]).astype(BF16)
        acc_scr[...] = jnp.zeros(acc_scr.shape, F32)

    h = h_scr[...]
    u = _dot(h, wu_ref[...])
    vv = _dot(h, wv_ref[...])
    cw = cw_ref[...]
    cb = cb_ref[...]

    first = (i % tiles_per_seq) == 0
    for s in range(n_seq):
        base = s * stride

        @pl.when(first)
        def _():
            u_scr[base:base + gap, :] = jnp.zeros((gap, tf), F32)
            u_scr[base + gap - tail:base + gap, :] = cs_ref[s]

        @pl.when(jnp.logical_not(first))
        def _():
            u_scr[base:base + gap, :] = carry_scr[f]

        u_scr[base + gap:base + gap + seq_rows, :] = u[s * seq_rows:(s + 1) * seq_rows]

    outs = []
    for s in range(n_seq):
        base = s * stride + gap
        conv = cb
        for j in range(CONV_W):
            off = base - tail + j
            conv = conv + cw[j:j + 1, :] * u_scr[off:off + seq_rows, :]
        outs.append(conv)
        cso_ref[s] = u_scr[base + seq_rows - tail:base + seq_rows, :]
    uc = jnp.concatenate(outs, axis=0) if n_seq > 1 else outs[0]
    carry_scr[f] = u_scr[seq_rows:seq_rows + gap, :]

    gelu = 0.5 * uc * (1.0 + jnp.tanh(math.sqrt(2.0 / math.pi) * (uc + 0.044715 * (uc * uc * uc))))
    acc_scr[...] += _dot((gelu * vv).astype(BF16), wd_ref[...])

    @pl.when(f == nf - 1)
    def _():
        y_ref[...] = x_ref[...] + acc_scr[...]


def _ffn(x2, g_ffn, w_up, conv_w, conv_b, w_down, conv_state, n_seq, seq_rows, tiles_per_seq, tf):
    t = x2.shape[0]
    d = D_MODEL
    tm = n_seq * seq_rows
    nf = D_FF // tf
    nb = conv_state.shape[0]
    tail = CONV_W - 1
    kern = functools.partial(_ffn_kernel, n_seq=n_seq, seq_rows=seq_rows, tiles_per_seq=tiles_per_seq)
    seq_blk = lambda i, f: ((i // tiles_per_seq), 0, f)
    y, tails = pl.pallas_call(
        kern,
        grid=(t // tm, nf),
        in_specs=[
            pl.BlockSpec((tm, d), lambda i, f: (i, 0)),
            pl.BlockSpec((1, d), lambda i, f: (0, 0)),
            pl.BlockSpec((d, tf), lambda i, f: (0, f)),
            pl.BlockSpec((d, tf), lambda i, f: (0, nf + f)),
            pl.BlockSpec((CONV_W, tf), lambda i, f: (0, f)),
            pl.BlockSpec((1, tf), lambda i, f: (0, f)),
            pl.BlockSpec((tf, d), lambda i, f: (f, 0)),
            pl.BlockSpec((n_seq, tail, tf), seq_blk),
        ],
        out_specs=[
            pl.BlockSpec((tm, d), lambda i, f: (i, 0)),
            pl.BlockSpec((n_seq, tail, tf), lambda i, f: (i, 0, f)),
        ],
        out_shape=[
            jax.ShapeDtypeStruct((t, d), F32),
            jax.ShapeDtypeStruct((nb * tiles_per_seq, tail, D_FF), F32),
        ],
        scratch_shapes=[
            pltpu.VMEM((tm, d), BF16),
            pltpu.VMEM((tm, d), F32),
            pltpu.VMEM((n_seq * (seq_rows + SUBLANES), tf), F32),
            pltpu.VMEM((nf, SUBLANES, tf), F32),
        ],
        compiler_params=pltpu.CompilerParams(
            dimension_semantics=("arbitrary", "arbitrary"), vmem_limit_bytes=VMEM_LIMIT_BYTES),
        name="conv_ffn",
    )(x2, g_ffn, w_up, w_up, conv_w, conv_b, w_down, conv_state)
    return y, tails.reshape(nb, tiles_per_seq, tail, D_FF)[:, -1]


def _tile_gain(g, reps):
    return jnp.tile(g.astype(F32), reps).reshape(1, -1)


def _layer_weights(l, g_attn, w_in, w_gk2, b_gk, qn_diff, kn_diff, lam_q1, lam_k1, lam_q2, lam_k2,
                   subln_diff, subln_gla, g_mem, w_mem_kv, qn_mem, kn_mem, w_proj_diff, w_proj_gla,
                   w_proj_mem, w_out, g_ffn, w_up, conv_w, conv_b, w_down):
    d = D_MODEL
    w = w_in[l]
    lr0 = 6 * d
    w_all = jnp.concatenate([w[:, :lr0], w[:, lr0 + GK_RANK:]], axis=1).astype(BF16)
    wg1 = jnp.pad(w[:, lr0:lr0 + GK_RANK], ((0, 0), (0, LANES - GK_RANK))).astype(BF16)
    wg2 = jnp.pad(w_gk2[l], ((0, LANES - GK_RANK), (0, 0))).astype(BF16)
    return dict(
        g_attn=g_attn[l].reshape(1, d), w_all=w_all, wg1=wg1, wg2=wg2, bgk=b_gk[l].reshape(1, -1),
        qn_t=_tile_gain(qn_diff[l], d // DIFF_HD), kn_t=_tile_gain(kn_diff[l], d // DIFF_HD),
        qmn_t=_tile_gain(qn_mem[l], d // MEM_HD), knm_t=_tile_gain(kn_mem[l], d // MEM_HD),
        g64=_group_matrix(LANES, DIFF_HD), g256=_group_matrix(MEM_HD, MEM_HD),
        lamp=jnp.stack([lam_q1[l], lam_k1[l], lam_q2[l], lam_k2[l]]).astype(F32),
        sub_diff=subln_diff[l].reshape(1, -1), sub_gla=subln_gla[l].reshape(1, -1),
        g_mem=g_mem[l].reshape(1, d), w_mem_kv=w_mem_kv[l].astype(BF16),
        wd=w_proj_diff[l].astype(BF16), wg=w_proj_gla[l].astype(BF16), wm=w_proj_mem[l].astype(BF16),
        wo=w_out[l].astype(BF16), g_ffn=g_ffn[l].reshape(1, d), w_up=w_up[l].astype(BF16),
        conv_w=conv_w[l], conv_b=conv_b[l].reshape(1, -1), w_down=w_down[l].astype(BF16),
    )


def _group(x, wts, lam_init, mem_k, mem_v, past_k, past_v, gla_state, conv_state, prompt):
    b, s, d = x.shape
    t = b * s
    x2 = x.reshape(t, d)
    tm = _pick(t, 512)
    p2, ka, va, gk = _inproj(x2, wts["g_attn"], wts["w_all"], wts["wg1"], wts["wg2"], wts["bgk"],
                             wts["qn_t"], wts["kn_t"], wts["qmn_t"], wts["g64"], wts["g256"], tm)
    p3 = p2.reshape(b, s, N_PTILES * d)
    gk3 = gk.reshape(b, s, GLA_HEADS * GLA_DK)
    if prompt:
        oa = _diff_prompt(p3, wts["lamp"], wts["sub_diff"], lam_init, _pick(s, 512))
        ob, gla_new = _gla(p3, gk3, wts["sub_gla"], None, _pick(s, 512))
    else:
        oa = _diff_sample(p3, past_k, past_v, wts["lamp"], wts["sub_diff"], lam_init)
        ob, gla_new = _gla(p3, gk3, wts["sub_gla"], gla_state, s)
    om = _mem_attn(p3, mem_k, mem_v, _pick(s, 512))
    x1 = _mix(x2, oa.reshape(t, d), ob.reshape(t, d), om.reshape(t, d), p2,
              wts["wd"], wts["wg"], wts["wm"], wts["wo"], tm)
    tf = 256
    if prompt:
        rows = _pick(s, 1024)
        y, cs = _ffn(x1, wts["g_ffn"], wts["w_up"], wts["conv_w"], wts["conv_b"], wts["w_down"],
                     conv_state, 1, rows, s // rows, tf)
    else:
        y, cs = _ffn(x1, wts["g_ffn"], wts["w_up"], wts["conv_w"], wts["conv_b"], wts["w_down"],
                     conv_state, b, s, 1, tf)
    return y.reshape(b, s, d), ka, va, gla_new, cs


def kernel(x_prompt, x_sample, mem_prompt, cache_diff_k, cache_diff_v, cache_mem_k, cache_mem_v, state_gla, state_conv, g_attn, w_in, w_gk2, b_gk, qn_diff, kn_diff, lam_q1, lam_k1, lam_q2, lam_k2, subln_diff, subln_gla, g_mem, w_mem_kv, qn_mem, kn_mem, w_proj_diff, w_proj_gla, w_proj_mem, w_out, g_ffn, w_up, conv_w, conv_b, w_down):
    depth = g_attn.shape[0]
    d = D_MODEL
    xp, xs = x_prompt, x_sample
    bp, sp, _ = xp.shape
    bs, ss, _ = xs.shape
    n_mem = mem_prompt.shape[1]
    outs = [[] for _ in range(10)]
    for l in range(depth):
        lam_init = 0.8 - 0.6 * math.exp(-0.3 * l)
        wts = _layer_weights(l, g_attn, w_in, w_gk2, b_gk, qn_diff, kn_diff, lam_q1, lam_k1, lam_q2,
                             lam_k2, subln_diff, subln_gla, g_mem, w_mem_kv, qn_mem, kn_mem,
                             w_proj_diff, w_proj_gla, w_proj_mem, w_out, g_ffn, w_up, conv_w, conv_b,
                             w_down)
        mk, mv = _memkv(mem_prompt.reshape(bp * n_mem, d), wts["g_mem"], wts["w_mem_kv"], wts["knm_t"],
                        wts["g256"])
        mk = mk.reshape(bp, n_mem, d)
        mv = mv.reshape(bp, n_mem, d)
        xp, kp, vp, gp, cp = _group(xp, wts, lam_init, mk, mv, None, None, None,
                                    jnp.zeros((bp, CONV_W - 1, D_FF), F32), True)
        xs, ks_, vs_, gs, cs = _group(
            xs, wts, lam_init, cache_mem_k[l].reshape(bs, n_mem, d), cache_mem_v[l].reshape(bs, n_mem, d),
            cache_diff_k[l].reshape(bs, -1, d), cache_diff_v[l].reshape(bs, -1, d), state_gla[l],
            state_conv[l], False)
        vals = (kp.reshape(bp, sp, DIFF_HEADS, 2, DIFF_HD), vp.reshape(bp, sp, DIFF_HEADS, 2 * DIFF_HD),
                mk.reshape(bp, n_mem, MEM_HEADS, MEM_HD), mv.reshape(bp, n_mem, MEM_HEADS, MEM_HD), gp, cp,
                ks_.reshape(bs, ss, DIFF_HEADS, 2, DIFF_HD), vs_.reshape(bs, ss, DIFF_HEADS, 2 * DIFF_HD),
                gs, cs)
        for o, v in zip(outs, vals):
            o.append(v)
    return (xp, xs) + tuple(jnp.stack(o) for o in outs)
```

```python
import functools
import math

import jax
import jax.numpy as jnp
from jax import lax
from jax.experimental import pallas as pl
from jax.experimental.pallas import tpu as pltpu

F32 = jnp.float32
BF16 = jnp.bfloat16

D_MODEL = 1024
CHUNK = 64
EPS = 1e-6
DIFF_HEADS = 8
DIFF_HD = 64
DIFF_SCALE = DIFF_HD ** -0.5
LOG2E = math.log2(math.e)
GLA_HEADS = 4
GLA_DK = 128
GLA_DV = 256
GLA_SCALE = GLA_DK ** -0.5
GK_RANK = 16
GK_NORM = 16.0
MEM_HEADS = 4
MEM_HD = 256
MEM_SCALE = MEM_HD ** -0.5
D_FF = 2816
CONV_W = 3

LANES = 128
SUBLANES = 8
VMEM_LIMIT_BYTES = 48 * 1024 * 1024

T_QA, T_KA, T_VA, T_QKB, T_VB, T_RB, T_QM, T_GATE = 0, 1, 2, 3, 4, 5, 6, 7
N_PTILES = 10
SUB_BLOCK = 16
NEG_BIG = -1e30


def _dot(a, b):
    return jnp.dot(a, b, preferred_element_type=F32)


def _dot_nt(a, b):
    return lax.dot_general(a, b, (((1,), (1,)), ((), ())), preferred_element_type=F32)


def _sigmoid(x):
    return 1.0 / (1.0 + jnp.exp(-x))


def _pick(n, pref):
    t = min(n, pref)
    while n % t:
        t -= 1
    return t


def _rms_rows(x, gain):
    ms = jnp.mean(x * x, axis=-1, keepdims=True)
    return x * lax.rsqrt(ms + EPS) * gain


def _group_rms(y, gmat, gain):
    slab = gmat.shape[0]
    outs = []
    for c in range(y.shape[-1] // slab):
        ys = y[:, c * slab:(c + 1) * slab]
        ms = _dot((ys * ys).astype(BF16), gmat)
        outs.append(ys * lax.rsqrt(ms + EPS))
    return jnp.concatenate(outs, axis=-1) * gain


def _group_matrix(slab, group):
    r = jnp.arange(slab) // group
    return jnp.where(r[:, None] == r[None, :], 1.0 / group, 0.0).astype(BF16)


def _inproj_kernel(x_ref, g_ref, w_ref, wg1_ref, wg2_ref, bgk_ref, qn_ref, kn_ref, qmn_ref,
                   g64_ref, g256_ref, p_ref, kaf_ref, vaf_ref, gk_ref, h_scr):
    j = pl.program_id(1)

    @pl.when(j == 0)
    def _():
        h_scr[...] = _rms_rows(x_ref[...], g_ref[...]).astype(BF16)

    h = h_scr[...]
    acc = _dot(h, w_ref[...])

    @pl.when(j == T_QA)
    def _():
        p_ref[...] = (_group_rms(acc, g64_ref[...], qn_ref[...]) * (DIFF_SCALE * LOG2E)).astype(BF16)

    @pl.when(j == T_KA)
    def _():
        kn = _group_rms(acc, g64_ref[...], kn_ref[...])
        kaf_ref[...] = kn
        p_ref[...] = kn.astype(BF16)

    @pl.when(j == T_VA)
    def _():
        vaf_ref[...] = acc
        p_ref[...] = acc.astype(BF16)

    @pl.when(j == T_QKB)
    def _():
        half = acc.shape[-1] // 2
        p_ref[...] = jnp.concatenate([acc[:, :half] * GLA_SCALE, acc[:, half:]], axis=-1).astype(BF16)
        lr = _dot(h, wg1_ref[...])
        z = _dot(lr.astype(BF16), wg2_ref[...]) + bgk_ref[...]
        log_sig = jnp.minimum(z, 0.0) - jnp.log(1.0 + jnp.exp(-jnp.abs(z)))
        gk_ref[...] = log_sig * (1.0 / GK_NORM)

    @pl.when(j == T_VB)
    def _():
        p_ref[...] = acc.astype(BF16)

    @pl.when(j == T_RB)
    def _():
        p_ref[...] = (acc * _sigmoid(acc)).astype(BF16)

    @pl.when(j == T_QM)
    def _():
        p_ref[...] = (_group_rms(acc, g256_ref[...], qmn_ref[...]) * MEM_SCALE).astype(BF16)

    @pl.when(j >= T_GATE)
    def _():
        p_ref[...] = _sigmoid(acc).astype(BF16)


def _inproj(x2, g_attn, w_all, wg1, wg2, bgk, qn_t, kn_t, qmn_t, g64, g256, tm):
    t = x2.shape[0]
    d = D_MODEL
    full = lambda i, j: (0, 0)
    row = lambda i, j: (i, 0)
    return pl.pallas_call(
        _inproj_kernel,
        grid=(t // tm, N_PTILES),
        in_specs=[
            pl.BlockSpec((tm, d), row),
            pl.BlockSpec((1, d), full),
            pl.BlockSpec((d, d), lambda i, j: (0, j)),
            pl.BlockSpec(wg1.shape, full),
            pl.BlockSpec(wg2.shape, full),
            pl.BlockSpec(bgk.shape, full),
            pl.BlockSpec((1, d), full),
            pl.BlockSpec((1, d), full),
            pl.BlockSpec((1, d), full),
            pl.BlockSpec(g64.shape, full),
            pl.BlockSpec(g256.shape, full),
        ],
        out_specs=[
            pl.BlockSpec((tm, d), lambda i, j: (i, j)),
            pl.BlockSpec((tm, d), row),
            pl.BlockSpec((tm, d), row),
            pl.BlockSpec((tm, GLA_HEADS * GLA_DK), row),
        ],
        out_shape=[
            jax.ShapeDtypeStruct((t, N_PTILES * d), BF16),
            jax.ShapeDtypeStruct((t, d), F32),
            jax.ShapeDtypeStruct((t, d), F32),
            jax.ShapeDtypeStruct((t, GLA_HEADS * GLA_DK), F32),
        ],
        scratch_shapes=[pltpu.VMEM((tm, d), BF16)],
        compiler_params=pltpu.CompilerParams(
            dimension_semantics=("parallel", "arbitrary"), vmem_limit_bytes=VMEM_LIMIT_BYTES),
        name="inproj",
    )(x2, g_attn, w_all, wg1, wg2, bgk, qn_t, kn_t, qmn_t, g64, g256)


def _memkv_kernel(m_ref, g_ref, w_ref, kn_ref, g256_ref, k_ref, v_ref):
    j = pl.program_id(0)
    h = _rms_rows(m_ref[...], g_ref[...]).astype(BF16)
    acc = _dot(h, w_ref[...])

    @pl.when(j == 0)
    def _():
        k_ref[...] = _group_rms(acc, g256_ref[...], kn_ref[...])

    @pl.when(j == 1)
    def _():
        v_ref[...] = acc


def _memkv(mem2, g_mem, w_kv, knm_t, g256):
    t = mem2.shape[0]
    d = D_MODEL
    full = lambda j: (0, 0)
    return pl.pallas_call(
        _memkv_kernel,
        grid=(2,),
        in_specs=[
            pl.BlockSpec((t, d), full),
            pl.BlockSpec((1, d), full),
            pl.BlockSpec((d, d), lambda j: (0, j)),
            pl.BlockSpec((1, d), full),
            pl.BlockSpec(g256.shape, full),
        ],
        out_specs=[pl.BlockSpec((t, d), full), pl.BlockSpec((t, d), full)],
        out_shape=[jax.ShapeDtypeStruct((t, d), F32), jax.ShapeDtypeStruct((t, d), F32)],
        compiler_params=pltpu.CompilerParams(
            dimension_semantics=("arbitrary",), vmem_limit_bytes=VMEM_LIMIT_BYTES),
        name="memkv",
    )(mem2, g_mem, w_kv, knm_t, g256)


def _lambda_value(lamp, lam_init):
    a = jnp.sum(lamp[0:1, :] * lamp[1:2, :], axis=-1, keepdims=True)
    b = jnp.sum(lamp[2:3, :] * lamp[3:4, :], axis=-1, keepdims=True)
    return jnp.exp(a) - jnp.exp(b) + lam_init


def _stack_maps(q):
    lane = lax.broadcasted_iota(jnp.int32, q.shape, 1)
    zero = jnp.zeros_like(q)
    return jnp.concatenate([jnp.where(lane < DIFF_HD, q, zero), jnp.where(lane >= DIFF_HD, q, zero)], axis=0)


def _diff_finish(acc, l, lam, sub, lam_init, tq):
    o2 = acc / l
    o = o2[:tq] - lam * o2[tq:]
    return _rms_rows(o, sub) * (1.0 - lam_init)


def _diff_prompt_kernel(q_ref, k_ref, v_ref, lamp_ref, sub_ref, o_ref, qt_scr, vt_scr, m_scr, l_scr, acc_scr,
                        *, tq, blocks_per_trip, lam_init):
    qi = pl.program_id(2)
    n_kv = v_ref.shape[0] // tq

    @pl.when(qi == 0)
    def _():
        for c in range(n_kv):
            vt_scr[c] = v_ref[c * tq:(c + 1) * tq, :].astype(F32).T.astype(BF16)

    qt_scr[...] = _stack_maps(q_ref[...]).astype(F32).T.astype(BF16)
    m_scr[...] = jnp.full(m_scr.shape, -jnp.inf, F32)
    l_scr[...] = jnp.zeros(l_scr.shape, F32)
    acc_scr[...] = jnp.zeros(acc_scr.shape, F32)

    def blocks(kbs, slots, masked):
        chains = [(kb, slot, g) for kb, slot in zip(kbs, slots) for g in range(2)]
        sts = []
        for kb, slot, g in chains:
            k = k_ref[pl.ds(pl.multiple_of(kb * tq, tq), tq), :]
            sts.append(_dot(k, qt_scr[:, g * tq:(g + 1) * tq]))
        pts, alphas = [], []
        for (kb, slot, g), st in zip(chains, sts):
            cols = slice(g * tq, (g + 1) * tq)
            if masked:
                r = lax.broadcasted_iota(jnp.int32, st.shape, 0)
                c = lax.broadcasted_iota(jnp.int32, st.shape, 1)
                st = jnp.where((r // CHUNK) <= (c // CHUNK), st, -jnp.inf)
            m_prev = m_scr[slot, :, cols]
            m_new = jnp.maximum(m_prev, jnp.max(st, axis=0, keepdims=True))
            alpha = jnp.exp2(m_prev - m_new)
            pt = jnp.exp2(st - m_new)
            l_scr[slot, :, cols] = alpha * l_scr[slot, :, cols] + jnp.sum(pt, axis=0, keepdims=True)
            m_scr[slot, :, cols] = m_new
            pts.append(pt.astype(BF16))
            alphas.append(alpha)
        for (kb, slot, g), pt, alpha in zip(chains, pts, alphas):
            cols = slice(g * tq, (g + 1) * tq)
            acc_scr[slot, :, cols] = alpha * acc_scr[slot, :, cols] + _dot(vt_scr[kb], pt)

    def trip(i, carry):
        blocks([i * blocks_per_trip + u for u in range(blocks_per_trip)], list(range(blocks_per_trip)), False)
        return carry

    n_trips = qi // blocks_per_trip
    lax.fori_loop(0, n_trips, trip, 0)
    for u in range(blocks_per_trip - 1):
        @pl.when(n_trips * blocks_per_trip + u < qi)
        def _():
            blocks([n_trips * blocks_per_trip + u], [u], False)
    blocks([qi], [blocks_per_trip - 1], True)

    m_all = m_scr[0]
    for u in range(1, blocks_per_trip):
        m_all = jnp.maximum(m_all, m_scr[u])
    l_all = jnp.zeros_like(m_all)
    acc_all = jnp.zeros(acc_scr.shape[1:], F32)
    for u in range(blocks_per_trip):
        w = jnp.exp2(m_scr[u] - m_all)
        l_all = l_all + w * l_scr[u]
        acc_all = acc_all + w * acc_scr[u]

    lam = _lambda_value(lamp_ref[...], lam_init)
    o2t = acc_all / l_all
    o = (o2t[:, :tq] - lam * o2t[:, tq:]).T
    o_ref[...] = (_rms_rows(o, sub_ref[...]) * (1.0 - lam_init)).astype(o_ref.dtype)


def _diff_prompt(p3, lamp, sub, lam_init, tq, blocks_per_trip):
    b, s, _ = p3.shape
    hd2 = 2 * DIFF_HD
    kern = functools.partial(_diff_prompt_kernel, tq=tq, blocks_per_trip=blocks_per_trip, lam_init=lam_init)
    per_d = D_MODEL // hd2
    return pl.pallas_call(
        kern,
        grid=(b, DIFF_HEADS, s // tq),
        in_specs=[
            pl.BlockSpec((None, tq, hd2), lambda bi, h, qi: (bi, qi, T_QA * per_d + h)),
            pl.BlockSpec((None, s, hd2), lambda bi, h, qi: (bi, 0, T_KA * per_d + h)),
            pl.BlockSpec((None, s, hd2), lambda bi, h, qi: (bi, 0, T_VA * per_d + h)),
            pl.BlockSpec(lamp.shape, lambda bi, h, qi: (0, 0)),
            pl.BlockSpec(sub.shape, lambda bi, h, qi: (0, 0)),
        ],
        out_specs=pl.BlockSpec((None, tq, hd2), lambda bi, h, qi: (bi, qi, h)),
        out_shape=jax.ShapeDtypeStruct((b, s, D_MODEL), BF16),
        scratch_shapes=[
            pltpu.VMEM((hd2, 2 * tq), BF16),
            pltpu.VMEM((s // tq, hd2, tq), BF16),
            pltpu.VMEM((blocks_per_trip, 1, 2 * tq), F32),
            pltpu.VMEM((blocks_per_trip, 1, 2 * tq), F32),
            pltpu.VMEM((blocks_per_trip, hd2, 2 * tq), F32),
        ],
        compiler_params=pltpu.CompilerParams(
            dimension_semantics=("parallel", "parallel", "arbitrary"), vmem_limit_bytes=VMEM_LIMIT_BYTES),
        name="diff_attn_prompt",
    )(p3, p3, p3, lamp, sub)


def _diff_sample_kernel(q_ref, kn_ref, vn_ref, kc_ref, vc_ref, lamp_ref, sub_ref, o_ref, *, lam_init):
    tq = q_ref.shape[0]
    q2 = _stack_maps(q_ref[...])
    s_c = _dot_nt(q2, kc_ref[...].astype(BF16))
    s_n = _dot_nt(q2, kn_ref[...])
    m = jnp.maximum(jnp.max(s_c, axis=-1, keepdims=True), jnp.max(s_n, axis=-1, keepdims=True))
    p_c = jnp.exp2(s_c - m)
    p_n = jnp.exp2(s_n - m)
    l = jnp.sum(p_c, axis=-1, keepdims=True) + jnp.sum(p_n, axis=-1, keepdims=True)
    acc = _dot(p_c.astype(BF16), vc_ref[...].astype(BF16)) + _dot(p_n.astype(BF16), vn_ref[...])
    lam = _lambda_value(lamp_ref[...], lam_init)
    o_ref[...] = _diff_finish(acc, l, lam, sub_ref[...], lam_init, tq).astype(o_ref.dtype)


def _diff_sample(p3, kc, vc, lamp, sub, lam_init):
    b, l, _ = p3.shape
    n_past = kc.shape[1]
    hd2 = 2 * DIFF_HD
    per_d = D_MODEL // hd2
    kern = functools.partial(_diff_sample_kernel, lam_init=lam_init)
    return pl.pallas_call(
        kern,
        grid=(b, DIFF_HEADS),
        in_specs=[
            pl.BlockSpec((None, l, hd2), lambda bi, h: (bi, 0, T_QA * per_d + h)),
            pl.BlockSpec((None, l, hd2), lambda bi, h: (bi, 0, T_KA * per_d + h)),
            pl.BlockSpec((None, l, hd2), lambda bi, h: (bi, 0, T_VA * per_d + h)),
            pl.BlockSpec((None, n_past, hd2), lambda bi, h: (bi, 0, h)),
            pl.BlockSpec((None, n_past, hd2), lambda bi, h: (bi, 0, h)),
            pl.BlockSpec(lamp.shape, lambda bi, h: (0, 0)),
            pl.BlockSpec(sub.shape, lambda bi, h: (0, 0)),
        ],
        out_specs=pl.BlockSpec((None, l, hd2), lambda bi, h: (bi, 0, h)),
        out_shape=jax.ShapeDtypeStruct((b, l, D_MODEL), BF16),
        compiler_params=pltpu.CompilerParams(
            dimension_semantics=("parallel", "parallel"), vmem_limit_bytes=VMEM_LIMIT_BYTES),
        name="diff_attn_sample",
    )(p3, p3, p3, kc, vc, lamp, sub)


def _bcast_rows(x, period, row):
    r, c = x.shape
    x3 = x.reshape(r // period, period, c)
    return jnp.broadcast_to(x3[:, row:row + 1, :], x3.shape).reshape(r, c)


def _gla_kernel(*refs, rows, has_init):
    if has_init:
        q_ref, k_ref, v_ref, r_ref, g_ref, sub_ref, s0_ref, o_ref, sout_ref, st_scr, kf_scr, b_scr = refs
    else:
        q_ref, k_ref, v_ref, r_ref, g_ref, sub_ref, o_ref, sout_ref, st_scr, kf_scr, b_scr = refs
        s0_ref = None
    step = pl.program_id(2)
    n_chunks = rows // CHUNK
    n_sub = CHUNK // SUB_BLOCK

    @pl.when(step == 0)
    def _():
        if has_init:
            st_scr[...] = s0_ref[...].T
        else:
            st_scr[...] = jnp.zeros(st_scr.shape, F32)

    q = q_ref[...].astype(F32)
    k = k_ref[...].astype(F32)
    g = g_ref[...]
    v = v_ref[...]

    ri = lax.broadcasted_iota(jnp.int32, (rows, rows), 0)
    ci = lax.broadcasted_iota(jnp.int32, (rows, rows), 1)
    tri = jnp.where((ci <= ri) & ((ri // CHUNK) == (ci // CHUNK)), 1.0, 0.0).astype(BF16)
    g1 = g.astype(BF16)
    rem = g - g1.astype(F32)
    g2 = rem.astype(BF16)
    g3 = (rem - g2.astype(F32)).astype(BF16)
    b = _dot(tri, g1) + _dot(tri, g2) + _dot(tri, g3)
    bex = b - g

    b_last = _bcast_rows(b, CHUNK, CHUNK - 1)
    b_blk = _bcast_rows(bex, SUB_BLOCK, 0)
    q_blk = q * jnp.exp(b - b_blk)
    q_chk = q * jnp.exp(b)
    k_end = k * jnp.exp(b_last - b)

    rowc = lax.broadcasted_iota(jnp.int32, (rows, GLA_DK), 0) % CHUNK
    lane = lax.broadcasted_iota(jnp.int32, (rows, GLA_DK), 1)
    zero = jnp.zeros_like(q)

    lhs_parts, rhs_parts = [], []
    for blk in range(1, n_sub):
        b_ref_blk = _bcast_rows(bex, CHUNK, blk * SUB_BLOCK)
        k_blk = k * jnp.exp(jnp.where(rowc < blk * SUB_BLOCK, b_ref_blk - b, NEG_BIG))
        lhs_parts.append(jnp.where((rowc // SUB_BLOCK) == blk, q_blk, zero))
        rhs_parts.append(k_blk)
    lhs = jnp.concatenate(lhs_parts, axis=-1).astype(BF16)
    rhs = jnp.concatenate(rhs_parts, axis=-1).astype(BF16)

    pad = SUB_BLOCK
    kf_scr[0:pad, :] = jnp.zeros((pad, GLA_DK), F32)
    b_scr[0:pad, :] = jnp.zeros((pad, GLA_DK), F32)
    kf_scr[pad:pad + rows, :] = k
    b_scr[pad:pad + rows, :] = b
    ones = jnp.ones((GLA_DK, LANES), BF16)
    delta = rowc - lane
    row_sub = rowc % SUB_BLOCK
    a_diag = jnp.zeros((rows, LANES), F32)
    for d in range(SUB_BLOCK):
        kd = kf_scr[pad - d:pad - d + rows, :]
        bd = b_scr[pad - d:pad - d + rows, :]
        e = jnp.exp(jnp.where(row_sub >= d, b - bd, NEG_BIG))
        rd = _dot((q * kd * e).astype(BF16), ones)
        a_diag = a_diag + jnp.where(delta == d, rd, 0.0)

    outs = []
    for c in range(n_chunks):
        sl = slice(c * CHUNK, (c + 1) * CHUNK)
        att = _dot_nt(lhs[sl], rhs[sl]) + a_diag[sl, :CHUNK]
        st = st_scr[...]
        o = _dot(att.astype(BF16), v[sl]) + _dot_nt(q_chk[sl].astype(BF16), st.astype(BF16))
        d_st = _dot(v[sl].astype(F32).T.astype(BF16), k_end[sl].astype(BF16))
        dec = jnp.exp(b[(c + 1) * CHUNK - 1:(c + 1) * CHUNK, :])
        st_scr[...] = st * dec + d_st
        outs.append(o)
    o = jnp.concatenate(outs, axis=0) if n_chunks > 1 else outs[0]
    o_ref[...] = (_rms_rows(o, sub_ref[...]) * r_ref[...].astype(F32)).astype(o_ref.dtype)

    @pl.when(step == pl.num_programs(2) - 1)
    def _():
        sout_ref[...] = st_scr[...].T


def _gla(p3, gk3, sub, s0, rows):
    b, s, _ = p3.shape
    has_init = s0 is not None
    kq = D_MODEL // GLA_DK
    kv = D_MODEL // GLA_DV
    in_specs = [
        pl.BlockSpec((None, rows, GLA_DK), lambda bi, h, r: (bi, r, T_QKB * kq + h)),
        pl.BlockSpec((None, rows, GLA_DK), lambda bi, h, r: (bi, r, T_QKB * kq + GLA_HEADS + h)),
        pl.BlockSpec((None, rows, GLA_DV), lambda bi, h, r: (bi, r, T_VB * kv + h)),
        pl.BlockSpec((None, rows, GLA_DV), lambda bi, h, r: (bi, r, T_RB * kv + h)),
        pl.BlockSpec((None, rows, GLA_DK), lambda bi, h, r: (bi, r, h)),
        pl.BlockSpec(sub.shape, lambda bi, h, r: (0, 0)),
    ]
    args = [p3, p3, p3, p3, gk3, sub]
    if has_init:
        in_specs.append(pl.BlockSpec((None, None, GLA_DK, GLA_DV), lambda bi, h, r: (bi, h, 0, 0)))
        args.append(s0)
    kern = functools.partial(_gla_kernel, rows=rows, has_init=has_init)
    return pl.pallas_call(
        kern,
        grid=(b, GLA_HEADS, s // rows),
        in_specs=in_specs,
        out_specs=[
            pl.BlockSpec((None, rows, GLA_DV), lambda bi, h, r: (bi, r, h)),
            pl.BlockSpec((None, None, GLA_DK, GLA_DV), lambda bi, h, r: (bi, h, 0, 0)),
        ],
        out_shape=[
            jax.ShapeDtypeStruct((b, s, GLA_HEADS * GLA_DV), BF16),
            jax.ShapeDtypeStruct((b, GLA_HEADS, GLA_DK, GLA_DV), F32),
        ],
        scratch_shapes=[
            pltpu.VMEM((GLA_DV, GLA_DK), F32),
            pltpu.VMEM((SUB_BLOCK + rows, GLA_DK), F32),
            pltpu.VMEM((SUB_BLOCK + rows, GLA_DK), F32),
        ],
        compiler_params=pltpu.CompilerParams(
            dimension_semantics=("parallel", "parallel", "arbitrary"), vmem_limit_bytes=VMEM_LIMIT_BYTES),
        name="gla",
    )(*args)


def _mem_attn_kernel(q_ref, k_ref, v_ref, o_ref):
    outs = []
    for h in range(MEM_HEADS):
        sl = slice(h * MEM_HD, (h + 1) * MEM_HD)
        s = _dot_nt(q_ref[:, sl], k_ref[:, sl].astype(BF16))
        m = jnp.max(s, axis=-1, keepdims=True)
        p = jnp.exp(s - m)
        l = jnp.sum(p, axis=-1, keepdims=True)
        outs.append(_dot(p.astype(BF16), v_ref[:, sl].astype(BF16)) / l)
    o_ref[...] = jnp.concatenate(outs, axis=-1).astype(o_ref.dtype)


def _mem_attn(p3, mk, mv, tq):
    b, s, _ = p3.shape
    m = mk.shape[1]
    d = D_MODEL
    return pl.pallas_call(
        _mem_attn_kernel,
        grid=(b, s // tq),
        in_specs=[
            pl.BlockSpec((None, tq, d), lambda bi, i: (bi, i, T_QM)),
            pl.BlockSpec((None, m, d), lambda bi, i: (bi, 0, 0)),
            pl.BlockSpec((None, m, d), lambda bi, i: (bi, 0, 0)),
        ],
        out_specs=pl.BlockSpec((None, tq, d), lambda bi, i: (bi, i, 0)),
        out_shape=jax.ShapeDtypeStruct((b, s, d), BF16),
        compiler_params=pltpu.CompilerParams(
            dimension_semantics=("parallel", "parallel"), vmem_limit_bytes=VMEM_LIMIT_BYTES),
        name="mem_attn",
    )(p3, mk, mv)


def _mix_kernel(x_ref, oa_ref, ob_ref, om_ref, ga_ref, gb_ref, gm_ref, wd_ref, wg_ref, wm_ref, wo_ref,
                y_ref):
    m = (ga_ref[...].astype(F32) * _dot(oa_ref[...], wd_ref[...])
         + gb_ref[...].astype(F32) * _dot(ob_ref[...], wg_ref[...])
         + gm_ref[...].astype(F32) * _dot(om_ref[...], wm_ref[...]))
    y_ref[...] = x_ref[...] + _dot(m.astype(BF16), wo_ref[...])


def _mix(x2, oa, ob, om, p2, wd, wg, wm, wo, tm):
    t = x2.shape[0]
    d = D_MODEL
    row = lambda i: (i, 0)
    full = lambda i: (0, 0)
    wspec = pl.BlockSpec((d, d), full)
    return pl.pallas_call(
        _mix_kernel,
        grid=(t // tm,),
        in_specs=[
            pl.BlockSpec((tm, d), row), pl.BlockSpec((tm, d), row), pl.BlockSpec((tm, d), row),
            pl.BlockSpec((tm, d), row),
            pl.BlockSpec((tm, d), lambda i: (i, T_GATE)),
            pl.BlockSpec((tm, d), lambda i: (i, T_GATE + 1)),
            pl.BlockSpec((tm, d), lambda i: (i, T_GATE + 2)),
            wspec, wspec, wspec, wspec,
        ],
        out_specs=pl.BlockSpec((tm, d), row),
        out_shape=jax.ShapeDtypeStruct((t, d), F32),
        compiler_params=pltpu.CompilerParams(
            dimension_semantics=("parallel",), vmem_limit_bytes=VMEM_LIMIT_BYTES),
        name="mix_out",
    )(x2, oa, ob, om, p2, p2, p2, wd, wg, wm, wo)


def _ffn_kernel(x_ref, g_ref, wu_ref, wv_ref, cw_ref, cb_ref, wd_ref, cs_ref, y_ref, cso_ref,
                h_scr, acc_scr, u_scr, carry_scr, *, n_seq, seq_rows, tiles_per_seq):
    i = pl.program_id(0)
    f = pl.program_id(1)
    nf = pl.num_programs(1)
    tf = wu_ref.shape[1]
    gap = SUBLANES
    stride = seq_rows + gap
    tail = CONV_W - 1

    @pl.when(f == 0)
    def _():
        h_scr[...] = _rms_rows(x_ref[...], g_ref[...]).astype(BF16)
        acc_scr[...] = jnp.zeros(acc_scr.shape, F32)

    h = h_scr[...]
    u = _dot(h, wu_ref[...])
    vv = _dot(h, wv_ref[...])
    cw = cw_ref[...]
    cb = cb_ref[...]

    first = (i % tiles_per_seq) == 0
    for s in range(n_seq):
        base = s * stride

        @pl.when(first)
        def _():
            u_scr[base:base + gap, :] = jnp.zeros((gap, tf), F32)
            u_scr[base + gap - tail:base + gap, :] = cs_ref[s]

        @pl.when(jnp.logical_not(first))
        def _():
            u_scr[base:base + gap, :] = carry_scr[f]

        u_scr[base + gap:base + gap + seq_rows, :] = u[s * seq_rows:(s + 1) * seq_rows]

    outs = []
    for s in range(n_seq):
        base = s * stride + gap
        conv = cb
        for j in range(CONV_W):
            off = base - tail + j
            conv = conv + cw[j:j + 1, :] * u_scr[off:off + seq_rows, :]
        outs.append(conv)
        cso_ref[s] = u_scr[base + seq_rows - tail:base + seq_rows, :]
    uc = jnp.concatenate(outs, axis=0) if n_seq > 1 else outs[0]
    carry_scr[f] = u_scr[seq_rows:seq_rows + gap, :]

    gelu = 0.5 * uc * (1.0 + jnp.tanh(math.sqrt(2.0 / math.pi) * (uc + 0.044715 * (uc * uc * uc))))
    acc_scr[...] += _dot((gelu * vv).astype(BF16), wd_ref[...])

    @pl.when(f == nf - 1)
    def _():
        y_ref[...] = x_ref[...] + acc_scr[...]


def _ffn(x2, g_ffn, w_up, conv_w, conv_b, w_down, conv_state, n_seq, seq_rows, tiles_per_seq, tf):
    t = x2.shape[0]
    d = D_MODEL
    tm = n_seq * seq_rows
    nf = D_FF // tf
    nb = conv_state.shape[0]
    tail = CONV_W - 1
    kern = functools.partial(_ffn_kernel, n_seq=n_seq, seq_rows=seq_rows, tiles_per_seq=tiles_per_seq)
    seq_blk = lambda i, f: ((i // tiles_per_seq), 0, f)
    y, tails = pl.pallas_call(
        kern,
        grid=(t // tm, nf),
        in_specs=[
            pl.BlockSpec((tm, d), lambda i, f: (i, 0)),
            pl.BlockSpec((1, d), lambda i, f: (0, 0)),
            pl.BlockSpec((d, tf), lambda i, f: (0, f)),
            pl.BlockSpec((d, tf), lambda i, f: (0, nf + f)),
            pl.BlockSpec((CONV_W, tf), lambda i, f: (0, f)),
            pl.BlockSpec((1, tf), lambda i, f: (0, f)),
            pl.BlockSpec((tf, d), lambda i, f: (f, 0)),
            pl.BlockSpec((n_seq, tail, tf), seq_blk),
        ],
        out_specs=[
            pl.BlockSpec((tm, d), lambda i, f: (i, 0)),
            pl.BlockSpec((n_seq, tail, tf), lambda i, f: (i, 0, f)),
        ],
        out_shape=[
            jax.ShapeDtypeStruct((t, d), F32),
            jax.ShapeDtypeStruct((nb * tiles_per_seq, tail, D_FF), F32),
        ],
        scratch_shapes=[
            pltpu.VMEM((tm, d), BF16),
            pltpu.VMEM((tm, d), F32),
            pltpu.VMEM((n_seq * (seq_rows + SUBLANES), tf), F32),
            pltpu.VMEM((nf, SUBLANES, tf), F32),
        ],
        compiler_params=pltpu.CompilerParams(
            dimension_semantics=("arbitrary", "arbitrary"), vmem_limit_bytes=VMEM_LIMIT_BYTES),
        name="conv_ffn",
    )(x2, g_ffn, w_up, w_up, conv_w, conv_b, w_down, conv_state)
    return y, tails.reshape(nb, tiles_per_seq, tail, D_FF)[:, -1]


def _tile_gain(g, reps):
    return jnp.tile(g.astype(F32), reps).reshape(1, -1)


def _layer_weights(l, g_attn, w_in, w_gk2, b_gk, qn_diff, kn_diff, lam_q1, lam_k1, lam_q2, lam_k2,
                   subln_diff, subln_gla, g_mem, w_mem_kv, qn_mem, kn_mem, w_proj_diff, w_proj_gla,
                   w_proj_mem, w_out, g_ffn, w_up, conv_w, conv_b, w_down):
    d = D_MODEL
    w = w_in[l]
    lr0 = 6 * d
    w_all = jnp.concatenate([w[:, :lr0], w[:, lr0 + GK_RANK:]], axis=1).astype(BF16)
    wg1 = jnp.pad(w[:, lr0:lr0 + GK_RANK], ((0, 0), (0, LANES - GK_RANK))).astype(BF16)
    wg2 = jnp.pad(w_gk2[l], ((0, LANES - GK_RANK), (0, 0))).astype(BF16)
    return dict(
        g_attn=g_attn[l].reshape(1, d), w_all=w_all, wg1=wg1, wg2=wg2, bgk=b_gk[l].reshape(1, -1),
        qn_t=_tile_gain(qn_diff[l], d // DIFF_HD), kn_t=_tile_gain(kn_diff[l], d // DIFF_HD),
        qmn_t=_tile_gain(qn_mem[l], d // MEM_HD), knm_t=_tile_gain(kn_mem[l], d // MEM_HD),
        g64=_group_matrix(LANES, DIFF_HD), g256=_group_matrix(MEM_HD, MEM_HD),
        lamp=jnp.stack([lam_q1[l], lam_k1[l], lam_q2[l], lam_k2[l]]).astype(F32),
        sub_diff=subln_diff[l].reshape(1, -1), sub_gla=subln_gla[l].reshape(1, -1),
        g_mem=g_mem[l].reshape(1, d), w_mem_kv=w_mem_kv[l].astype(BF16),
        wd=w_proj_diff[l].astype(BF16), wg=w_proj_gla[l].astype(BF16), wm=w_proj_mem[l].astype(BF16),
        wo=w_out[l].astype(BF16), g_ffn=g_ffn[l].reshape(1, d), w_up=w_up[l].astype(BF16),
        conv_w=conv_w[l], conv_b=conv_b[l].reshape(1, -1), w_down=w_down[l].astype(BF16),
    )


def _group(x, wts, lam_init, mem_k, mem_v, past_k, past_v, gla_state, conv_state, prompt):
    b, s, d = x.shape
    t = b * s
    x2 = x.reshape(t, d)
    tm = _pick(t, 512)
    p2, ka, va, gk = _inproj(x2, wts["g_attn"], wts["w_all"], wts["wg1"], wts["wg2"], wts["bgk"],
                             wts["qn_t"], wts["kn_t"], wts["qmn_t"], wts["g64"], wts["g256"], tm)
    p3 = p2.reshape(b, s, N_PTILES * d)
    gk3 = gk.reshape(b, s, GLA_HEADS * GLA_DK)
    if prompt:
        oa = _diff_prompt(p3, wts["lamp"], wts["sub_diff"], lam_init, _pick(s, 256), 2)
        ob, gla_new = _gla(p3, gk3, wts["sub_gla"], None, _pick(s, 512))
    else:
        oa = _diff_sample(p3, past_k, past_v, wts["lamp"], wts["sub_diff"], lam_init)
        ob, gla_new = _gla(p3, gk3, wts["sub_gla"], gla_state, s)
    om = _mem_attn(p3, mem_k, mem_v, _pick(s, 512))
    x1 = _mix(x2, oa.reshape(t, d), ob.reshape(t, d), om.reshape(t, d), p2,
              wts["wd"], wts["wg"], wts["wm"], wts["wo"], tm)
    tf = 256
    if prompt:
        rows = _pick(s, 1024)
        y, cs = _ffn(x1, wts["g_ffn"], wts["w_up"], wts["conv_w"], wts["conv_b"], wts["w_down"],
                     conv_state, 1, rows, s // rows, tf)
    else:
        y, cs = _ffn(x1, wts["g_ffn"], wts["w_up"], wts["conv_w"], wts["conv_b"], wts["w_down"],
                     conv_state, b, s, 1, tf)
    return y.reshape(b, s, d), ka, va, gla_new, cs


def kernel(x_prompt, x_sample, mem_prompt, cache_diff_k, cache_diff_v, cache_mem_k, cache_mem_v, state_gla, state_conv, g_attn, w_in, w_gk2, b_gk, qn_diff, kn_diff, lam_q1, lam_k1, lam_q2, lam_k2, subln_diff, subln_gla, g_mem, w_mem_kv, qn_mem, kn_mem, w_proj_diff, w_proj_gla, w_proj_mem, w_out, g_ffn, w_up, conv_w, conv_b, w_down):
    depth = g_attn.shape[0]
    d = D_MODEL
    xp, xs = x_prompt, x_sample
    bp, sp, _ = xp.shape
    bs, ss, _ = xs.shape
    n_mem = mem_prompt.shape[1]
    outs = [[] for _ in range(10)]
    for l in range(depth):
        lam_init = 0.8 - 0.6 * math.exp(-0.3 * l)
        wts = _layer_weights(l, g_attn, w_in, w_gk2, b_gk, qn_diff, kn_diff, lam_q1, lam_k1, lam_q2,
                             lam_k2, subln_diff, subln_gla, g_mem, w_mem_kv, qn_mem, kn_mem,
                             w_proj_diff, w_proj_gla, w_proj_mem, w_out, g_ffn, w_up, conv_w, conv_b,
                             w_down)
        mk, mv = _memkv(mem_prompt.reshape(bp * n_mem, d), wts["g_mem"], wts["w_mem_kv"], wts["knm_t"],
                        wts["g256"])
        mk = mk.reshape(bp, n_mem, d)
        mv = mv.reshape(bp, n_mem, d)
        xp, kp, vp, gp, cp = _group(xp, wts, lam_init, mk, mv, None, None, None,
                                    jnp.zeros((bp, CONV_W - 1, D_FF), F32), True)
        xs, ks_, vs_, gs, cs = _group(
            xs, wts, lam_init, cache_mem_k[l].reshape(bs, n_mem, d), cache_mem_v[l].reshape(bs, n_mem, d),
            cache_diff_k[l].reshape(bs, -1, d), cache_diff_v[l].reshape(bs, -1, d), state_gla[l],
            state_conv[l], False)
        vals = (kp.reshape(bp, sp, DIFF_HEADS, 2, DIFF_HD), vp.reshape(bp, sp, DIFF_HEADS, 2 * DIFF_HD),
                mk.reshape(bp, n_mem, MEM_HEADS, MEM_HD), mv.reshape(bp, n_mem, MEM_HEADS, MEM_HD), gp, cp,
                ks_.reshape(bs, ss, DIFF_HEADS, 2, DIFF_HD), vs_.reshape(bs, ss, DIFF_HEADS, 2 * DIFF_HD),
                gs, cs)
        for o, v in zip(outs, vals):
            o.append(v)
    return (xp, xs) + tuple(jnp.stack(o) for o in outs)
```

```python
import functools
import math

import jax
import jax.numpy as jnp
from jax import lax
from jax.experimental import pallas as pl
from jax.experimental.pallas import tpu as pltpu

F32 = jnp.float32
BF16 = jnp.bfloat16

D_MODEL = 1024
CHUNK = 64
EPS = 1e-6
DIFF_HEADS = 8
DIFF_HD = 64
DIFF_SCALE = DIFF_HD ** -0.5
LOG2E = math.log2(math.e)
GLA_HEADS = 4
GLA_DK = 128
GLA_DV = 256
GLA_SCALE = GLA_DK ** -0.5
GK_RANK = 16
GK_NORM = 16.0
MEM_HEADS = 4
MEM_HD = 256
MEM_SCALE = MEM_HD ** -0.5
D_FF = 2816
CONV_W = 3

LANES = 128
SUBLANES = 8
VMEM_LIMIT_BYTES = 48 * 1024 * 1024

T_QA, T_KA, T_VA, T_QKB, T_VB, T_RB, T_QM, T_GATE = 0, 1, 2, 3, 4, 5, 6, 7
N_PTILES = 10
SUB_BLOCK = 16
NEG_BIG = -1e30


def _dot(a, b):
    return jnp.dot(a, b, preferred_element_type=F32)


def _dot_nt(a, b):
    return lax.dot_general(a, b, (((1,), (1,)), ((), ())), preferred_element_type=F32)


def _sigmoid(x):
    return 1.0 / (1.0 + jnp.exp(-x))


def _pick(n, pref):
    t = min(n, pref)
    while n % t:
        t -= 1
    return t


def _rms_rows(x, gain):
    ms = jnp.mean(x * x, axis=-1, keepdims=True)
    return x * lax.rsqrt(ms + EPS) * gain


def _group_rms(y, gmat, gain):
    slab = gmat.shape[0]
    outs = []
    for c in range(y.shape[-1] // slab):
        ys = y[:, c * slab:(c + 1) * slab]
        ms = _dot((ys * ys).astype(BF16), gmat)
        outs.append(ys * lax.rsqrt(ms + EPS))
    return jnp.concatenate(outs, axis=-1) * gain


def _group_matrix(slab, group):
    r = jnp.arange(slab) // group
    return jnp.where(r[:, None] == r[None, :], 1.0 / group, 0.0).astype(BF16)


def _inproj_kernel(x_ref, g_ref, wa_ref, wb_ref, wg1_ref, wg2_ref, bgk_ref, qn_ref, kn_ref, qmn_ref,
                   g64_ref, g256_ref, p_ref, ka_ref, vaf_ref, gk_ref, *, k_transposed):
    d = D_MODEL
    n_a = wa_ref.shape[1] // d
    h = _rms_rows(x_ref[...], g_ref[...]).astype(BF16)

    def project(j):
        if j < n_a:
            return _dot(h, wa_ref[:, j * d:(j + 1) * d])
        return _dot(h, wb_ref[:, (j - n_a) * d:(j - n_a + 1) * d])

    def finish(j, acc):
        if j == T_QA:
            out = _group_rms(acc, g64_ref[...], qn_ref[...]) * (DIFF_SCALE * LOG2E)
        elif j == T_KA:
            out = _group_rms(acc, g64_ref[...], kn_ref[...])
            ka_ref[...] = out.T if k_transposed else out
        elif j == T_VA:
            vaf_ref[...] = acc
            out = acc
        elif j == T_QKB:
            half = d // 2
            out = jnp.concatenate([acc[:, :half] * GLA_SCALE, acc[:, half:]], axis=-1)
            lr = _dot(h, wg1_ref[...])
            z = _dot(lr.astype(BF16), wg2_ref[...]) + bgk_ref[...]
            log_sig = jnp.minimum(z, 0.0) - jnp.log(1.0 + jnp.exp(-jnp.abs(z)))
            gk_ref[...] = log_sig * (1.0 / GK_NORM)
        elif j == T_VB:
            out = acc
        elif j == T_RB:
            out = acc * _sigmoid(acc)
        elif j == T_QM:
            out = _group_rms(acc, g256_ref[...], qmn_ref[...]) * MEM_SCALE
        else:
            out = _sigmoid(acc)
        p_ref[:, j * d:(j + 1) * d] = out.astype(BF16)

    acc_next = project(0)
    for j in range(N_PTILES):
        acc = acc_next
        if j + 1 < N_PTILES:
            acc_next = project(j + 1)
        finish(j, acc)


def _resident(shape):
    return pl.BlockSpec(shape, lambda *_: (0,) * len(shape), pipeline_mode=pl.Buffered(1))


def _inproj(x2, g_attn, wa, wb, wg1, wg2, bgk, qn_t, kn_t, qmn_t, g64, g256, tm, seq_len, k_transposed):
    t = x2.shape[0]
    d = D_MODEL
    row = lambda i: (i, 0)
    if k_transposed:
        per_seq = seq_len // tm
        ka_spec = pl.BlockSpec((None, d, tm), lambda i: (i // per_seq, 0, i % per_seq))
        ka_shape = jax.ShapeDtypeStruct((t // seq_len, d, seq_len), F32)
    else:
        ka_spec = pl.BlockSpec((tm, d), row)
        ka_shape = jax.ShapeDtypeStruct((t, d), F32)
    small = [wg1, wg2, bgk, qn_t, kn_t, qmn_t, g64, g256]
    return pl.pallas_call(
        functools.partial(_inproj_kernel, k_transposed=k_transposed),
        grid=(t // tm,),
        in_specs=[pl.BlockSpec((tm, d), row), _resident((1, d)), _resident(wa.shape), _resident(wb.shape)]
        + [_resident(a.shape) for a in small],
        out_specs=[
            pl.BlockSpec((tm, N_PTILES * d), row),
            ka_spec,
            pl.BlockSpec((tm, d), row),
            pl.BlockSpec((tm, GLA_HEADS * GLA_DK), row),
        ],
        out_shape=[
            jax.ShapeDtypeStruct((t, N_PTILES * d), BF16),
            ka_shape,
            jax.ShapeDtypeStruct((t, d), F32),
            jax.ShapeDtypeStruct((t, GLA_HEADS * GLA_DK), F32),
        ],
        compiler_params=pltpu.CompilerParams(
            dimension_semantics=("parallel",), vmem_limit_bytes=VMEM_LIMIT_BYTES),
        name="inproj",
    )(x2, g_attn, wa, wb, *small)


def _memkv_kernel(m_ref, g_ref, w_ref, kn_ref, g256_ref, k_ref, v_ref):
    j = pl.program_id(0)
    h = _rms_rows(m_ref[...], g_ref[...]).astype(BF16)
    acc = _dot(h, w_ref[...])

    @pl.when(j == 0)
    def _():
        k_ref[...] = _group_rms(acc, g256_ref[...], kn_ref[...])

    @pl.when(j == 1)
    def _():
        v_ref[...] = acc


def _memkv(mem2, g_mem, w_kv, knm_t, g256):
    t = mem2.shape[0]
    d = D_MODEL
    full = lambda j: (0, 0)
    return pl.pallas_call(
        _memkv_kernel,
        grid=(2,),
        in_specs=[
            pl.BlockSpec((t, d), full),
            pl.BlockSpec((1, d), full),
            pl.BlockSpec((d, d), lambda j: (0, j)),
            pl.BlockSpec((1, d), full),
            pl.BlockSpec(g256.shape, full),
        ],
        out_specs=[pl.BlockSpec((t, d), full), pl.BlockSpec((t, d), full)],
        out_shape=[jax.ShapeDtypeStruct((t, d), F32), jax.ShapeDtypeStruct((t, d), F32)],
        compiler_params=pltpu.CompilerParams(
            dimension_semantics=("arbitrary",), vmem_limit_bytes=VMEM_LIMIT_BYTES),
        name="memkv",
    )(mem2, g_mem, w_kv, knm_t, g256)


def _lambda_value(lamp, lam_init):
    a = jnp.sum(lamp[0:1, :] * lamp[1:2, :], axis=-1, keepdims=True)
    b = jnp.sum(lamp[2:3, :] * lamp[3:4, :], axis=-1, keepdims=True)
    return jnp.exp(a) - jnp.exp(b) + lam_init


def _stack_maps(q):
    lane = lax.broadcasted_iota(jnp.int32, q.shape, 1)
    zero = jnp.zeros_like(q)
    return jnp.concatenate([jnp.where(lane < DIFF_HD, q, zero), jnp.where(lane >= DIFF_HD, q, zero)], axis=0)


def _diff_finish(acc, l, lam, sub, lam_init, tq):
    o2 = acc / l
    o = o2[:tq] - lam * o2[tq:]
    return _rms_rows(o, sub) * (1.0 - lam_init)


ONES_ROWS = 16


def _diff_prompt_kernel(q_ref, k_ref, v_ref, lamp_ref, sub_ref, o_ref, qt_scr, vt_scr, m_scr, acc_scr,
                        st0_scr, st1_scr, p0_scr, p1_scr, alpha0_scr, alpha1_scr,
                        *, tq, blocks_per_trip, lam_init):
    qi = pl.program_id(2)
    n_kv = v_ref.shape[0] // tq
    hd2 = v_ref.shape[1]
    slots = list(range(blocks_per_trip))

    @pl.when(qi == 0)
    def _():
        for c in range(n_kv):
            vt_scr[c, 0:hd2, :] = v_ref[c * tq:(c + 1) * tq, :].astype(F32).T.astype(BF16)
            vt_scr[c, hd2:hd2 + ONES_ROWS, :] = jnp.ones((ONES_ROWS, tq), BF16)

    qt_scr[...] = _stack_maps(q_ref[...]).astype(F32).T.astype(BF16)
    m_scr[...] = jnp.full(m_scr.shape, -jnp.inf, F32)
    acc_scr[...] = jnp.zeros(acc_scr.shape, F32)

    def scores(kbs):
        return [_dot(k_ref[pl.ds(pl.multiple_of(kb * tq, tq), tq), :], qt_scr[:, g * tq:(g + 1) * tq])
                for kb in kbs for g in range(2)]

    def softmax(sts, blk_slots, masked):
        pts, alphas = [], []
        for i, st in enumerate(sts):
            slot, g = blk_slots[i // 2], i % 2
            cols = slice(g * tq, (g + 1) * tq)
            if masked[i // 2]:
                r = lax.broadcasted_iota(jnp.int32, st.shape, 0)
                c = lax.broadcasted_iota(jnp.int32, st.shape, 1)
                st = jnp.where((r // CHUNK) <= (c // CHUNK), st, -jnp.inf)
            m_prev = m_scr[slot, :, cols]
            m_new = jnp.maximum(m_prev, jnp.max(st, axis=0, keepdims=True))
            alphas.append(jnp.exp2(m_prev - m_new))
            pts.append(jnp.exp2(st - m_new).astype(BF16))
            m_scr[slot, :, cols] = m_new
        return pts, alphas

    def accumulate(kbs, blk_slots, pts, alphas):
        for i, (pt, alpha) in enumerate(zip(pts, alphas)):
            slot, g = blk_slots[i // 2], i % 2
            cols = slice(g * tq, (g + 1) * tq)
            acc_scr[slot, :, cols] = alpha * acc_scr[slot, :, cols] + _dot(vt_scr[kbs[i // 2]], pt)

    def group(t):
        return [jnp.minimum(t * blocks_per_trip + u, n_kv - 1) for u in slots]

    n_chains = 2 * blocks_per_trip
    chain_ids = list(range(n_chains))

    st_scr, p_scr, alpha_scr = (st0_scr, st1_scr), (p0_scr, p1_scr), (alpha0_scr, alpha1_scr)

    def put(scr, buf, vals):
        for c, val in zip(chain_ids, vals):
            scr[buf][c] = val

    def get(scr, buf):
        return [scr[buf][c] for c in chain_ids]

    def trip(t, par):
        put(st_scr, 1 - par, scores(group(t + 1)))
        accumulate(group(jnp.maximum(t - 1, 0)), slots, get(p_scr, 1 - par), get(alpha_scr, 1 - par))
        pts, alphas = softmax(get(st_scr, par), slots, [False] * blocks_per_trip)
        put(p_scr, par, pts)
        put(alpha_scr, par, alphas)

    def two_trips(i, carry):
        trip(2 * i, 0)
        trip(2 * i + 1, 1)
        return carry

    n_trips = qi // blocks_per_trip
    put(st_scr, 0, scores(group(0)))
    put(p_scr, 1, [jnp.zeros((tq, tq), BF16)] * n_chains)
    put(alpha_scr, 1, [jnp.ones((1, tq), F32)] * n_chains)
    lax.fori_loop(0, n_trips // 2, two_trips, 0)

    @pl.when(n_trips % 2 == 1)
    def _():
        trip(n_trips - 1, 0)

    first_left = n_trips * blocks_per_trip
    last_group = group(jnp.maximum(n_trips - 1, 0))
    for last_par in range(2):
        for n_left in range(blocks_per_trip):
            @pl.when(((n_trips + 1) % 2 == last_par) & (qi - first_left == n_left))
            def _():
                kbs = [first_left + u for u in range(n_left + 1)]
                sts = get(st_scr, 1 - last_par)[:2 * (n_left + 1)]
                accumulate(last_group, slots, get(p_scr, last_par), get(alpha_scr, last_par))
                pts, alphas = softmax(sts, slots[:n_left + 1], [False] * n_left + [True])
                accumulate(kbs, slots[:n_left + 1], pts, alphas)

    m_all = m_scr[0]
    for u in slots[1:]:
        m_all = jnp.maximum(m_all, m_scr[u])
    acc_all = jnp.zeros(acc_scr.shape[1:], F32)
    for u in slots:
        acc_all = acc_all + jnp.exp2(m_scr[u] - m_all) * acc_scr[u]

    lam = _lambda_value(lamp_ref[...], lam_init)
    o2t = acc_all[0:hd2, :] / acc_all[hd2:hd2 + 1, :]
    o = (o2t[:, :tq] - lam * o2t[:, tq:]).T
    o_ref[...] = (_rms_rows(o, sub_ref[...]) * (1.0 - lam_init)).astype(o_ref.dtype)


def _diff_prompt(p3, lamp, sub, lam_init, tq, blocks_per_trip):
    b, s, _ = p3.shape
    hd2 = 2 * DIFF_HD
    kern = functools.partial(_diff_prompt_kernel, tq=tq, blocks_per_trip=blocks_per_trip, lam_init=lam_init)
    per_d = D_MODEL // hd2
    return pl.pallas_call(
        kern,
        grid=(b, DIFF_HEADS, s // tq),
        in_specs=[
            pl.BlockSpec((None, tq, hd2), lambda bi, h, qi: (bi, qi, T_QA * per_d + h)),
            pl.BlockSpec((None, s, hd2), lambda bi, h, qi: (bi, 0, T_KA * per_d + h)),
            pl.BlockSpec((None, s, hd2), lambda bi, h, qi: (bi, 0, T_VA * per_d + h)),
            pl.BlockSpec(lamp.shape, lambda bi, h, qi: (0, 0)),
            pl.BlockSpec(sub.shape, lambda bi, h, qi: (0, 0)),
        ],
        out_specs=pl.BlockSpec((None, tq, hd2), lambda bi, h, qi: (bi, qi, h)),
        out_shape=jax.ShapeDtypeStruct((b, s, D_MODEL), BF16),
        scratch_shapes=[
            pltpu.VMEM((hd2, 2 * tq), BF16),
            pltpu.VMEM((s // tq, hd2 + ONES_ROWS, tq), BF16),
            pltpu.VMEM((blocks_per_trip, 1, 2 * tq), F32),
            pltpu.VMEM((blocks_per_trip, hd2 + ONES_ROWS, 2 * tq), F32),
            pltpu.VMEM((2 * blocks_per_trip, tq, tq), F32),
            pltpu.VMEM((2 * blocks_per_trip, tq, tq), F32),
            pltpu.VMEM((2 * blocks_per_trip, tq, tq), BF16),
            pltpu.VMEM((2 * blocks_per_trip, tq, tq), BF16),
            pltpu.VMEM((2 * blocks_per_trip, 1, tq), F32),
            pltpu.VMEM((2 * blocks_per_trip, 1, tq), F32),
        ],
        compiler_params=pltpu.CompilerParams(
            dimension_semantics=("parallel", "parallel", "arbitrary"), vmem_limit_bytes=VMEM_LIMIT_BYTES),
        name="diff_attn_prompt",
    )(p3, p3, p3, lamp, sub)


def _diff_sample_kernel(q_ref, kn_ref, vn_ref, kc_ref, vc_ref, lamp_ref, sub_ref, o_ref, *, lam_init):
    tq = q_ref.shape[0]
    q2 = _stack_maps(q_ref[...])
    s_c = _dot(q2, kc_ref[...].astype(BF16))
    s_n = _dot_nt(q2, kn_ref[...])
    m = jnp.maximum(jnp.max(s_c, axis=-1, keepdims=True), jnp.max(s_n, axis=-1, keepdims=True))
    p_c = jnp.exp2(s_c - m)
    p_n = jnp.exp2(s_n - m)
    l = jnp.sum(p_c, axis=-1, keepdims=True) + jnp.sum(p_n, axis=-1, keepdims=True)
    n_past = s_c.shape[1]
    vc = vc_ref[pl.ds(pl.program_id(1), n_past, stride=DIFF_HEADS), :]
    acc = _dot(p_c.astype(BF16), vc.astype(BF16)) + _dot(p_n.astype(BF16), vn_ref[...])
    lam = _lambda_value(lamp_ref[...], lam_init)
    o_ref[...] = _diff_finish(acc, l, lam, sub_ref[...], lam_init, tq).astype(o_ref.dtype)


def _diff_sample(p3, kc_t, vc, lamp, sub, lam_init):
    b, l, _ = p3.shape
    n_past = kc_t.shape[2]
    hd2 = 2 * DIFF_HD
    per_d = D_MODEL // hd2
    kern = functools.partial(_diff_sample_kernel, lam_init=lam_init)
    return pl.pallas_call(
        kern,
        grid=(b, DIFF_HEADS),
        in_specs=[
            pl.BlockSpec((None, l, hd2), lambda bi, h: (bi, 0, T_QA * per_d + h)),
            pl.BlockSpec((None, l, hd2), lambda bi, h: (bi, 0, T_KA * per_d + h)),
            pl.BlockSpec((None, l, hd2), lambda bi, h: (bi, 0, T_VA * per_d + h)),
            pl.BlockSpec((None, hd2, n_past), lambda bi, h: (bi, h, 0)),
            pl.BlockSpec((None, n_past * DIFF_HEADS, hd2), lambda bi, h: (bi, 0, 0)),
            pl.BlockSpec(lamp.shape, lambda bi, h: (0, 0)),
            pl.BlockSpec(sub.shape, lambda bi, h: (0, 0)),
        ],
        out_specs=pl.BlockSpec((None, l, hd2), lambda bi, h: (bi, 0, h)),
        out_shape=jax.ShapeDtypeStruct((b, l, D_MODEL), BF16),
        compiler_params=pltpu.CompilerParams(
            dimension_semantics=("parallel", "parallel"), vmem_limit_bytes=VMEM_LIMIT_BYTES),
        name="diff_attn_sample",
    )(p3, p3, p3, kc_t, vc, lamp, sub)


def _bcast_rows(x, period, row):
    r, c = x.shape
    x3 = x.reshape(r // period, period, c)
    return jnp.broadcast_to(x3[:, row:row + 1, :], x3.shape).reshape(r, c)


def _gla_kernel(*refs, rows, has_init):
    if has_init:
        q_ref, k_ref, v_ref, r_ref, g_ref, sub_ref, s0_ref, o_ref, sout_ref, st_scr, kf_scr, b_scr = refs
    else:
        q_ref, k_ref, v_ref, r_ref, g_ref, sub_ref, o_ref, sout_ref, st_scr, kf_scr, b_scr = refs
        s0_ref = None
    step = pl.program_id(2)
    n_chunks = rows // CHUNK
    n_sub = CHUNK // SUB_BLOCK

    @pl.when(step == 0)
    def _():
        if has_init:
            st_scr[...] = s0_ref[...].T
        else:
            st_scr[...] = jnp.zeros(st_scr.shape, F32)

    q = q_ref[...].astype(F32)
    k = k_ref[...].astype(F32)
    g = g_ref[...]
    v = v_ref[...]

    ri = lax.broadcasted_iota(jnp.int32, (rows, rows), 0)
    ci = lax.broadcasted_iota(jnp.int32, (rows, rows), 1)
    tri = jnp.where((ci <= ri) & ((ri // CHUNK) == (ci // CHUNK)), 1.0, 0.0).astype(BF16)
    g1 = g.astype(BF16)
    rem = g - g1.astype(F32)
    g2 = rem.astype(BF16)
    g3 = (rem - g2.astype(F32)).astype(BF16)
    b = _dot(tri, g1) + _dot(tri, g2) + _dot(tri, g3)
    bex = b - g

    b_last = _bcast_rows(b, CHUNK, CHUNK - 1)
    b_blk = _bcast_rows(bex, SUB_BLOCK, 0)
    q_blk = q * jnp.exp(b - b_blk)
    q_chk = q * jnp.exp(b)
    k_end = k * jnp.exp(b_last - b)

    rowc = lax.broadcasted_iota(jnp.int32, (rows, GLA_DK), 0) % CHUNK
    lane = lax.broadcasted_iota(jnp.int32, (rows, GLA_DK), 1)
    zero = jnp.zeros_like(q)

    lhs_parts, rhs_parts = [], []
    for blk in range(1, n_sub):
        b_ref_blk = _bcast_rows(bex, CHUNK, blk * SUB_BLOCK)
        k_blk = k * jnp.exp(jnp.where(rowc < blk * SUB_BLOCK, b_ref_blk - b, NEG_BIG))
        lhs_parts.append(jnp.where((rowc // SUB_BLOCK) == blk, q_blk, zero))
        rhs_parts.append(k_blk)
    lhs = jnp.concatenate(lhs_parts, axis=-1).astype(BF16)
    rhs = jnp.concatenate(rhs_parts, axis=-1).astype(BF16)

    pad = SUB_BLOCK
    kf_scr[0:pad, :] = jnp.zeros((pad, GLA_DK), F32)
    b_scr[0:pad, :] = jnp.zeros((pad, GLA_DK), F32)
    kf_scr[pad:pad + rows, :] = k
    b_scr[pad:pad + rows, :] = b
    ones = jnp.ones((GLA_DK, LANES), BF16)
    delta = rowc - lane
    row_sub = rowc % SUB_BLOCK
    a_diag = jnp.zeros((rows, LANES), F32)
    for d in range(SUB_BLOCK):
        kd = kf_scr[pad - d:pad - d + rows, :]
        bd = b_scr[pad - d:pad - d + rows, :]
        e = jnp.exp(jnp.where(row_sub >= d, b - bd, NEG_BIG))
        rd = _dot((q * kd * e).astype(BF16), ones)
        a_diag = a_diag + jnp.where(delta == d, rd, 0.0)

    outs = []
    for c in range(n_chunks):
        sl = slice(c * CHUNK, (c + 1) * CHUNK)
        att = _dot_nt(lhs[sl], rhs[sl]) + a_diag[sl, :CHUNK]
        st = st_scr[...]
        o = _dot(att.astype(BF16), v[sl]) + _dot_nt(q_chk[sl].astype(BF16), st.astype(BF16))
        d_st = _dot(v[sl].astype(F32).T.astype(BF16), k_end[sl].astype(BF16))
        dec = jnp.exp(b[(c + 1) * CHUNK - 1:(c + 1) * CHUNK, :])
        st_scr[...] = st * dec + d_st
        outs.append(o)
    o = jnp.concatenate(outs, axis=0) if n_chunks > 1 else outs[0]
    o_ref[...] = (_rms_rows(o, sub_ref[...]) * r_ref[...].astype(F32)).astype(o_ref.dtype)

    @pl.when(step == pl.num_programs(2) - 1)
    def _():
        sout_ref[...] = st_scr[...].T


def _gla(p3, gk3, sub, s0, rows):
    b, s, _ = p3.shape
    has_init = s0 is not None
    kq = D_MODEL // GLA_DK
    kv = D_MODEL // GLA_DV
    in_specs = [
        pl.BlockSpec((None, rows, GLA_DK), lambda bi, h, r: (bi, r, T_QKB * kq + h)),
        pl.BlockSpec((None, rows, GLA_DK), lambda bi, h, r: (bi, r, T_QKB * kq + GLA_HEADS + h)),
        pl.BlockSpec((None, rows, GLA_DV), lambda bi, h, r: (bi, r, T_VB * kv + h)),
        pl.BlockSpec((None, rows, GLA_DV), lambda bi, h, r: (bi, r, T_RB * kv + h)),
        pl.BlockSpec((None, rows, GLA_DK), lambda bi, h, r: (bi, r, h)),
        pl.BlockSpec(sub.shape, lambda bi, h, r: (0, 0)),
    ]
    args = [p3, p3, p3, p3, gk3, sub]
    if has_init:
        in_specs.append(pl.BlockSpec((None, None, GLA_DK, GLA_DV), lambda bi, h, r: (bi, h, 0, 0)))
        args.append(s0)
    kern = functools.partial(_gla_kernel, rows=rows, has_init=has_init)
    return pl.pallas_call(
        kern,
        grid=(b, GLA_HEADS, s // rows),
        in_specs=in_specs,
        out_specs=[
            pl.BlockSpec((None, rows, GLA_DV), lambda bi, h, r: (bi, r, h)),
            pl.BlockSpec((None, None, GLA_DK, GLA_DV), lambda bi, h, r: (bi, h, 0, 0)),
        ],
        out_shape=[
            jax.ShapeDtypeStruct((b, s, GLA_HEADS * GLA_DV), BF16),
            jax.ShapeDtypeStruct((b, GLA_HEADS, GLA_DK, GLA_DV), F32),
        ],
        scratch_shapes=[
            pltpu.VMEM((GLA_DV, GLA_DK), F32),
            pltpu.VMEM((SUB_BLOCK + rows, GLA_DK), F32),
            pltpu.VMEM((SUB_BLOCK + rows, GLA_DK), F32),
        ],
        compiler_params=pltpu.CompilerParams(
            dimension_semantics=("parallel", "parallel", "arbitrary"), vmem_limit_bytes=VMEM_LIMIT_BYTES),
        name="gla",
    )(*args)


def _mem_attn_kernel(q_ref, k_ref, v_ref, o_ref):
    outs = []
    for h in range(MEM_HEADS):
        sl = slice(h * MEM_HD, (h + 1) * MEM_HD)
        s = _dot_nt(q_ref[:, sl], k_ref[:, sl].astype(BF16))
        m = jnp.max(s, axis=-1, keepdims=True)
        p = jnp.exp(s - m)
        l = jnp.sum(p, axis=-1, keepdims=True)
        outs.append(_dot(p.astype(BF16), v_ref[:, sl].astype(BF16)) / l)
    o_ref[...] = jnp.concatenate(outs, axis=-1).astype(o_ref.dtype)


def _mem_attn(p3, mk, mv, tq):
    b, s, _ = p3.shape
    d = D_MODEL
    kv_block = (None,) + mk.shape[1:]
    return pl.pallas_call(
        _mem_attn_kernel,
        grid=(b, s // tq),
        in_specs=[
            pl.BlockSpec((None, tq, d), lambda bi, i: (bi, i, T_QM)),
            pl.BlockSpec(kv_block, lambda bi, i: (bi, 0, 0)),
            pl.BlockSpec(kv_block, lambda bi, i: (bi, 0, 0)),
        ],
        out_specs=pl.BlockSpec((None, tq, d), lambda bi, i: (bi, i, 0)),
        out_shape=jax.ShapeDtypeStruct((b, s, d), BF16),
        compiler_params=pltpu.CompilerParams(
            dimension_semantics=("parallel", "parallel"), vmem_limit_bytes=VMEM_LIMIT_BYTES),
        name="mem_attn",
    )(p3, mk, mv)


def _mix_kernel(x_ref, oa_ref, ob_ref, om_ref, ga_ref, gb_ref, gm_ref, wd_ref, wg_ref, wm_ref, wo_ref,
                y_ref):
    m = (ga_ref[...].astype(F32) * _dot(oa_ref[...], wd_ref[...])
         + gb_ref[...].astype(F32) * _dot(ob_ref[...], wg_ref[...])
         + gm_ref[...].astype(F32) * _dot(om_ref[...], wm_ref[...]))
    y_ref[...] = x_ref[...] + _dot(m.astype(BF16), wo_ref[...])


def _mix(x2, oa, ob, om, p2, wd, wg, wm, wo, tm):
    t = x2.shape[0]
    d = D_MODEL
    row = lambda i: (i, 0)
    full = lambda i: (0, 0)
    wspec = pl.BlockSpec((d, d), full, pipeline_mode=pl.Buffered(1))
    return pl.pallas_call(
        _mix_kernel,
        grid=(t // tm,),
        in_specs=[
            pl.BlockSpec((tm, d), row), pl.BlockSpec((tm, d), row), pl.BlockSpec((tm, d), row),
            pl.BlockSpec((tm, d), row),
            pl.BlockSpec((tm, d), lambda i: (i, T_GATE)),
            pl.BlockSpec((tm, d), lambda i: (i, T_GATE + 1)),
            pl.BlockSpec((tm, d), lambda i: (i, T_GATE + 2)),
            wspec, wspec, wspec, wspec,
        ],
        out_specs=pl.BlockSpec((tm, d), row),
        out_shape=jax.ShapeDtypeStruct((t, d), F32),
        compiler_params=pltpu.CompilerParams(
            dimension_semantics=("parallel",), vmem_limit_bytes=VMEM_LIMIT_BYTES),
        name="mix_out",
    )(x2, oa, ob, om, p2, p2, p2, wd, wg, wm, wo)


FFN_CHUNK = 256


def _ffn_kernel(x_ref, g_ref, wup_ref, cw_ref, cb_ref, wd_ref, cs_ref, y_ref, cso_ref,
                u_scr, carry_scr, gv_scr, *, n_seq, seq_rows, tiles_per_seq):
    i = pl.program_id(0)
    gap = SUBLANES
    stride = seq_rows + gap
    tail = CONV_W - 1
    n_chunks = D_FF // FFN_CHUNK
    x = x_ref[...]
    h = _rms_rows(x, g_ref[...]).astype(BF16)

    first = (i % tiles_per_seq) == 0
    for s in range(n_seq):
        base = s * stride

        @pl.when(first)
        def _():
            u_scr[base:base + gap, :] = jnp.zeros((gap, D_FF), F32)
            u_scr[base + gap - tail:base + gap, :] = cs_ref[s]

        @pl.when(jnp.logical_not(first))
        def _():
            u_scr[base:base + gap, :] = carry_scr[...]

    def up(c):
        cols = slice(c * FFN_CHUNK, (c + 1) * FFN_CHUNK)
        gate_cols = slice(D_FF + c * FFN_CHUNK, D_FF + (c + 1) * FFN_CHUNK)
        return _dot(h, wup_ref[:, cols]), _dot(h, wup_ref[:, gate_cols])

    def gated(c, u, vv):
        cols = slice(c * FFN_CHUNK, (c + 1) * FFN_CHUNK)
        cw = cw_ref[:, cols]
        outs = []
        for s in range(n_seq):
            base = s * stride + gap
            u_scr[base:base + seq_rows, cols] = u[s * seq_rows:(s + 1) * seq_rows]
            conv = cb_ref[:, cols]
            for j in range(CONV_W):
                off = base - tail + j
                conv = conv + cw[j:j + 1, :] * u_scr[off:off + seq_rows, cols]
            outs.append(conv)
        uc = jnp.concatenate(outs, axis=0) if n_seq > 1 else outs[0]
        gelu = 0.5 * uc * (1.0 + jnp.tanh(math.sqrt(2.0 / math.pi) * (uc + 0.044715 * (uc * uc * uc))))
        return (gelu * vv).astype(BF16)

    nxt = up(0)
    for c in range(n_chunks):
        cur = nxt
        if c + 1 < n_chunks:
            nxt = up(c + 1)
        gv_scr[:, c * FFN_CHUNK:(c + 1) * FFN_CHUNK] = gated(c, *cur)
    y_ref[...] = x + _dot(gv_scr[...], wd_ref[...])

    for s in range(n_seq):
        base = s * stride + gap
        cso_ref[s] = u_scr[base + seq_rows - tail:base + seq_rows, :]
    carry_scr[...] = u_scr[seq_rows:seq_rows + gap, :]


def _ffn(x2, g_ffn, w_up, conv_w, conv_b, w_down, conv_state, n_seq, seq_rows, tiles_per_seq):
    t = x2.shape[0]
    d = D_MODEL
    tm = n_seq * seq_rows
    nb = conv_state.shape[0]
    tail = CONV_W - 1
    kern = functools.partial(_ffn_kernel, n_seq=n_seq, seq_rows=seq_rows, tiles_per_seq=tiles_per_seq)
    y, tails = pl.pallas_call(
        kern,
        grid=(t // tm,),
        in_specs=[
            pl.BlockSpec((tm, d), lambda i: (i, 0)),
            _resident((1, d)),
            _resident(w_up.shape),
            _resident(conv_w.shape),
            _resident(conv_b.shape),
            _resident(w_down.shape),
            pl.BlockSpec((n_seq, tail, D_FF), lambda i: (i // tiles_per_seq, 0, 0)),
        ],
        out_specs=[
            pl.BlockSpec((tm, d), lambda i: (i, 0)),
            pl.BlockSpec((n_seq, tail, D_FF), lambda i: (i, 0, 0)),
        ],
        out_shape=[
            jax.ShapeDtypeStruct((t, d), F32),
            jax.ShapeDtypeStruct((nb * tiles_per_seq, tail, D_FF), F32),
        ],
        scratch_shapes=[
            pltpu.VMEM((n_seq * (seq_rows + SUBLANES), D_FF), F32),
            pltpu.VMEM((SUBLANES, D_FF), F32),
            pltpu.VMEM((tm, D_FF), BF16),
        ],
        compiler_params=pltpu.CompilerParams(
            dimension_semantics=("arbitrary",), vmem_limit_bytes=VMEM_LIMIT_BYTES),
        name="conv_ffn",
    )(x2, g_ffn, w_up, conv_w, conv_b, w_down, conv_state)
    return y, tails.reshape(nb, tiles_per_seq, tail, D_FF)[:, -1]


def _tile_gain(g, reps):
    return jnp.tile(g.astype(F32), reps).reshape(1, -1)


def _layer_weights(l, g_attn, w_in, w_gk2, b_gk, qn_diff, kn_diff, lam_q1, lam_k1, lam_q2, lam_k2,
                   subln_diff, subln_gla, g_mem, w_mem_kv, qn_mem, kn_mem, w_proj_diff, w_proj_gla,
                   w_proj_mem, w_out, g_ffn, w_up, conv_w, conv_b, w_down):
    d = D_MODEL
    w = w_in[l]
    lr0 = 6 * d
    wa = w[:, :lr0].astype(BF16)
    wb = w[:, lr0 + GK_RANK:].astype(BF16)
    wg1 = jnp.pad(w[:, lr0:lr0 + GK_RANK], ((0, 0), (0, LANES - GK_RANK))).astype(BF16)
    wg2 = jnp.pad(w_gk2[l], ((0, LANES - GK_RANK), (0, 0))).astype(BF16)
    return dict(
        g_attn=g_attn[l].reshape(1, d), wa=wa, wb=wb, wg1=wg1, wg2=wg2, bgk=b_gk[l].reshape(1, -1),
        qn_t=_tile_gain(qn_diff[l], d // DIFF_HD), kn_t=_tile_gain(kn_diff[l], d // DIFF_HD),
        qmn_t=_tile_gain(qn_mem[l], d // MEM_HD), knm_t=_tile_gain(kn_mem[l], d // MEM_HD),
        g64=_group_matrix(LANES, DIFF_HD), g256=_group_matrix(MEM_HD, MEM_HD),
        lamp=jnp.stack([lam_q1[l], lam_k1[l], lam_q2[l], lam_k2[l]]).astype(F32),
        sub_diff=subln_diff[l].reshape(1, -1), sub_gla=subln_gla[l].reshape(1, -1),
        g_mem=g_mem[l].reshape(1, d), w_mem_kv=w_mem_kv[l].astype(BF16),
        wd=w_proj_diff[l].astype(BF16), wg=w_proj_gla[l].astype(BF16), wm=w_proj_mem[l].astype(BF16),
        wo=w_out[l].astype(BF16), g_ffn=g_ffn[l].reshape(1, d), w_up=w_up[l].astype(BF16),
        conv_w=conv_w[l], conv_b=conv_b[l].reshape(1, -1), w_down=w_down[l].astype(BF16),
    )


def _group(x, wts, lam_init, mem_k, mem_v, past_k, past_v, gla_state, conv_state, prompt):
    b, s, d = x.shape
    t = b * s
    x2 = x.reshape(t, d)
    tm = _pick(t, 512)
    p2, ka, va, gk = _inproj(x2, wts["g_attn"], wts["wa"], wts["wb"], wts["wg1"], wts["wg2"], wts["bgk"],
                             wts["qn_t"], wts["kn_t"], wts["qmn_t"], wts["g64"], wts["g256"],
                             _pick(s, 256) if prompt else tm, s, prompt)
    if prompt:
        ka = ka.reshape(b, DIFF_HEADS, 2, DIFF_HD, s).transpose(0, 4, 1, 2, 3)
    else:
        ka = ka.reshape(b, s, DIFF_HEADS, 2, DIFF_HD)
    p3 = p2.reshape(b, s, N_PTILES * d)
    gk3 = gk.reshape(b, s, GLA_HEADS * GLA_DK)
    if prompt:
        oa = _diff_prompt(p3, wts["lamp"], wts["sub_diff"], lam_init, _pick(s, 256), 2)
        ob, gla_new = _gla(p3, gk3, wts["sub_gla"], None, _pick(s, 512))
    else:
        oa = _diff_sample(p3, past_k, past_v, wts["lamp"], wts["sub_diff"], lam_init)
        ob, gla_new = _gla(p3, gk3, wts["sub_gla"], gla_state, s)
    om = _mem_attn(p3, mem_k, mem_v, _pick(s, 512))
    x1 = _mix(x2, oa.reshape(t, d), ob.reshape(t, d), om.reshape(t, d), p2,
              wts["wd"], wts["wg"], wts["wm"], wts["wo"], tm)
    if prompt:
        rows = _pick(s, 512)
        y, cs = _ffn(x1, wts["g_ffn"], wts["w_up"], wts["conv_w"], wts["conv_b"], wts["w_down"],
                     conv_state, 1, rows, s // rows)
    else:
        y, cs = _ffn(x1, wts["g_ffn"], wts["w_up"], wts["conv_w"], wts["conv_b"], wts["w_down"],
                     conv_state, b, s, 1)
    return y.reshape(b, s, d), ka, va, gla_new, cs


def kernel(x_prompt, x_sample, mem_prompt, cache_diff_k, cache_diff_v, cache_mem_k, cache_mem_v, state_gla, state_conv, g_attn, w_in, w_gk2, b_gk, qn_diff, kn_diff, lam_q1, lam_k1, lam_q2, lam_k2, subln_diff, subln_gla, g_mem, w_mem_kv, qn_mem, kn_mem, w_proj_diff, w_proj_gla, w_proj_mem, w_out, g_ffn, w_up, conv_w, conv_b, w_down):
    depth = g_attn.shape[0]
    d = D_MODEL
    xp, xs = x_prompt, x_sample
    bp, sp, _ = xp.shape
    bs, ss, _ = xs.shape
    n_mem = mem_prompt.shape[1]
    outs = [[] for _ in range(10)]
    for l in range(depth):
        lam_init = 0.8 - 0.6 * math.exp(-0.3 * l)
        wts = _layer_weights(l, g_attn, w_in, w_gk2, b_gk, qn_diff, kn_diff, lam_q1, lam_k1, lam_q2,
                             lam_k2, subln_diff, subln_gla, g_mem, w_mem_kv, qn_mem, kn_mem,
                             w_proj_diff, w_proj_gla, w_proj_mem, w_out, g_ffn, w_up, conv_w, conv_b,
                             w_down)
        mk, mv = _memkv(mem_prompt.reshape(bp * n_mem, d), wts["g_mem"], wts["w_mem_kv"], wts["knm_t"],
                        wts["g256"])
        mk = mk.reshape(bp, n_mem, d)
        mv = mv.reshape(bp, n_mem, d)
        xp, kp, vp, gp, cp = _group(xp, wts, lam_init, mk, mv, None, None, None,
                                    jnp.zeros((bp, CONV_W - 1, D_FF), F32), True)
        xs, ks_, vs_, gs, cs = _group(
            xs, wts, lam_init, cache_mem_k[l].reshape(bs, n_mem, d), cache_mem_v[l].reshape(bs, n_mem, d),
            cache_diff_k[l].transpose(0, 2, 3, 4, 1).reshape(bs, d, -1), cache_diff_v[l].reshape(bs, -1, 2 * DIFF_HD),
            state_gla[l], state_conv[l], False)
        vals = (kp, vp.reshape(bp, sp, DIFF_HEADS, 2 * DIFF_HD),
                mk.reshape(bp, n_mem, MEM_HEADS, MEM_HD), mv.reshape(bp, n_mem, MEM_HEADS, MEM_HD), gp, cp,
                ks_, vs_.reshape(bs, ss, DIFF_HEADS, 2 * DIFF_HD), gs, cs)
        for o, v in zip(outs, vals):
            o.append(v)
    return (xp, xs) + tuple(jnp.stack(o) for o in outs)
```

```python
import functools
import math

import jax
import jax.numpy as jnp
from jax import lax
from jax.experimental import pallas as pl
from jax.experimental.pallas import tpu as pltpu

F32 = jnp.float32
BF16 = jnp.bfloat16

D_MODEL = 1024
CHUNK = 64
EPS = 1e-6
DIFF_HEADS = 8
DIFF_HD = 64
DIFF_SCALE = DIFF_HD ** -0.5
LOG2E = math.log2(math.e)
GLA_HEADS = 4
GLA_DK = 128
GLA_DV = 256
GLA_SCALE = GLA_DK ** -0.5
GK_RANK = 16
GK_NORM = 16.0
MEM_HEADS = 4
MEM_HD = 256
MEM_SCALE = MEM_HD ** -0.5
D_FF = 2816
CONV_W = 3

LANES = 128
SUBLANES = 8
VMEM_LIMIT_BYTES = 48 * 1024 * 1024

T_QA, T_KA, T_VA, T_QKB, T_VB, T_RB, T_QM, T_GATE = 0, 1, 2, 3, 4, 5, 6, 7
N_PTILES = 10
SUB_BLOCK = 8
NEG_BIG = -1e30


def _dot(a, b):
    return jnp.dot(a, b, preferred_element_type=F32)


def _dot_nt(a, b):
    return lax.dot_general(a, b, (((1,), (1,)), ((), ())), preferred_element_type=F32)


def _sigmoid(x):
    return 1.0 / (1.0 + jnp.exp(-x))


def _pick(n, pref):
    t = min(n, pref)
    while n % t:
        t -= 1
    return t


def _rms_rows(x, gain):
    ms = jnp.mean(x * x, axis=-1, keepdims=True)
    return x * lax.rsqrt(ms + EPS) * gain


def _group_rms(y, gmat, gain):
    slab = gmat.shape[0]
    outs = []
    for c in range(y.shape[-1] // slab):
        ys = y[:, c * slab:(c + 1) * slab]
        ms = _dot((ys * ys).astype(BF16), gmat)
        outs.append(ys * lax.rsqrt(ms + EPS))
    return jnp.concatenate(outs, axis=-1) * gain


def _group_matrix(slab, group):
    r = jnp.arange(slab) // group
    return jnp.where(r[:, None] == r[None, :], 1.0 / group, 0.0).astype(BF16)


def _inproj_kernel(x_ref, g_ref, wa_ref, wb_ref, wg1_ref, wg2_ref, bgk_ref, qn_ref, kn_ref, qmn_ref,
                   g64_ref, g256_ref, p_ref, ka_ref, vaf_ref, gk_ref, *, k_transposed):
    d = D_MODEL
    n_a = wa_ref.shape[1] // d
    h = _rms_rows(x_ref[...], g_ref[...]).astype(BF16)

    def project(j):
        if j < n_a:
            return _dot(h, wa_ref[:, j * d:(j + 1) * d])
        return _dot(h, wb_ref[:, (j - n_a) * d:(j - n_a + 1) * d])

    def finish(j, acc):
        if j == T_QA:
            out = _group_rms(acc, g64_ref[...], qn_ref[...]) * (DIFF_SCALE * LOG2E)
        elif j == T_KA:
            out = _group_rms(acc, g64_ref[...], kn_ref[...])
            ka_ref[...] = out.T if k_transposed else out
        elif j == T_VA:
            vaf_ref[...] = acc
            out = acc
        elif j == T_QKB:
            half = d // 2
            out = jnp.concatenate([acc[:, :half] * GLA_SCALE, acc[:, half:]], axis=-1)
            lr = _dot(h, wg1_ref[...])
            z = _dot(lr.astype(BF16), wg2_ref[...]) + bgk_ref[...]
            log_sig = jnp.minimum(z, 0.0) - jnp.log(1.0 + jnp.exp(-jnp.abs(z)))
            gk_ref[...] = log_sig * (1.0 / GK_NORM)
        elif j == T_VB:
            out = acc
        elif j == T_RB:
            out = acc * _sigmoid(acc)
        elif j == T_QM:
            out = _group_rms(acc, g256_ref[...], qmn_ref[...]) * MEM_SCALE
        else:
            out = _sigmoid(acc)
        p_ref[:, j * d:(j + 1) * d] = out.astype(BF16)

    acc_next = project(0)
    for j in range(N_PTILES):
        acc = acc_next
        if j + 1 < N_PTILES:
            acc_next = project(j + 1)
        finish(j, acc)


def _resident(shape):
    return pl.BlockSpec(shape, lambda *_: (0,) * len(shape), pipeline_mode=pl.Buffered(1))


def _inproj(x2, g_attn, wa, wb, wg1, wg2, bgk, qn_t, kn_t, qmn_t, g64, g256, tm, seq_len, k_transposed):
    t = x2.shape[0]
    d = D_MODEL
    row = lambda i: (i, 0)
    if k_transposed:
        per_seq = seq_len // tm
        ka_spec = pl.BlockSpec((None, d, tm), lambda i: (i // per_seq, 0, i % per_seq))
        ka_shape = jax.ShapeDtypeStruct((t // seq_len, d, seq_len), F32)
    else:
        ka_spec = pl.BlockSpec((tm, d), row)
        ka_shape = jax.ShapeDtypeStruct((t, d), F32)
    small = [wg1, wg2, bgk, qn_t, kn_t, qmn_t, g64, g256]
    return pl.pallas_call(
        functools.partial(_inproj_kernel, k_transposed=k_transposed),
        grid=(t // tm,),
        in_specs=[pl.BlockSpec((tm, d), row), _resident((1, d)), _resident(wa.shape), _resident(wb.shape)]
        + [_resident(a.shape) for a in small],
        out_specs=[
            pl.BlockSpec((tm, N_PTILES * d), row),
            ka_spec,
            pl.BlockSpec((tm, d), row),
            pl.BlockSpec((tm, GLA_HEADS * GLA_DK), row),
        ],
        out_shape=[
            jax.ShapeDtypeStruct((t, N_PTILES * d), BF16),
            ka_shape,
            jax.ShapeDtypeStruct((t, d), F32),
            jax.ShapeDtypeStruct((t, GLA_HEADS * GLA_DK), F32),
        ],
        compiler_params=pltpu.CompilerParams(
            dimension_semantics=("parallel",), vmem_limit_bytes=VMEM_LIMIT_BYTES),
        name="inproj",
    )(x2, g_attn, wa, wb, *small)


def _memkv_kernel(m_ref, g_ref, w_ref, kn_ref, g256_ref, k_ref, v_ref):
    j = pl.program_id(0)
    h = _rms_rows(m_ref[...], g_ref[...]).astype(BF16)
    acc = _dot(h, w_ref[...])

    @pl.when(j == 0)
    def _():
        k_ref[...] = _group_rms(acc, g256_ref[...], kn_ref[...])

    @pl.when(j == 1)
    def _():
        v_ref[...] = acc


def _memkv(mem2, g_mem, w_kv, knm_t, g256):
    t = mem2.shape[0]
    d = D_MODEL
    full = lambda j: (0, 0)
    return pl.pallas_call(
        _memkv_kernel,
        grid=(2,),
        in_specs=[
            pl.BlockSpec((t, d), full),
            pl.BlockSpec((1, d), full),
            pl.BlockSpec((d, d), lambda j: (0, j)),
            pl.BlockSpec((1, d), full),
            pl.BlockSpec(g256.shape, full),
        ],
        out_specs=[pl.BlockSpec((t, d), full), pl.BlockSpec((t, d), full)],
        out_shape=[jax.ShapeDtypeStruct((t, d), F32), jax.ShapeDtypeStruct((t, d), F32)],
        compiler_params=pltpu.CompilerParams(
            dimension_semantics=("arbitrary",), vmem_limit_bytes=VMEM_LIMIT_BYTES),
        name="memkv",
    )(mem2, g_mem, w_kv, knm_t, g256)


def _lambda_value(lamp, lam_init):
    a = jnp.sum(lamp[0:1, :] * lamp[1:2, :], axis=-1, keepdims=True)
    b = jnp.sum(lamp[2:3, :] * lamp[3:4, :], axis=-1, keepdims=True)
    return jnp.exp(a) - jnp.exp(b) + lam_init


def _stack_maps(q):
    lane = lax.broadcasted_iota(jnp.int32, q.shape, 1)
    zero = jnp.zeros_like(q)
    return jnp.concatenate([jnp.where(lane < DIFF_HD, q, zero), jnp.where(lane >= DIFF_HD, q, zero)], axis=0)


def _diff_finish(acc, l, lam, sub, lam_init, tq):
    o2 = acc / l
    o = o2[:tq] - lam * o2[tq:]
    return _rms_rows(o, sub) * (1.0 - lam_init)


ONES_ROWS = 16


def _diff_prompt_kernel(q_ref, k_ref, v_ref, lamp_ref, sub_ref, o_ref, qt_scr, vt_scr, m_scr, acc_scr,
                        st0_scr, st1_scr, p0_scr, p1_scr, alpha0_scr, alpha1_scr, mx0_scr, mx1_scr,
                        *, tq, blocks_per_trip, lam_init):
    qi = pl.program_id(2)
    n_kv = v_ref.shape[0] // tq
    hd2 = v_ref.shape[1]
    slots = list(range(blocks_per_trip))

    @pl.when(qi == 0)
    def _():
        for c in range(n_kv):
            vt_scr[c, 0:hd2, :] = v_ref[c * tq:(c + 1) * tq, :].astype(F32).T.astype(BF16)
            vt_scr[c, hd2:hd2 + ONES_ROWS, :] = jnp.ones((ONES_ROWS, tq), BF16)

    qt_scr[...] = _stack_maps(q_ref[...]).astype(F32).T.astype(BF16)
    m_scr[...] = jnp.full(m_scr.shape, -jnp.inf, F32)
    acc_scr[...] = jnp.zeros(acc_scr.shape, F32)

    def scores(kbs):
        sts = [_dot(k_ref[pl.ds(pl.multiple_of(kb * tq, tq), tq), :], qt_scr[:, g * tq:(g + 1) * tq])
               for kb in kbs for g in range(2)]
        return sts, [jnp.max(st, axis=0, keepdims=True) for st in sts]

    def softmax(sts, maxes, blk_slots, masked):
        pts, alphas = [], []
        for i, (st, mx) in enumerate(zip(sts, maxes)):
            slot, g = blk_slots[i // 2], i % 2
            cols = slice(g * tq, (g + 1) * tq)
            if masked[i // 2]:
                r = lax.broadcasted_iota(jnp.int32, st.shape, 0)
                c = lax.broadcasted_iota(jnp.int32, st.shape, 1)
                st = jnp.where((r // CHUNK) <= (c // CHUNK), st, -jnp.inf)
                mx = jnp.max(st, axis=0, keepdims=True)
            m_prev = m_scr[slot, :, cols]
            m_new = jnp.maximum(m_prev, mx)
            alphas.append(jnp.exp2(m_prev - m_new))
            pts.append(jnp.exp2(st - m_new).astype(BF16))
            m_scr[slot, :, cols] = m_new
        return pts, alphas

    def accumulate(kbs, blk_slots, pts, alphas):
        for i, (pt, alpha) in enumerate(zip(pts, alphas)):
            slot, g = blk_slots[i // 2], i % 2
            cols = slice(g * tq, (g + 1) * tq)
            acc_scr[slot, :, cols] = alpha * acc_scr[slot, :, cols] + _dot(vt_scr[kbs[i // 2]], pt)

    def group(t):
        return [jnp.minimum(t * blocks_per_trip + u, n_kv - 1) for u in slots]

    n_chains = 2 * blocks_per_trip
    chain_ids = list(range(n_chains))

    st_scr, p_scr, alpha_scr = (st0_scr, st1_scr), (p0_scr, p1_scr), (alpha0_scr, alpha1_scr)
    mx_scr = (mx0_scr, mx1_scr)

    def put(scr, buf, vals):
        for c, val in zip(chain_ids, vals):
            scr[buf][c] = val

    def get(scr, buf):
        return [scr[buf][c] for c in chain_ids]

    def put_scores(buf, sts_maxes):
        put(st_scr, buf, sts_maxes[0])
        put(mx_scr, buf, sts_maxes[1])

    def trip(t, par):
        pts, alphas = softmax(get(st_scr, par), get(mx_scr, par), slots, [False] * blocks_per_trip)
        put(p_scr, par, pts)
        put(alpha_scr, par, alphas)
        put_scores(1 - par, scores(group(t + 1)))
        accumulate(group(jnp.maximum(t - 1, 0)), slots, get(p_scr, 1 - par), get(alpha_scr, 1 - par))

    def four_trips(i, carry):
        for u in range(4):
            trip(4 * i + u, u % 2)
        return carry

    n_trips = qi // blocks_per_trip
    put_scores(0, scores(group(0)))
    put(p_scr, 1, [jnp.zeros((tq, tq), BF16)] * n_chains)
    put(alpha_scr, 1, [jnp.ones((1, tq), F32)] * n_chains)
    lax.fori_loop(0, n_trips // 4, four_trips, 0)
    done = (n_trips // 4) * 4

    @pl.when(n_trips - done >= 2)
    def _():
        trip(done, 0)
        trip(done + 1, 1)

    @pl.when(n_trips % 2 == 1)
    def _():
        trip(n_trips - 1, 0)

    first_left = n_trips * blocks_per_trip
    last_group = group(jnp.maximum(n_trips - 1, 0))
    for last_par in range(2):
        for n_left in range(blocks_per_trip):
            @pl.when(((n_trips + 1) % 2 == last_par) & (qi - first_left == n_left))
            def _():
                kbs = [first_left + u for u in range(n_left + 1)]
                sts = get(st_scr, 1 - last_par)[:2 * (n_left + 1)]
                maxes = get(mx_scr, 1 - last_par)[:2 * (n_left + 1)]
                accumulate(last_group, slots, get(p_scr, last_par), get(alpha_scr, last_par))
                pts, alphas = softmax(sts, maxes, slots[:n_left + 1], [False] * n_left + [True])
                accumulate(kbs, slots[:n_left + 1], pts, alphas)

    m_all = m_scr[0]
    for u in slots[1:]:
        m_all = jnp.maximum(m_all, m_scr[u])
    acc_all = jnp.zeros(acc_scr.shape[1:], F32)
    for u in slots:
        acc_all = acc_all + jnp.exp2(m_scr[u] - m_all) * acc_scr[u]

    lam = _lambda_value(lamp_ref[...], lam_init)
    o2t = acc_all[0:hd2, :] / acc_all[hd2:hd2 + 1, :]
    o = (o2t[:, :tq] - lam * o2t[:, tq:]).T
    o_ref[...] = (_rms_rows(o, sub_ref[...]) * (1.0 - lam_init)).astype(o_ref.dtype)


def _diff_prompt(p3, lamp, sub, lam_init, tq, blocks_per_trip):
    b, s, _ = p3.shape
    hd2 = 2 * DIFF_HD
    kern = functools.partial(_diff_prompt_kernel, tq=tq, blocks_per_trip=blocks_per_trip, lam_init=lam_init)
    per_d = D_MODEL // hd2
    return pl.pallas_call(
        kern,
        grid=(b, DIFF_HEADS, s // tq),
        in_specs=[
            pl.BlockSpec((None, tq, hd2), lambda bi, h, qi: (bi, qi, T_QA * per_d + h)),
            pl.BlockSpec((None, s, hd2), lambda bi, h, qi: (bi, 0, T_KA * per_d + h)),
            pl.BlockSpec((None, s, hd2), lambda bi, h, qi: (bi, 0, T_VA * per_d + h)),
            pl.BlockSpec(lamp.shape, lambda bi, h, qi: (0, 0)),
            pl.BlockSpec(sub.shape, lambda bi, h, qi: (0, 0)),
        ],
        out_specs=pl.BlockSpec((None, tq, hd2), lambda bi, h, qi: (bi, qi, h)),
        out_shape=jax.ShapeDtypeStruct((b, s, D_MODEL), BF16),
        scratch_shapes=[
            pltpu.VMEM((hd2, 2 * tq), BF16),
            pltpu.VMEM((s // tq, hd2 + ONES_ROWS, tq), BF16),
            pltpu.VMEM((blocks_per_trip, 1, 2 * tq), F32),
            pltpu.VMEM((blocks_per_trip, hd2 + ONES_ROWS, 2 * tq), F32),
            pltpu.VMEM((2 * blocks_per_trip, tq, tq), F32),
            pltpu.VMEM((2 * blocks_per_trip, tq, tq), F32),
            pltpu.VMEM((2 * blocks_per_trip, tq, tq), BF16),
            pltpu.VMEM((2 * blocks_per_trip, tq, tq), BF16),
            pltpu.VMEM((2 * blocks_per_trip, 1, tq), F32),
            pltpu.VMEM((2 * blocks_per_trip, 1, tq), F32),
            pltpu.VMEM((2 * blocks_per_trip, 1, tq), F32),
            pltpu.VMEM((2 * blocks_per_trip, 1, tq), F32),
        ],
        compiler_params=pltpu.CompilerParams(
            dimension_semantics=("parallel", "parallel", "arbitrary"), vmem_limit_bytes=VMEM_LIMIT_BYTES),
        name="diff_attn_prompt",
    )(p3, p3, p3, lamp, sub)


def _diff_sample_kernel(q_ref, kn_ref, vn_ref, kc_ref, vc_ref, lamp_ref, sub_ref, o_ref, *, lam_init):
    tq = q_ref.shape[0]
    q2 = _stack_maps(q_ref[...])
    s_c = _dot(q2, kc_ref[...].astype(BF16))
    s_n = _dot_nt(q2, kn_ref[...])
    m = jnp.maximum(jnp.max(s_c, axis=-1, keepdims=True), jnp.max(s_n, axis=-1, keepdims=True))
    p_c = jnp.exp2(s_c - m)
    p_n = jnp.exp2(s_n - m)
    l = jnp.sum(p_c, axis=-1, keepdims=True) + jnp.sum(p_n, axis=-1, keepdims=True)
    n_past = s_c.shape[1]
    vc = vc_ref[pl.ds(pl.program_id(1), n_past, stride=DIFF_HEADS), :]
    acc = _dot(p_c.astype(BF16), vc.astype(BF16)) + _dot(p_n.astype(BF16), vn_ref[...])
    lam = _lambda_value(lamp_ref[...], lam_init)
    o_ref[...] = _diff_finish(acc, l, lam, sub_ref[...], lam_init, tq).astype(o_ref.dtype)


def _diff_sample(p3, kc_t, vc, lamp, sub, lam_init):
    b, l, _ = p3.shape
    n_past = kc_t.shape[2]
    hd2 = 2 * DIFF_HD
    per_d = D_MODEL // hd2
    kern = functools.partial(_diff_sample_kernel, lam_init=lam_init)
    return pl.pallas_call(
        kern,
        grid=(b, DIFF_HEADS),
        in_specs=[
            pl.BlockSpec((None, l, hd2), lambda bi, h: (bi, 0, T_QA * per_d + h)),
            pl.BlockSpec((None, l, hd2), lambda bi, h: (bi, 0, T_KA * per_d + h)),
            pl.BlockSpec((None, l, hd2), lambda bi, h: (bi, 0, T_VA * per_d + h)),
            pl.BlockSpec((None, hd2, n_past), lambda bi, h: (bi, h, 0)),
            pl.BlockSpec((None, n_past * DIFF_HEADS, hd2), lambda bi, h: (bi, 0, 0)),
            pl.BlockSpec(lamp.shape, lambda bi, h: (0, 0)),
            pl.BlockSpec(sub.shape, lambda bi, h: (0, 0)),
        ],
        out_specs=pl.BlockSpec((None, l, hd2), lambda bi, h: (bi, 0, h)),
        out_shape=jax.ShapeDtypeStruct((b, l, D_MODEL), BF16),
        compiler_params=pltpu.CompilerParams(
            dimension_semantics=("parallel", "parallel"), vmem_limit_bytes=VMEM_LIMIT_BYTES),
        name="diff_attn_sample",
    )(p3, p3, p3, kc_t, vc, lamp, sub)


def _bcast_rows(x, period, row):
    r, c = x.shape
    x3 = x.reshape(r // period, period, c)
    return jnp.broadcast_to(x3[:, row:row + 1, :], x3.shape).reshape(r, c)


def _gla_kernel(*refs, rows, has_init):
    if has_init:
        q_ref, k_ref, v_ref, r_ref, g_ref, sub_ref, s0_ref, o_ref, sout_ref, st_scr, kf_scr, b_scr = refs
    else:
        q_ref, k_ref, v_ref, r_ref, g_ref, sub_ref, o_ref, sout_ref, st_scr, kf_scr, b_scr = refs
        s0_ref = None
    step = pl.program_id(2)
    n_chunks = rows // CHUNK
    n_sub = CHUNK // SUB_BLOCK

    @pl.when(step == 0)
    def _():
        if has_init:
            st_scr[...] = s0_ref[...]
        else:
            st_scr[...] = jnp.zeros(st_scr.shape, F32)

    q = q_ref[...].astype(F32)
    k = k_ref[...].astype(F32)
    g = g_ref[...]
    v = v_ref[...]
    chunk_rows = [slice(c * CHUNK, (c + 1) * CHUNK) for c in range(n_chunks)]

    ri = lax.broadcasted_iota(jnp.int32, (CHUNK, CHUNK), 0)
    ci = lax.broadcasted_iota(jnp.int32, (CHUNK, CHUNK), 1)
    tri = jnp.where(ci <= ri, 1.0, 0.0).astype(BF16)
    gw = jnp.concatenate([g[sl] for sl in chunk_rows], axis=1)
    g1 = gw.astype(BF16)
    rem = gw - g1.astype(F32)
    g2 = rem.astype(BF16)
    g3 = (rem - g2.astype(F32)).astype(BF16)
    bw = _dot(tri, g1) + _dot(tri, g2) + _dot(tri, g3)
    b = jnp.concatenate([bw[:, c * GLA_DK:(c + 1) * GLA_DK] for c in range(n_chunks)], axis=0)
    bex = b - g

    b_last = _bcast_rows(b, CHUNK, CHUNK - 1)
    b_blk = _bcast_rows(bex, SUB_BLOCK, 0)
    q_blk = q * jnp.exp(b - b_blk)
    q_chk = q * jnp.exp(b)
    k_end = k * jnp.exp(b_last - b)

    rowc = lax.broadcasted_iota(jnp.int32, (rows, GLA_DK), 0) % CHUNK
    lane = lax.broadcasted_iota(jnp.int32, (rows, GLA_DK), 1)
    zero = jnp.zeros_like(q)

    lhs_parts, rhs_parts = [], []
    for blk in range(1, n_sub):
        b_ref_blk = _bcast_rows(bex, CHUNK, blk * SUB_BLOCK)
        k_blk = k * jnp.exp(jnp.where(rowc < blk * SUB_BLOCK, b_ref_blk - b, NEG_BIG))
        lhs_parts.append(jnp.where((rowc // SUB_BLOCK) == blk, q_blk, zero))
        rhs_parts.append(k_blk)
    lhs = jnp.concatenate(lhs_parts, axis=-1).astype(BF16)
    rhs = jnp.concatenate(rhs_parts, axis=-1).astype(BF16)

    pad = SUB_BLOCK
    kf_scr[0:pad, :] = jnp.zeros((pad, GLA_DK), F32)
    b_scr[0:pad, :] = jnp.zeros((pad, GLA_DK), F32)
    kf_scr[pad:pad + rows, :] = k
    b_scr[pad:pad + rows, :] = b
    ones = jnp.ones((GLA_DK, LANES), BF16)
    delta = rowc - lane
    row_sub = rowc % SUB_BLOCK
    a_diag = jnp.zeros((rows, LANES), F32)
    for d in range(SUB_BLOCK):
        kd = kf_scr[pad - d:pad - d + rows, :]
        bd = b_scr[pad - d:pad - d + rows, :]
        e = jnp.exp(jnp.where(row_sub >= d, b - bd, NEG_BIG))
        rd = _dot((q * kd * e).astype(BF16), ones)
        a_diag = a_diag + jnp.where(delta == d, rd, 0.0)

    atts = [(_dot_nt(lhs[sl], rhs[sl]) + a_diag[sl, :CHUNK]).astype(BF16) for sl in chunk_rows]
    d_sts = [_dot(k_end[sl].T.astype(BF16), v[sl]) for sl in chunk_rows]
    o_intra = [_dot(att, v[sl]) for att, sl in zip(atts, chunk_rows)]
    decs = [jnp.exp(b[sl].T[:, CHUNK - 1:CHUNK]) for sl in chunk_rows]
    st = st_scr[...]
    outs = []
    for c, sl in enumerate(chunk_rows):
        outs.append(o_intra[c] + _dot(q_chk[sl].astype(BF16), st.astype(BF16)))
        st = st * decs[c] + d_sts[c]
    st_scr[...] = st
    o = jnp.concatenate(outs, axis=0) if n_chunks > 1 else outs[0]
    o_ref[...] = (_rms_rows(o, sub_ref[...]) * r_ref[...].astype(F32)).astype(o_ref.dtype)

    @pl.when(step == pl.num_programs(2) - 1)
    def _():
        sout_ref[...] = st_scr[...]


def _gla(p3, gk3, sub, s0, rows):
    b, s, _ = p3.shape
    has_init = s0 is not None
    kq = D_MODEL // GLA_DK
    kv = D_MODEL // GLA_DV
    in_specs = [
        pl.BlockSpec((None, rows, GLA_DK), lambda bi, h, r: (bi, r, T_QKB * kq + h)),
        pl.BlockSpec((None, rows, GLA_DK), lambda bi, h, r: (bi, r, T_QKB * kq + GLA_HEADS + h)),
        pl.BlockSpec((None, rows, GLA_DV), lambda bi, h, r: (bi, r, T_VB * kv + h)),
        pl.BlockSpec((None, rows, GLA_DV), lambda bi, h, r: (bi, r, T_RB * kv + h)),
        pl.BlockSpec((None, rows, GLA_DK), lambda bi, h, r: (bi, r, h)),
        pl.BlockSpec(sub.shape, lambda bi, h, r: (0, 0)),
    ]
    args = [p3, p3, p3, p3, gk3, sub]
    if has_init:
        in_specs.append(pl.BlockSpec((None, None, GLA_DK, GLA_DV), lambda bi, h, r: (bi, h, 0, 0)))
        args.append(s0)
    kern = functools.partial(_gla_kernel, rows=rows, has_init=has_init)
    return pl.pallas_call(
        kern,
        grid=(b, GLA_HEADS, s // rows),
        in_specs=in_specs,
        out_specs=[
            pl.BlockSpec((None, rows, GLA_DV), lambda bi, h, r: (bi, r, h)),
            pl.BlockSpec((None, None, GLA_DK, GLA_DV), lambda bi, h, r: (bi, h, 0, 0)),
        ],
        out_shape=[
            jax.ShapeDtypeStruct((b, s, GLA_HEADS * GLA_DV), BF16),
            jax.ShapeDtypeStruct((b, GLA_HEADS, GLA_DK, GLA_DV), F32),
        ],
        scratch_shapes=[
            pltpu.VMEM((GLA_DK, GLA_DV), F32),
            pltpu.VMEM((SUB_BLOCK + rows, GLA_DK), F32),
            pltpu.VMEM((SUB_BLOCK + rows, GLA_DK), F32),
        ],
        compiler_params=pltpu.CompilerParams(
            dimension_semantics=("parallel", "parallel", "arbitrary"), vmem_limit_bytes=VMEM_LIMIT_BYTES),
        name="gla",
    )(*args)


def _mem_attn_kernel(q_ref, k_ref, v_ref, o_ref):
    outs = []
    for h in range(MEM_HEADS):
        sl = slice(h * MEM_HD, (h + 1) * MEM_HD)
        s = _dot_nt(q_ref[:, sl], k_ref[:, sl].astype(BF16))
        m = jnp.max(s, axis=-1, keepdims=True)
        p = jnp.exp(s - m)
        l = jnp.sum(p, axis=-1, keepdims=True)
        outs.append(_dot(p.astype(BF16), v_ref[:, sl].astype(BF16)) / l)
    o_ref[...] = jnp.concatenate(outs, axis=-1).astype(o_ref.dtype)


def _mem_attn(p3, mk, mv, tq):
    b, s, _ = p3.shape
    d = D_MODEL
    kv_block = (None,) + mk.shape[1:]
    return pl.pallas_call(
        _mem_attn_kernel,
        grid=(b, s // tq),
        in_specs=[
            pl.BlockSpec((None, tq, d), lambda bi, i: (bi, i, T_QM)),
            pl.BlockSpec(kv_block, lambda bi, i: (bi, 0, 0)),
            pl.BlockSpec(kv_block, lambda bi, i: (bi, 0, 0)),
        ],
        out_specs=pl.BlockSpec((None, tq, d), lambda bi, i: (bi, i, 0)),
        out_shape=jax.ShapeDtypeStruct((b, s, d), BF16),
        compiler_params=pltpu.CompilerParams(
            dimension_semantics=("parallel", "parallel"), vmem_limit_bytes=VMEM_LIMIT_BYTES),
        name="mem_attn",
    )(p3, mk, mv)


def _mix_kernel(x_ref, oa_ref, ob_ref, om_ref, ga_ref, gb_ref, gm_ref, wd_ref, wg_ref, wm_ref, wo_ref,
                y_ref):
    m = (ga_ref[...].astype(F32) * _dot(oa_ref[...], wd_ref[...])
         + gb_ref[...].astype(F32) * _dot(ob_ref[...], wg_ref[...])
         + gm_ref[...].astype(F32) * _dot(om_ref[...], wm_ref[...]))
    y_ref[...] = x_ref[...] + _dot(m.astype(BF16), wo_ref[...])


def _mix(x2, oa, ob, om, p2, wd, wg, wm, wo, tm):
    t = x2.shape[0]
    d = D_MODEL
    row = lambda i: (i, 0)
    full = lambda i: (0, 0)
    wspec = pl.BlockSpec((d, d), full, pipeline_mode=pl.Buffered(1))
    return pl.pallas_call(
        _mix_kernel,
        grid=(t // tm,),
        in_specs=[
            pl.BlockSpec((tm, d), row), pl.BlockSpec((tm, d), row), pl.BlockSpec((tm, d), row),
            pl.BlockSpec((tm, d), row),
            pl.BlockSpec((tm, d), lambda i: (i, T_GATE)),
            pl.BlockSpec((tm, d), lambda i: (i, T_GATE + 1)),
            pl.BlockSpec((tm, d), lambda i: (i, T_GATE + 2)),
            wspec, wspec, wspec, wspec,
        ],
        out_specs=pl.BlockSpec((tm, d), row),
        out_shape=jax.ShapeDtypeStruct((t, d), F32),
        compiler_params=pltpu.CompilerParams(
            dimension_semantics=("parallel",), vmem_limit_bytes=VMEM_LIMIT_BYTES),
        name="mix_out",
    )(x2, oa, ob, om, p2, p2, p2, wd, wg, wm, wo)


FFN_CHUNK = 256


def _ffn_kernel(x_ref, g_ref, wup_ref, cw_ref, cb_ref, wd_ref, cs_ref, y_ref, cso_ref,
                u_scr, carry_scr, gv_scr, *, n_seq, seq_rows, tiles_per_seq):
    i = pl.program_id(0)
    gap = SUBLANES
    stride = seq_rows + gap
    tail = CONV_W - 1
    n_chunks = D_FF // FFN_CHUNK
    x = x_ref[...]
    h = _rms_rows(x, g_ref[...]).astype(BF16)

    first = (i % tiles_per_seq) == 0
    for s in range(n_seq):
        base = s * stride

        @pl.when(first)
        def _():
            u_scr[base:base + gap, :] = jnp.zeros((gap, D_FF), F32)
            u_scr[base + gap - tail:base + gap, :] = cs_ref[s]

        @pl.when(jnp.logical_not(first))
        def _():
            u_scr[base:base + gap, :] = carry_scr[...]

    def up(c):
        cols = slice(c * FFN_CHUNK, (c + 1) * FFN_CHUNK)
        gate_cols = slice(D_FF + c * FFN_CHUNK, D_FF + (c + 1) * FFN_CHUNK)
        return _dot(h, wup_ref[:, cols]), _dot(h, wup_ref[:, gate_cols])

    def gated(c, u, vv):
        cols = slice(c * FFN_CHUNK, (c + 1) * FFN_CHUNK)
        cw = cw_ref[:, cols]
        outs = []
        for s in range(n_seq):
            base = s * stride + gap
            u_scr[base:base + seq_rows, cols] = u[s * seq_rows:(s + 1) * seq_rows]
            conv = cb_ref[:, cols]
            for j in range(CONV_W):
                off = base - tail + j
                conv = conv + cw[j:j + 1, :] * u_scr[off:off + seq_rows, cols]
            outs.append(conv)
        uc = jnp.concatenate(outs, axis=0) if n_seq > 1 else outs[0]
        gelu = 0.5 * uc * (1.0 + jnp.tanh(math.sqrt(2.0 / math.pi) * (uc + 0.044715 * (uc * uc * uc))))
        return (gelu * vv).astype(BF16)

    nxt = up(0)
    for c in range(n_chunks):
        cur = nxt
        if c + 1 < n_chunks:
            nxt = up(c + 1)
        gv_scr[:, c * FFN_CHUNK:(c + 1) * FFN_CHUNK] = gated(c, *cur)
    y_ref[...] = x + _dot(gv_scr[...], wd_ref[...])

    for s in range(n_seq):
        base = s * stride + gap
        cso_ref[s] = u_scr[base + seq_rows - tail:base + seq_rows, :]
    carry_scr[...] = u_scr[seq_rows:seq_rows + gap, :]


def _ffn(x2, g_ffn, w_up, conv_w, conv_b, w_down, conv_state, n_seq, seq_rows, tiles_per_seq):
    t = x2.shape[0]
    d = D_MODEL
    tm = n_seq * seq_rows
    nb = conv_state.shape[0]
    tail = CONV_W - 1
    kern = functools.partial(_ffn_kernel, n_seq=n_seq, seq_rows=seq_rows, tiles_per_seq=tiles_per_seq)
    y, tails = pl.pallas_call(
        kern,
        grid=(t // tm,),
        in_specs=[
            pl.BlockSpec((tm, d), lambda i: (i, 0)),
            _resident((1, d)),
            _resident(w_up.shape),
            _resident(conv_w.shape),
            _resident(conv_b.shape),
            _resident(w_down.shape),
            pl.BlockSpec((n_seq, tail, D_FF), lambda i: (i // tiles_per_seq, 0, 0)),
        ],
        out_specs=[
            pl.BlockSpec((tm, d), lambda i: (i, 0)),
            pl.BlockSpec((n_seq, tail, D_FF), lambda i: (i, 0, 0)),
        ],
        out_shape=[
            jax.ShapeDtypeStruct((t, d), F32),
            jax.ShapeDtypeStruct((nb * tiles_per_seq, tail, D_FF), F32),
        ],
        scratch_shapes=[
            pltpu.VMEM((n_seq * (seq_rows + SUBLANES), D_FF), F32),
            pltpu.VMEM((SUBLANES, D_FF), F32),
            pltpu.VMEM((tm, D_FF), BF16),
        ],
        compiler_params=pltpu.CompilerParams(
            dimension_semantics=("arbitrary",), vmem_limit_bytes=VMEM_LIMIT_BYTES),
        name="conv_ffn",
    )(x2, g_ffn, w_up, conv_w, conv_b, w_down, conv_state)
    return y, tails.reshape(nb, tiles_per_seq, tail, D_FF)[:, -1]


def _tile_gain(g, reps):
    return jnp.tile(g.astype(F32), reps).reshape(1, -1)


def _layer_weights(l, g_attn, w_in, w_gk2, b_gk, qn_diff, kn_diff, lam_q1, lam_k1, lam_q2, lam_k2,
                   subln_diff, subln_gla, g_mem, w_mem_kv, qn_mem, kn_mem, w_proj_diff, w_proj_gla,
                   w_proj_mem, w_out, g_ffn, w_up, conv_w, conv_b, w_down):
    d = D_MODEL
    w = w_in[l]
    lr0 = 6 * d
    wa = w[:, :lr0].astype(BF16)
    wb = w[:, lr0 + GK_RANK:].astype(BF16)
    wg1 = jnp.pad(w[:, lr0:lr0 + GK_RANK], ((0, 0), (0, LANES - GK_RANK))).astype(BF16)
    wg2 = jnp.pad(w_gk2[l], ((0, LANES - GK_RANK), (0, 0))).astype(BF16)
    return dict(
        g_attn=g_attn[l].reshape(1, d), wa=wa, wb=wb, wg1=wg1, wg2=wg2, bgk=b_gk[l].reshape(1, -1),
        qn_t=_tile_gain(qn_diff[l], d // DIFF_HD), kn_t=_tile_gain(kn_diff[l], d // DIFF_HD),
        qmn_t=_tile_gain(qn_mem[l], d // MEM_HD), knm_t=_tile_gain(kn_mem[l], d // MEM_HD),
        g64=_group_matrix(LANES, DIFF_HD), g256=_group_matrix(MEM_HD, MEM_HD),
        lamp=jnp.stack([lam_q1[l], lam_k1[l], lam_q2[l], lam_k2[l]]).astype(F32),
        sub_diff=subln_diff[l].reshape(1, -1), sub_gla=subln_gla[l].reshape(1, -1),
        g_mem=g_mem[l].reshape(1, d), w_mem_kv=w_mem_kv[l].astype(BF16),
        wd=w_proj_diff[l].astype(BF16), wg=w_proj_gla[l].astype(BF16), wm=w_proj_mem[l].astype(BF16),
        wo=w_out[l].astype(BF16), g_ffn=g_ffn[l].reshape(1, d), w_up=w_up[l].astype(BF16),
        conv_w=conv_w[l], conv_b=conv_b[l].reshape(1, -1), w_down=w_down[l].astype(BF16),
    )


def _group(x, wts, lam_init, mem_k, mem_v, past_k, past_v, gla_state, conv_state, prompt):
    b, s, d = x.shape
    t = b * s
    x2 = x.reshape(t, d)
    tm = _pick(t, 512)
    p2, ka, va, gk = _inproj(x2, wts["g_attn"], wts["wa"], wts["wb"], wts["wg1"], wts["wg2"], wts["bgk"],
                             wts["qn_t"], wts["kn_t"], wts["qmn_t"], wts["g64"], wts["g256"],
                             _pick(s, 256) if prompt else tm, s, prompt)
    if prompt:
        ka = ka.reshape(b, DIFF_HEADS, 2, DIFF_HD, s).transpose(0, 4, 1, 2, 3)
    else:
        ka = ka.reshape(b, s, DIFF_HEADS, 2, DIFF_HD)
    p3 = p2.reshape(b, s, N_PTILES * d)
    gk3 = gk.reshape(b, s, GLA_HEADS * GLA_DK)
    if prompt:
        oa = _diff_prompt(p3, wts["lamp"], wts["sub_diff"], lam_init, _pick(s, 256), 2)
        ob, gla_new = _gla(p3, gk3, wts["sub_gla"], None, _pick(s, 512))
    else:
        oa = _diff_sample(p3, past_k, past_v, wts["lamp"], wts["sub_diff"], lam_init)
        ob, gla_new = _gla(p3, gk3, wts["sub_gla"], gla_state, s)
    om = _mem_attn(p3, mem_k, mem_v, _pick(s, 512))
    x1 = _mix(x2, oa.reshape(t, d), ob.reshape(t, d), om.reshape(t, d), p2,
              wts["wd"], wts["wg"], wts["wm"], wts["wo"], tm)
    if prompt:
        rows = _pick(s, 512)
        y, cs = _ffn(x1, wts["g_ffn"], wts["w_up"], wts["conv_w"], wts["conv_b"], wts["w_down"],
                     conv_state, 1, rows, s // rows)
    else:
        y, cs = _ffn(x1, wts["g_ffn"], wts["w_up"], wts["conv_w"], wts["conv_b"], wts["w_down"],
                     conv_state, b, s, 1)
    return y.reshape(b, s, d), ka, va, gla_new, cs


def kernel(x_prompt, x_sample, mem_prompt, cache_diff_k, cache_diff_v, cache_mem_k, cache_mem_v, state_gla, state_conv, g_attn, w_in, w_gk2, b_gk, qn_diff, kn_diff, lam_q1, lam_k1, lam_q2, lam_k2, subln_diff, subln_gla, g_mem, w_mem_kv, qn_mem, kn_mem, w_proj_diff, w_proj_gla, w_proj_mem, w_out, g_ffn, w_up, conv_w, conv_b, w_down):
    depth = g_attn.shape[0]
    d = D_MODEL
    xp, xs = x_prompt, x_sample
    bp, sp, _ = xp.shape
    bs, ss, _ = xs.shape
    n_mem = mem_prompt.shape[1]
    outs = [[] for _ in range(10)]
    for l in range(depth):
        lam_init = 0.8 - 0.6 * math.exp(-0.3 * l)
        wts = _layer_weights(l, g_attn, w_in, w_gk2, b_gk, qn_diff, kn_diff, lam_q1, lam_k1, lam_q2,
                             lam_k2, subln_diff, subln_gla, g_mem, w_mem_kv, qn_mem, kn_mem,
                             w_proj_diff, w_proj_gla, w_proj_mem, w_out, g_ffn, w_up, conv_w, conv_b,
                             w_down)
        mk, mv = _memkv(mem_prompt.reshape(bp * n_mem, d), wts["g_mem"], wts["w_mem_kv"], wts["knm_t"],
                        wts["g256"])
        mk = mk.reshape(bp, n_mem, d)
        mv = mv.reshape(bp, n_mem, d)
        xp, kp, vp, gp, cp = _group(xp, wts, lam_init, mk, mv, None, None, None,
                                    jnp.zeros((bp, CONV_W - 1, D_FF), F32), True)
        xs, ks_, vs_, gs, cs = _group(
            xs, wts, lam_init, cache_mem_k[l].reshape(bs, n_mem, d), cache_mem_v[l].reshape(bs, n_mem, d),
            cache_diff_k[l].transpose(0, 2, 3, 4, 1).reshape(bs, d, -1), cache_diff_v[l].reshape(bs, -1, 2 * DIFF_HD),
            state_gla[l], state_conv[l], False)
        vals = (kp, vp.reshape(bp, sp, DIFF_HEADS, 2 * DIFF_HD),
                mk.reshape(bp, n_mem, MEM_HEADS, MEM_HD), mv.reshape(bp, n_mem, MEM_HEADS, MEM_HD), gp, cp,
                ks_, vs_.reshape(bs, ss, DIFF_HEADS, 2 * DIFF_HD), gs, cs)
        for o, v in zip(outs, vals):
            o.append(v)
    return (xp, xs) + tuple(jnp.stack(o) for o in outs)
```

```python
import functools
import math

import jax
import jax.numpy as jnp
from jax import lax
from jax.experimental import pallas as pl
from jax.experimental.pallas import tpu as pltpu

F32 = jnp.float32
BF16 = jnp.bfloat16

D_MODEL = 1024
CHUNK = 64
EPS = 1e-6
DIFF_HEADS = 8
DIFF_HD = 64
DIFF_SCALE = DIFF_HD ** -0.5
LOG2E = math.log2(math.e)
GLA_HEADS = 4
GLA_DK = 128
GLA_DV = 256
GLA_SCALE = GLA_DK ** -0.5
GK_RANK = 16
GK_NORM = 16.0
MEM_HEADS = 4
MEM_HD = 256
MEM_SCALE = MEM_HD ** -0.5
D_FF = 2816
CONV_W = 3

LANES = 128
SUBLANES = 8
MXU_DIM = 256
VMEM_LIMIT_BYTES = 48 * 1024 * 1024

T_QA, T_KA, T_VA, T_QKB, T_VB, T_RB, T_QM, T_GATE = 0, 1, 2, 3, 4, 5, 6, 7
N_PTILES = 10
SUB_BLOCK = 8
NEG_BIG = -1e30


def _dot(a, b):
    return jnp.dot(a, b, preferred_element_type=F32)


def _dot_nt(a, b):
    return lax.dot_general(a, b, (((1,), (1,)), ((), ())), preferred_element_type=F32)


def _sigmoid(x):
    return 1.0 / (1.0 + jnp.exp(-x))


def _pick(n, pref):
    t = min(n, pref)
    while n % t:
        t -= 1
    return t


def _rms_rows(x, gain):
    ms = jnp.mean(x * x, axis=-1, keepdims=True)
    return x * lax.rsqrt(ms + EPS) * gain


def _group_rms(y, gmat, gain):
    slab = gmat.shape[0]
    outs = []
    for c in range(y.shape[-1] // slab):
        ys = y[:, c * slab:(c + 1) * slab]
        ms = _dot((ys * ys).astype(BF16), gmat)
        outs.append(ys * lax.rsqrt(ms + EPS))
    return jnp.concatenate(outs, axis=-1) * gain


def _group_matrix(slab, group):
    r = jnp.arange(slab) // group
    return jnp.where(r[:, None] == r[None, :], 1.0 / group, 0.0).astype(BF16)


def _inproj_kernel(x_ref, g_ref, wa_ref, wb_ref, wg1_ref, wg2_ref, bgk_ref, qn_ref, kn_ref, qmn_ref,
                   g64_ref, g256_ref, p_ref, ka_ref, vaf_ref, gk_ref, *, k_transposed):
    d = D_MODEL
    n_a = wa_ref.shape[1] // d
    h = _rms_rows(x_ref[...], g_ref[...]).astype(BF16)

    def project(j):
        if j < n_a:
            return _dot(h, wa_ref[:, j * d:(j + 1) * d])
        return _dot(h, wb_ref[:, (j - n_a) * d:(j - n_a + 1) * d])

    def finish(j, acc):
        if j == T_QA:
            out = _group_rms(acc, g64_ref[...], qn_ref[...]) * (DIFF_SCALE * LOG2E)
        elif j == T_KA:
            out = _group_rms(acc, g64_ref[...], kn_ref[...])
            ka_ref[...] = out.T if k_transposed else out
        elif j == T_VA:
            vaf_ref[...] = acc
            out = acc
        elif j == T_QKB:
            half = d // 2
            out = jnp.concatenate([acc[:, :half] * GLA_SCALE, acc[:, half:]], axis=-1)
            lr = _dot(h, wg1_ref[...])
            z = _dot(lr.astype(BF16), wg2_ref[...]) + bgk_ref[...]
            log_sig = jnp.minimum(z, 0.0) - jnp.log(1.0 + jnp.exp(-jnp.abs(z)))
            gk_ref[...] = log_sig * (1.0 / GK_NORM)
        elif j == T_VB:
            out = acc
        elif j == T_RB:
            out = acc * _sigmoid(acc)
        elif j == T_QM:
            out = _group_rms(acc, g256_ref[...], qmn_ref[...]) * MEM_SCALE
        else:
            out = _sigmoid(acc)
        p_ref[:, j * d:(j + 1) * d] = out.astype(BF16)

    acc_next = project(0)
    for j in range(N_PTILES):
        acc = acc_next
        if j + 1 < N_PTILES:
            acc_next = project(j + 1)
        finish(j, acc)


def _resident(shape):
    return pl.BlockSpec(shape, lambda *_: (0,) * len(shape), pipeline_mode=pl.Buffered(1))


def _inproj(x2, g_attn, wa, wb, wg1, wg2, bgk, qn_t, kn_t, qmn_t, g64, g256, tm, seq_len, k_transposed):
    t = x2.shape[0]
    d = D_MODEL
    row = lambda i: (i, 0)
    if k_transposed:
        per_seq = seq_len // tm
        ka_spec = pl.BlockSpec((None, d, tm), lambda i: (i // per_seq, 0, i % per_seq))
        ka_shape = jax.ShapeDtypeStruct((t // seq_len, d, seq_len), F32)
    else:
        ka_spec = pl.BlockSpec((tm, d), row)
        ka_shape = jax.ShapeDtypeStruct((t, d), F32)
    small = [wg1, wg2, bgk, qn_t, kn_t, qmn_t, g64, g256]
    return pl.pallas_call(
        functools.partial(_inproj_kernel, k_transposed=k_transposed),
        grid=(t // tm,),
        in_specs=[pl.BlockSpec((tm, d), row), _resident((1, d)), _resident(wa.shape), _resident(wb.shape)]
        + [_resident(a.shape) for a in small],
        out_specs=[
            pl.BlockSpec((tm, N_PTILES * d), row),
            ka_spec,
            pl.BlockSpec((tm, d), row),
            pl.BlockSpec((tm, GLA_HEADS * GLA_DK), row),
        ],
        out_shape=[
            jax.ShapeDtypeStruct((t, N_PTILES * d), BF16),
            ka_shape,
            jax.ShapeDtypeStruct((t, d), F32),
            jax.ShapeDtypeStruct((t, GLA_HEADS * GLA_DK), F32),
        ],
        compiler_params=pltpu.CompilerParams(
            dimension_semantics=("parallel",), vmem_limit_bytes=VMEM_LIMIT_BYTES),
        name="inproj",
    )(x2, g_attn, wa, wb, *small)


def _memkv_kernel(m_ref, g_ref, w_ref, kn_ref, g256_ref, k_ref, v_ref):
    j = pl.program_id(0)
    h = _rms_rows(m_ref[...], g_ref[...]).astype(BF16)
    acc = _dot(h, w_ref[...])

    @pl.when(j == 0)
    def _():
        k_ref[...] = _group_rms(acc, g256_ref[...], kn_ref[...])

    @pl.when(j == 1)
    def _():
        v_ref[...] = acc


def _memkv(mem2, g_mem, w_kv, knm_t, g256):
    t = mem2.shape[0]
    d = D_MODEL
    full = lambda j: (0, 0)
    return pl.pallas_call(
        _memkv_kernel,
        grid=(2,),
        in_specs=[
            pl.BlockSpec((t, d), full),
            pl.BlockSpec((1, d), full),
            pl.BlockSpec((d, d), lambda j: (0, j)),
            pl.BlockSpec((1, d), full),
            pl.BlockSpec(g256.shape, full),
        ],
        out_specs=[pl.BlockSpec((t, d), full), pl.BlockSpec((t, d), full)],
        out_shape=[jax.ShapeDtypeStruct((t, d), F32), jax.ShapeDtypeStruct((t, d), F32)],
        compiler_params=pltpu.CompilerParams(
            dimension_semantics=("arbitrary",), vmem_limit_bytes=VMEM_LIMIT_BYTES),
        name="memkv",
    )(mem2, g_mem, w_kv, knm_t, g256)


def _lambda_value(lamp, lam_init):
    a = jnp.sum(lamp[0:1, :] * lamp[1:2, :], axis=-1, keepdims=True)
    b = jnp.sum(lamp[2:3, :] * lamp[3:4, :], axis=-1, keepdims=True)
    return jnp.exp(a) - jnp.exp(b) + lam_init


def _stack_maps(q):
    lane = lax.broadcasted_iota(jnp.int32, q.shape, 1)
    zero = jnp.zeros_like(q)
    return jnp.concatenate([jnp.where(lane < DIFF_HD, q, zero), jnp.where(lane >= DIFF_HD, q, zero)], axis=0)


def _diff_finish(acc, l, lam, sub, lam_init, tq):
    o2 = acc / l
    o = o2[:tq] - lam * o2[tq:]
    return _rms_rows(o, sub) * (1.0 - lam_init)


ONES_ROWS = 16


def _diff_prompt_kernel(q_ref, k_ref, v_ref, lamp_ref, sub_ref, o_ref, qt_scr, vt_scr, m_scr, acc_scr,
                        st0_scr, st1_scr, p0_scr, p1_scr, alpha0_scr, alpha1_scr, mx0_scr, mx1_scr,
                        *, tq, tk, lam_init):
    qi = pl.program_id(2)
    n_kv = v_ref.shape[0] // tk
    hd2 = v_ref.shape[1]
    halves = tq // tk
    assert halves == 2
    all_cgs = list(range(2 * halves))
    upper_cgs = [g * halves + 1 for g in range(2)]

    @pl.when(qi == 0)
    def _():
        for c in range(n_kv):
            vt_scr[c, 0:hd2, :] = v_ref[c * tk:(c + 1) * tk, :].astype(F32).T.astype(BF16)
            vt_scr[c, hd2:hd2 + ONES_ROWS, :] = jnp.ones((ONES_ROWS, tk), BF16)

    qt_scr[...] = _stack_maps(q_ref[...]).astype(F32).T.astype(BF16)
    m_scr[...] = jnp.full(m_scr.shape, -jnp.inf, F32)
    acc_scr[...] = jnp.zeros(acc_scr.shape, F32)

    def scores(kb, cgs):
        k = k_ref[pl.ds(pl.multiple_of(kb * tk, tk), tk), :]
        sts = [_dot(k, qt_scr[:, cg * tk:(cg + 1) * tk]) for cg in cgs]
        return sts, [jnp.max(st, axis=0, keepdims=True) for st in sts]

    def softmax(sts, maxes, slot, cgs, masked):
        pts, alphas = [], []
        for st, mx, cg, msk in zip(sts, maxes, cgs, masked):
            cols = slice(cg * tk, (cg + 1) * tk)
            if msk:
                r = lax.broadcasted_iota(jnp.int32, st.shape, 0)
                c = lax.broadcasted_iota(jnp.int32, st.shape, 1)
                st = jnp.where((r // CHUNK) <= (c // CHUNK), st, -jnp.inf)
                mx = jnp.max(st, axis=0, keepdims=True)
            m_prev = m_scr[slot, :, cols]
            m_new = jnp.maximum(m_prev, mx)
            alphas.append(jnp.exp2(m_prev - m_new))
            pts.append(jnp.exp2(st - m_new).astype(BF16))
            m_scr[slot, :, cols] = m_new
        return pts, alphas

    def accumulate(kb, slot, cgs, pts, alphas):
        vt = vt_scr[kb]
        for pt, alpha, cg in zip(pts, alphas, cgs):
            cols = slice(cg * tk, (cg + 1) * tk)
            acc_scr[slot, :, cols] = alpha * acc_scr[slot, :, cols] + _dot(vt, pt)

    st_scr, p_scr, alpha_scr = (st0_scr, st1_scr), (p0_scr, p1_scr), (alpha0_scr, alpha1_scr)
    mx_scr = (mx0_scr, mx1_scr)

    def put(scr, buf, cgs, vals):
        for cg, val in zip(cgs, vals):
            scr[buf][cg] = val

    def get(scr, buf, cgs):
        return [scr[buf][cg] for cg in cgs]

    def put_scores(buf, cgs, sts_maxes):
        put(st_scr, buf, cgs, sts_maxes[0])
        put(mx_scr, buf, cgs, sts_maxes[1])

    def trip(t, par):
        pts, alphas = softmax(get(st_scr, par, all_cgs), get(mx_scr, par, all_cgs), par, all_cgs,
                              [False] * len(all_cgs))
        put(p_scr, par, all_cgs, pts)
        put(alpha_scr, par, all_cgs, alphas)
        put_scores(1 - par, all_cgs, scores(jnp.minimum(t + 1, n_kv - 1), all_cgs))
        accumulate(jnp.maximum(t - 1, 0), 1 - par, all_cgs,
                   get(p_scr, 1 - par, all_cgs), get(alpha_scr, 1 - par, all_cgs))

    def four_trips(i, carry):
        for u in range(4):
            trip(4 * i + u, u % 2)
        return carry

    n_trips = qi * halves
    put_scores(0, all_cgs, scores(0, all_cgs))
    put(p_scr, 1, all_cgs, [jnp.zeros((tk, tk), BF16)] * len(all_cgs))
    put(alpha_scr, 1, all_cgs, [jnp.ones((1, tk), F32)] * len(all_cgs))
    lax.fori_loop(0, n_trips // 4, four_trips, 0)

    @pl.when(n_trips % 4 == 2)
    def _():
        trip(n_trips - 2, 0)
        trip(n_trips - 1, 1)

    kb0 = n_trips
    upper = scores(kb0 + 1, upper_cgs)
    accumulate(jnp.maximum(kb0 - 1, 0), 1, all_cgs, get(p_scr, 1, all_cgs), get(alpha_scr, 1, all_cgs))
    pts, alphas = softmax(get(st_scr, 0, all_cgs), get(mx_scr, 0, all_cgs), 0, all_cgs,
                          [cg % halves == 0 for cg in all_cgs])
    accumulate(kb0, 0, all_cgs, pts, alphas)
    pts, alphas = softmax(upper[0], upper[1], 1, upper_cgs, [True] * len(upper_cgs))
    accumulate(kb0 + 1, 1, upper_cgs, pts, alphas)

    m_all = jnp.maximum(m_scr[0], m_scr[1])
    acc_all = jnp.exp2(m_scr[0] - m_all) * acc_scr[0] + jnp.exp2(m_scr[1] - m_all) * acc_scr[1]

    lam = _lambda_value(lamp_ref[...], lam_init)
    o2t = acc_all[0:hd2, :] / acc_all[hd2:hd2 + 1, :]
    o = (o2t[:, :tq] - lam * o2t[:, tq:]).T
    o_ref[...] = (_rms_rows(o, sub_ref[...]) * (1.0 - lam_init)).astype(o_ref.dtype)


def _diff_prompt(p3, lamp, sub, lam_init, tq, tk):
    b, s, _ = p3.shape
    hd2 = 2 * DIFF_HD
    kern = functools.partial(_diff_prompt_kernel, tq=tq, tk=tk, lam_init=lam_init)
    per_d = D_MODEL // hd2
    n_cg = 2 * tq // tk
    return pl.pallas_call(
        kern,
        grid=(b, DIFF_HEADS, s // tq),
        in_specs=[
            pl.BlockSpec((None, tq, hd2), lambda bi, h, qi: (bi, qi, T_QA * per_d + h)),
            pl.BlockSpec((None, s, hd2), lambda bi, h, qi: (bi, 0, T_KA * per_d + h)),
            pl.BlockSpec((None, s, hd2), lambda bi, h, qi: (bi, 0, T_VA * per_d + h)),
            pl.BlockSpec(lamp.shape, lambda bi, h, qi: (0, 0)),
            pl.BlockSpec(sub.shape, lambda bi, h, qi: (0, 0)),
        ],
        out_specs=pl.BlockSpec((None, tq, hd2), lambda bi, h, qi: (bi, qi, h)),
        out_shape=jax.ShapeDtypeStruct((b, s, D_MODEL), BF16),
        scratch_shapes=[
            pltpu.VMEM((hd2, 2 * tq), BF16),
            pltpu.VMEM((s // tk, hd2 + ONES_ROWS, tk), BF16),
            pltpu.VMEM((2, 1, 2 * tq), F32),
            pltpu.VMEM((2, hd2 + ONES_ROWS, 2 * tq), F32),
            pltpu.VMEM((n_cg, tk, tk), F32),
            pltpu.VMEM((n_cg, tk, tk), F32),
            pltpu.VMEM((n_cg, tk, tk), BF16),
            pltpu.VMEM((n_cg, tk, tk), BF16),
            pltpu.VMEM((n_cg, 1, tk), F32),
            pltpu.VMEM((n_cg, 1, tk), F32),
            pltpu.VMEM((n_cg, 1, tk), F32),
            pltpu.VMEM((n_cg, 1, tk), F32),
        ],
        compiler_params=pltpu.CompilerParams(
            dimension_semantics=("parallel", "parallel", "arbitrary"), vmem_limit_bytes=VMEM_LIMIT_BYTES),
        name="diff_attn_prompt",
    )(p3, p3, p3, lamp, sub)


def _diff_sample_kernel(q_ref, kn_ref, vn_ref, kc_ref, vc_ref, lamp_ref, sub_ref, o_ref, *, lam_init):
    tq = q_ref.shape[0]
    q2 = _stack_maps(q_ref[...])
    s_c = _dot(q2, kc_ref[...].astype(BF16))
    s_n = _dot_nt(q2, kn_ref[...])
    m = jnp.maximum(jnp.max(s_c, axis=-1, keepdims=True), jnp.max(s_n, axis=-1, keepdims=True))
    p_c = jnp.exp2(s_c - m)
    p_n = jnp.exp2(s_n - m)
    l = jnp.sum(p_c, axis=-1, keepdims=True) + jnp.sum(p_n, axis=-1, keepdims=True)
    n_past = s_c.shape[1]
    vc = vc_ref[pl.ds(pl.program_id(1), n_past, stride=DIFF_HEADS), :]
    acc = _dot(p_c.astype(BF16), vc.astype(BF16)) + _dot(p_n.astype(BF16), vn_ref[...])
    lam = _lambda_value(lamp_ref[...], lam_init)
    o_ref[...] = _diff_finish(acc, l, lam, sub_ref[...], lam_init, tq).astype(o_ref.dtype)


def _diff_sample(p3, kc_t, vc, lamp, sub, lam_init):
    b, l, _ = p3.shape
    n_past = kc_t.shape[2]
    hd2 = 2 * DIFF_HD
    per_d = D_MODEL // hd2
    kern = functools.partial(_diff_sample_kernel, lam_init=lam_init)
    return pl.pallas_call(
        kern,
        grid=(b, DIFF_HEADS),
        in_specs=[
            pl.BlockSpec((None, l, hd2), lambda bi, h: (bi, 0, T_QA * per_d + h)),
            pl.BlockSpec((None, l, hd2), lambda bi, h: (bi, 0, T_KA * per_d + h)),
            pl.BlockSpec((None, l, hd2), lambda bi, h: (bi, 0, T_VA * per_d + h)),
            pl.BlockSpec((None, hd2, n_past), lambda bi, h: (bi, h, 0)),
            pl.BlockSpec((None, n_past * DIFF_HEADS, hd2), lambda bi, h: (bi, 0, 0)),
            pl.BlockSpec(lamp.shape, lambda bi, h: (0, 0)),
            pl.BlockSpec(sub.shape, lambda bi, h: (0, 0)),
        ],
        out_specs=pl.BlockSpec((None, l, hd2), lambda bi, h: (bi, 0, h)),
        out_shape=jax.ShapeDtypeStruct((b, l, D_MODEL), BF16),
        compiler_params=pltpu.CompilerParams(
            dimension_semantics=("parallel", "parallel"), vmem_limit_bytes=VMEM_LIMIT_BYTES),
        name="diff_attn_sample",
    )(p3, p3, p3, kc_t, vc, lamp, sub)


def _bcast_rows(x, period, row):
    r, c = x.shape
    x3 = x.reshape(r // period, period, c)
    return jnp.broadcast_to(x3[:, row:row + 1, :], x3.shape).reshape(r, c)


def _gla_kernel(*refs, rows, has_init):
    if has_init:
        q_ref, k_ref, v_ref, r_ref, g_ref, sub_ref, s0_ref, o_ref, sout_ref, st_scr, kf_scr, b_scr = refs
    else:
        q_ref, k_ref, v_ref, r_ref, g_ref, sub_ref, o_ref, sout_ref, st_scr, kf_scr, b_scr = refs
        s0_ref = None
    step = pl.program_id(2)
    n_chunks = rows // CHUNK
    n_sub = CHUNK // SUB_BLOCK

    @pl.when(step == 0)
    def _():
        if has_init:
            st_scr[...] = s0_ref[...]
        else:
            st_scr[...] = jnp.zeros(st_scr.shape, F32)

    q = q_ref[...].astype(F32)
    k = k_ref[...].astype(F32)
    g = g_ref[...]
    v = v_ref[...]
    chunk_rows = [slice(c * CHUNK, (c + 1) * CHUNK) for c in range(n_chunks)]

    ri = lax.broadcasted_iota(jnp.int32, (CHUNK, CHUNK), 0)
    ci = lax.broadcasted_iota(jnp.int32, (CHUNK, CHUNK), 1)
    tri = jnp.where(ci <= ri, 1.0, 0.0).astype(BF16)
    gw = jnp.concatenate([g[sl] for sl in chunk_rows], axis=1)
    g1 = gw.astype(BF16)
    rem = gw - g1.astype(F32)
    g2 = rem.astype(BF16)
    g3 = (rem - g2.astype(F32)).astype(BF16)
    bw = _dot(tri, g1) + _dot(tri, g2) + _dot(tri, g3)
    b = jnp.concatenate([bw[:, c * GLA_DK:(c + 1) * GLA_DK] for c in range(n_chunks)], axis=0)
    bex = b - g

    b_last = _bcast_rows(b, CHUNK, CHUNK - 1)
    b_blk = _bcast_rows(bex, SUB_BLOCK, 0)
    q_blk = q * jnp.exp(b - b_blk)
    q_chk = q * jnp.exp(b)
    k_end = k * jnp.exp(b_last - b)

    rowc = lax.broadcasted_iota(jnp.int32, (rows, GLA_DK), 0) % CHUNK
    lane = lax.broadcasted_iota(jnp.int32, (rows, GLA_DK), 1)
    zero = jnp.zeros_like(q)

    lhs_parts, rhs_parts = [], []
    for blk in range(1, n_sub):
        b_ref_blk = _bcast_rows(bex, CHUNK, blk * SUB_BLOCK)
        k_blk = k * jnp.exp(jnp.where(rowc < blk * SUB_BLOCK, b_ref_blk - b, NEG_BIG))
        lhs_parts.append(jnp.where((rowc // SUB_BLOCK) == blk, q_blk, zero))
        rhs_parts.append(k_blk)
    lhs = jnp.concatenate(lhs_parts, axis=-1).astype(BF16)
    rhs = jnp.concatenate(rhs_parts, axis=-1).astype(BF16)

    pad = SUB_BLOCK
    kf_scr[0:pad, :] = jnp.zeros((pad, GLA_DK), F32)
    b_scr[0:pad, :] = jnp.zeros((pad, GLA_DK), F32)
    kf_scr[pad:pad + rows, :] = k
    b_scr[pad:pad + rows, :] = b
    ones = jnp.ones((GLA_DK, LANES), BF16)
    delta = rowc - lane
    row_sub = rowc % SUB_BLOCK
    a_diag = jnp.zeros((rows, LANES), F32)
    for d in range(SUB_BLOCK):
        kd = kf_scr[pad - d:pad - d + rows, :]
        bd = b_scr[pad - d:pad - d + rows, :]
        e = jnp.exp(jnp.where(row_sub >= d, b - bd, NEG_BIG))
        rd = _dot((q * kd * e).astype(BF16), ones)
        a_diag = a_diag + jnp.where(delta == d, rd, 0.0)

    atts = [(_dot_nt(lhs[sl], rhs[sl]) + a_diag[sl, :CHUNK]).astype(BF16) for sl in chunk_rows]
    d_sts = [_dot(k_end[sl].T.astype(BF16), v[sl]) for sl in chunk_rows]
    o_intra = [_dot(att, v[sl]) for att, sl in zip(atts, chunk_rows)]
    decs = [jnp.exp(b[sl].T[:, CHUNK - 1:CHUNK]) for sl in chunk_rows]
    st = st_scr[...]
    outs = []
    for c, sl in enumerate(chunk_rows):
        outs.append(o_intra[c] + _dot(q_chk[sl].astype(BF16), st.astype(BF16)))
        st = st * decs[c] + d_sts[c]
    st_scr[...] = st
    o = jnp.concatenate(outs, axis=0) if n_chunks > 1 else outs[0]
    o_ref[...] = (_rms_rows(o, sub_ref[...]) * r_ref[...].astype(F32)).astype(o_ref.dtype)

    @pl.when(step == pl.num_programs(2) - 1)
    def _():
        sout_ref[...] = st_scr[...]


def _gla(p3, gk3, sub, s0, rows):
    b, s, _ = p3.shape
    has_init = s0 is not None
    kq = D_MODEL // GLA_DK
    kv = D_MODEL // GLA_DV
    in_specs = [
        pl.BlockSpec((None, rows, GLA_DK), lambda bi, h, r: (bi, r, T_QKB * kq + h)),
        pl.BlockSpec((None, rows, GLA_DK), lambda bi, h, r: (bi, r, T_QKB * kq + GLA_HEADS + h)),
        pl.BlockSpec((None, rows, GLA_DV), lambda bi, h, r: (bi, r, T_VB * kv + h)),
        pl.BlockSpec((None, rows, GLA_DV), lambda bi, h, r: (bi, r, T_RB * kv + h)),
        pl.BlockSpec((None, rows, GLA_DK), lambda bi, h, r: (bi, r, h)),
        pl.BlockSpec(sub.shape, lambda bi, h, r: (0, 0)),
    ]
    args = [p3, p3, p3, p3, gk3, sub]
    if has_init:
        in_specs.append(pl.BlockSpec((None, None, GLA_DK, GLA_DV), lambda bi, h, r: (bi, h, 0, 0)))
        args.append(s0)
    kern = functools.partial(_gla_kernel, rows=rows, has_init=has_init)
    return pl.pallas_call(
        kern,
        grid=(b, GLA_HEADS, s // rows),
        in_specs=in_specs,
        out_specs=[
            pl.BlockSpec((None, rows, GLA_DV), lambda bi, h, r: (bi, r, h)),
            pl.BlockSpec((None, None, GLA_DK, GLA_DV), lambda bi, h, r: (bi, h, 0, 0)),
        ],
        out_shape=[
            jax.ShapeDtypeStruct((b, s, GLA_HEADS * GLA_DV), BF16),
            jax.ShapeDtypeStruct((b, GLA_HEADS, GLA_DK, GLA_DV), F32),
        ],
        scratch_shapes=[
            pltpu.VMEM((GLA_DK, GLA_DV), F32),
            pltpu.VMEM((SUB_BLOCK + rows, GLA_DK), F32),
            pltpu.VMEM((SUB_BLOCK + rows, GLA_DK), F32),
        ],
        compiler_params=pltpu.CompilerParams(
            dimension_semantics=("parallel", "parallel", "arbitrary"), vmem_limit_bytes=VMEM_LIMIT_BYTES),
        name="gla",
    )(*args)


def _mem_attn_kernel(q_ref, k_ref, v_ref, o_ref):
    outs = []
    for h in range(MEM_HEADS):
        sl = slice(h * MEM_HD, (h + 1) * MEM_HD)
        s = _dot_nt(q_ref[:, sl], k_ref[:, sl].astype(BF16))
        m = jnp.max(s, axis=-1, keepdims=True)
        p = jnp.exp(s - m)
        l = jnp.sum(p, axis=-1, keepdims=True)
        outs.append(_dot(p.astype(BF16), v_ref[:, sl].astype(BF16)) / l)
    o_ref[...] = jnp.concatenate(outs, axis=-1).astype(o_ref.dtype)


def _mem_attn(p3, mk, mv, tq):
    b, s, _ = p3.shape
    d = D_MODEL
    kv_block = (None,) + mk.shape[1:]
    return pl.pallas_call(
        _mem_attn_kernel,
        grid=(b, s // tq),
        in_specs=[
            pl.BlockSpec((None, tq, d), lambda bi, i: (bi, i, T_QM)),
            pl.BlockSpec(kv_block, lambda bi, i: (bi, 0, 0)),
            pl.BlockSpec(kv_block, lambda bi, i: (bi, 0, 0)),
        ],
        out_specs=pl.BlockSpec((None, tq, d), lambda bi, i: (bi, i, 0)),
        out_shape=jax.ShapeDtypeStruct((b, s, d), BF16),
        compiler_params=pltpu.CompilerParams(
            dimension_semantics=("parallel", "parallel"), vmem_limit_bytes=VMEM_LIMIT_BYTES),
        name="mem_attn",
    )(p3, mk, mv)


def _mix_kernel(x_ref, oa_ref, ob_ref, om_ref, ga_ref, gb_ref, gm_ref, wd_ref, wg_ref, wm_ref, wo_ref,
                y_ref):
    m = (ga_ref[...].astype(F32) * _dot(oa_ref[...], wd_ref[...])
         + gb_ref[...].astype(F32) * _dot(ob_ref[...], wg_ref[...])
         + gm_ref[...].astype(F32) * _dot(om_ref[...], wm_ref[...]))
    y_ref[...] = x_ref[...] + _dot(m.astype(BF16), wo_ref[...])


def _mix(x2, oa, ob, om, p2, wd, wg, wm, wo, tm):
    t = x2.shape[0]
    d = D_MODEL
    row = lambda i: (i, 0)
    full = lambda i: (0, 0)
    wspec = pl.BlockSpec((d, d), full, pipeline_mode=pl.Buffered(1))
    return pl.pallas_call(
        _mix_kernel,
        grid=(t // tm,),
        in_specs=[
            pl.BlockSpec((tm, d), row), pl.BlockSpec((tm, d), row), pl.BlockSpec((tm, d), row),
            pl.BlockSpec((tm, d), row),
            pl.BlockSpec((tm, d), lambda i: (i, T_GATE)),
            pl.BlockSpec((tm, d), lambda i: (i, T_GATE + 1)),
            pl.BlockSpec((tm, d), lambda i: (i, T_GATE + 2)),
            wspec, wspec, wspec, wspec,
        ],
        out_specs=pl.BlockSpec((tm, d), row),
        out_shape=jax.ShapeDtypeStruct((t, d), F32),
        compiler_params=pltpu.CompilerParams(
            dimension_semantics=("parallel",), vmem_limit_bytes=VMEM_LIMIT_BYTES),
        name="mix_out",
    )(x2, oa, ob, om, p2, p2, p2, wd, wg, wm, wo)


FFN_CHUNK = 256


def _ffn_kernel(x_ref, g_ref, wup_ref, cw_ref, cb_ref, wd_ref, cs_ref, y_ref, cso_ref,
                u_scr, carry_scr, gv_scr, *, n_seq, seq_rows, tiles_per_seq):
    i = pl.program_id(0)
    gap = SUBLANES
    stride = seq_rows + gap
    tail = CONV_W - 1
    n_chunks = D_FF // FFN_CHUNK
    x = x_ref[...]
    h = _rms_rows(x, g_ref[...]).astype(BF16)

    first = (i % tiles_per_seq) == 0
    for s in range(n_seq):
        base = s * stride

        @pl.when(first)
        def _():
            u_scr[base:base + gap, :] = jnp.zeros((gap, D_FF), F32)
            u_scr[base + gap - tail:base + gap, :] = cs_ref[s]

        @pl.when(jnp.logical_not(first))
        def _():
            u_scr[base:base + gap, :] = carry_scr[...]

    def up(c):
        cols = slice(c * FFN_CHUNK, (c + 1) * FFN_CHUNK)
        gate_cols = slice(D_FF + c * FFN_CHUNK, D_FF + (c + 1) * FFN_CHUNK)
        return _dot(h, wup_ref[:, cols]), _dot(h, wup_ref[:, gate_cols])

    def gated(c, u, vv):
        cols = slice(c * FFN_CHUNK, (c + 1) * FFN_CHUNK)
        cw = cw_ref[:, cols]
        outs = []
        for s in range(n_seq):
            base = s * stride + gap
            u_scr[base:base + seq_rows, cols] = u[s * seq_rows:(s + 1) * seq_rows]
            conv = cb_ref[:, cols]
            for j in range(CONV_W):
                off = base - tail + j
                conv = conv + cw[j:j + 1, :] * u_scr[off:off + seq_rows, cols]
            outs.append(conv)
        uc = jnp.concatenate(outs, axis=0) if n_seq > 1 else outs[0]
        gelu = 0.5 * uc * (1.0 + jnp.tanh(math.sqrt(2.0 / math.pi) * (uc + 0.044715 * (uc * uc * uc))))
        return (gelu * vv).astype(BF16)

    nxt = up(0)
    for c in range(n_chunks):
        cur = nxt
        if c + 1 < n_chunks:
            nxt = up(c + 1)
        gv_scr[:, c * FFN_CHUNK:(c + 1) * FFN_CHUNK] = gated(c, *cur)
    y_ref[...] = x + _dot(gv_scr[...], wd_ref[...])

    for s in range(n_seq):
        base = s * stride + gap
        cso_ref[s] = u_scr[base + seq_rows - tail:base + seq_rows, :]
    carry_scr[...] = u_scr[seq_rows:seq_rows + gap, :]


def _ffn(x2, g_ffn, w_up, conv_w, conv_b, w_down, conv_state, n_seq, seq_rows, tiles_per_seq):
    t = x2.shape[0]
    d = D_MODEL
    tm = n_seq * seq_rows
    nb = conv_state.shape[0]
    tail = CONV_W - 1
    kern = functools.partial(_ffn_kernel, n_seq=n_seq, seq_rows=seq_rows, tiles_per_seq=tiles_per_seq)
    y, tails = pl.pallas_call(
        kern,
        grid=(t // tm,),
        in_specs=[
            pl.BlockSpec((tm, d), lambda i: (i, 0)),
            _resident((1, d)),
            _resident(w_up.shape),
            _resident(conv_w.shape),
            _resident(conv_b.shape),
            _resident(w_down.shape),
            pl.BlockSpec((n_seq, tail, D_FF), lambda i: (i // tiles_per_seq, 0, 0)),
        ],
        out_specs=[
            pl.BlockSpec((tm, d), lambda i: (i, 0)),
            pl.BlockSpec((n_seq, tail, D_FF), lambda i: (i, 0, 0)),
        ],
        out_shape=[
            jax.ShapeDtypeStruct((t, d), F32),
            jax.ShapeDtypeStruct((nb * tiles_per_seq, tail, D_FF), F32),
        ],
        scratch_shapes=[
            pltpu.VMEM((n_seq * (seq_rows + SUBLANES), D_FF), F32),
            pltpu.VMEM((SUBLANES, D_FF), F32),
            pltpu.VMEM((tm, D_FF), BF16),
        ],
        compiler_params=pltpu.CompilerParams(
            dimension_semantics=("arbitrary",), vmem_limit_bytes=VMEM_LIMIT_BYTES),
        name="conv_ffn",
    )(x2, g_ffn, w_up, conv_w, conv_b, w_down, conv_state)
    return y, tails.reshape(nb, tiles_per_seq, tail, D_FF)[:, -1]


def _tile_gain(g, reps):
    return jnp.tile(g.astype(F32), reps).reshape(1, -1)


def _layer_weights(l, g_attn, w_in, w_gk2, b_gk, qn_diff, kn_diff, lam_q1, lam_k1, lam_q2, lam_k2,
                   subln_diff, subln_gla, g_mem, w_mem_kv, qn_mem, kn_mem, w_proj_diff, w_proj_gla,
                   w_proj_mem, w_out, g_ffn, w_up, conv_w, conv_b, w_down):
    d = D_MODEL
    w = w_in[l]
    lr0 = 6 * d
    wa = w[:, :lr0].astype(BF16)
    wb = w[:, lr0 + GK_RANK:].astype(BF16)
    wg1 = jnp.pad(w[:, lr0:lr0 + GK_RANK], ((0, 0), (0, LANES - GK_RANK))).astype(BF16)
    wg2 = jnp.pad(w_gk2[l], ((0, LANES - GK_RANK), (0, 0))).astype(BF16)
    return dict(
        g_attn=g_attn[l].reshape(1, d), wa=wa, wb=wb, wg1=wg1, wg2=wg2, bgk=b_gk[l].reshape(1, -1),
        qn_t=_tile_gain(qn_diff[l], d // DIFF_HD), kn_t=_tile_gain(kn_diff[l], d // DIFF_HD),
        qmn_t=_tile_gain(qn_mem[l], d // MEM_HD), knm_t=_tile_gain(kn_mem[l], d // MEM_HD),
        g64=_group_matrix(MXU_DIM, DIFF_HD), g256=_group_matrix(MEM_HD, MEM_HD),
        lamp=jnp.stack([lam_q1[l], lam_k1[l], lam_q2[l], lam_k2[l]]).astype(F32),
        sub_diff=subln_diff[l].reshape(1, -1), sub_gla=subln_gla[l].reshape(1, -1),
        g_mem=g_mem[l].reshape(1, d), w_mem_kv=w_mem_kv[l].astype(BF16),
        wd=w_proj_diff[l].astype(BF16), wg=w_proj_gla[l].astype(BF16), wm=w_proj_mem[l].astype(BF16),
        wo=w_out[l].astype(BF16), g_ffn=g_ffn[l].reshape(1, d), w_up=w_up[l].astype(BF16),
        conv_w=conv_w[l], conv_b=conv_b[l].reshape(1, -1), w_down=w_down[l].astype(BF16),
    )


def _group(x, wts, lam_init, mem_k, mem_v, past_k, past_v, gla_state, conv_state, prompt):
    b, s, d = x.shape
    t = b * s
    x2 = x.reshape(t, d)
    tm = _pick(t, 512)
    p2, ka, va, gk = _inproj(x2, wts["g_attn"], wts["wa"], wts["wb"], wts["wg1"], wts["wg2"], wts["bgk"],
                             wts["qn_t"], wts["kn_t"], wts["qmn_t"], wts["g64"], wts["g256"],
                             _pick(s, 256) if prompt else tm, s, prompt)
    if prompt:
        ka = ka.reshape(b, DIFF_HEADS, 2, DIFF_HD, s).transpose(0, 4, 1, 2, 3)
    else:
        ka = ka.reshape(b, s, DIFF_HEADS, 2, DIFF_HD)
    p3 = p2.reshape(b, s, N_PTILES * d)
    gk3 = gk.reshape(b, s, GLA_HEADS * GLA_DK)
    if prompt:
        oa = _diff_prompt(p3, wts["lamp"], wts["sub_diff"], lam_init, _pick(s, 512), 256)
        ob, gla_new = _gla(p3, gk3, wts["sub_gla"], None, _pick(s, 512))
    else:
        oa = _diff_sample(p3, past_k, past_v, wts["lamp"], wts["sub_diff"], lam_init)
        ob, gla_new = _gla(p3, gk3, wts["sub_gla"], gla_state, s)
    om = _mem_attn(p3, mem_k, mem_v, _pick(s, 512))
    x1 = _mix(x2, oa.reshape(t, d), ob.reshape(t, d), om.reshape(t, d), p2,
              wts["wd"], wts["wg"], wts["wm"], wts["wo"], tm)
    if prompt:
        rows = _pick(s, 512)
        y, cs = _ffn(x1, wts["g_ffn"], wts["w_up"], wts["conv_w"], wts["conv_b"], wts["w_down"],
                     conv_state, 1, rows, s // rows)
    else:
        y, cs = _ffn(x1, wts["g_ffn"], wts["w_up"], wts["conv_w"], wts["conv_b"], wts["w_down"],
                     conv_state, b, s, 1)
    return y.reshape(b, s, d), ka, va, gla_new, cs


def kernel(x_prompt, x_sample, mem_prompt, cache_diff_k, cache_diff_v, cache_mem_k, cache_mem_v, state_gla, state_conv, g_attn, w_in, w_gk2, b_gk, qn_diff, kn_diff, lam_q1, lam_k1, lam_q2, lam_k2, subln_diff, subln_gla, g_mem, w_mem_kv, qn_mem, kn_mem, w_proj_diff, w_proj_gla, w_proj_mem, w_out, g_ffn, w_up, conv_w, conv_b, w_down):
    depth = g_attn.shape[0]
    d = D_MODEL
    xp, xs = x_prompt, x_sample
    bp, sp, _ = xp.shape
    bs, ss, _ = xs.shape
    n_mem = mem_prompt.shape[1]
    outs = [[] for _ in range(10)]
    for l in range(depth):
        lam_init = 0.8 - 0.6 * math.exp(-0.3 * l)
        wts = _layer_weights(l, g_attn, w_in, w_gk2, b_gk, qn_diff, kn_diff, lam_q1, lam_k1, lam_q2,
                             lam_k2, subln_diff, subln_gla, g_mem, w_mem_kv, qn_mem, kn_mem,
                             w_proj_diff, w_proj_gla, w_proj_mem, w_out, g_ffn, w_up, conv_w, conv_b,
                             w_down)
        mk, mv = _memkv(mem_prompt.reshape(bp * n_mem, d), wts["g_mem"], wts["w_mem_kv"], wts["knm_t"],
                        wts["g256"])
        mk = mk.reshape(bp, n_mem, d)
        mv = mv.reshape(bp, n_mem, d)
        xp, kp, vp, gp, cp = _group(xp, wts, lam_init, mk, mv, None, None, None,
                                    jnp.zeros((bp, CONV_W - 1, D_FF), F32), True)
        xs, ks_, vs_, gs, cs = _group(
            xs, wts, lam_init, cache_mem_k[l].reshape(bs, n_mem, d), cache_mem_v[l].reshape(bs, n_mem, d),
            cache_diff_k[l].transpose(0, 2, 3, 4, 1).reshape(bs, d, -1), cache_diff_v[l].reshape(bs, -1, 2 * DIFF_HD),
            state_gla[l], state_conv[l], False)
        vals = (kp, vp.reshape(bp, sp, DIFF_HEADS, 2 * DIFF_HD),
                mk.reshape(bp, n_mem, MEM_HEADS, MEM_HD), mv.reshape(bp, n_mem, MEM_HEADS, MEM_HD), gp, cp,
                ks_, vs_.reshape(bs, ss, DIFF_HEADS, 2 * DIFF_HD), gs, cs)
        for o, v in zip(outs, vals):
            o.append(v)
    return (xp, xs) + tuple(jnp.stack(o) for o in outs)
```

```python
import functools
import math

import jax
import jax.numpy as jnp
from jax import lax
from jax.experimental import pallas as pl
from jax.experimental.pallas import tpu as pltpu

F32 = jnp.float32
BF16 = jnp.bfloat16

D_MODEL = 1024
CHUNK = 64
EPS = 1e-6
DIFF_HEADS = 8
DIFF_HD = 64
DIFF_SCALE = DIFF_HD ** -0.5
LOG2E = math.log2(math.e)
GLA_HEADS = 4
GLA_DK = 128
GLA_DV = 256
GLA_SCALE = GLA_DK ** -0.5
GK_RANK = 16
GK_NORM = 16.0
MEM_HEADS = 4
MEM_HD = 256
MEM_SCALE = MEM_HD ** -0.5
D_FF = 2816
CONV_W = 3

LANES = 128
SUBLANES = 8
MXU_DIM = 256
VMEM_LIMIT_BYTES = 48 * 1024 * 1024

T_QA, T_KA, T_VA, T_QKB, T_VB, T_RB, T_QM, T_GATE = 0, 1, 2, 3, 4, 5, 6, 7
N_PTILES = 10
SUB_BLOCK = 8
NEG_BIG = -1e30


def _dot(a, b):
    return jnp.dot(a, b, preferred_element_type=F32)


def _dot_nt(a, b):
    return lax.dot_general(a, b, (((1,), (1,)), ((), ())), preferred_element_type=F32)


def _sigmoid(x):
    return 1.0 / (1.0 + jnp.exp(-x))


def _pick(n, pref):
    t = min(n, pref)
    while n % t:
        t -= 1
    return t


def _rms_rows(x, gain):
    ms = jnp.mean(x * x, axis=-1, keepdims=True)
    return x * lax.rsqrt(ms + EPS) * gain


def _group_rms(y, gmat, gain):
    slab = gmat.shape[0]
    outs = []
    for c in range(y.shape[-1] // slab):
        ys = y[:, c * slab:(c + 1) * slab]
        ms = _dot((ys * ys).astype(BF16), gmat)
        outs.append(ys * lax.rsqrt(ms + EPS))
    return jnp.concatenate(outs, axis=-1) * gain


def _group_matrix(slab, group):
    r = jnp.arange(slab) // group
    return jnp.where(r[:, None] == r[None, :], 1.0 / group, 0.0).astype(BF16)


def _inproj_kernel(x_ref, g_ref, wa_ref, wb_ref, wg1_ref, wg2_ref, bgk_ref, qn_ref, kn_ref, qmn_ref,
                   g64_ref, g256_ref, p_ref, ka_ref, vaf_ref, gk_ref, *, k_transposed):
    d = D_MODEL
    n_a = wa_ref.shape[1] // d
    h = _rms_rows(x_ref[...], g_ref[...]).astype(BF16)

    def project(j):
        if j < n_a:
            return _dot(h, wa_ref[:, j * d:(j + 1) * d])
        return _dot(h, wb_ref[:, (j - n_a) * d:(j - n_a + 1) * d])

    def finish(j, acc):
        if j == T_QA:
            out = _group_rms(acc, g64_ref[...], qn_ref[...]) * (DIFF_SCALE * LOG2E)
        elif j == T_KA:
            out = _group_rms(acc, g64_ref[...], kn_ref[...])
            ka_ref[...] = out.T if k_transposed else out
        elif j == T_VA:
            vaf_ref[...] = acc
            out = acc
        elif j == T_QKB:
            half = d // 2
            out = jnp.concatenate([acc[:, :half] * GLA_SCALE, acc[:, half:]], axis=-1)
            lr = _dot(h, wg1_ref[...])
            z = _dot(lr.astype(BF16), wg2_ref[...]) + bgk_ref[...]
            log_sig = jnp.minimum(z, 0.0) - jnp.log(1.0 + jnp.exp(-jnp.abs(z)))
            gk_ref[...] = log_sig * (1.0 / GK_NORM)
        elif j == T_VB:
            out = acc
        elif j == T_RB:
            out = acc * _sigmoid(acc)
        elif j == T_QM:
            out = _group_rms(acc, g256_ref[...], qmn_ref[...]) * MEM_SCALE
        else:
            out = _sigmoid(acc)
        p_ref[:, j * d:(j + 1) * d] = out.astype(BF16)

    acc_next = project(0)
    for j in range(N_PTILES):
        acc = acc_next
        if j + 1 < N_PTILES:
            acc_next = project(j + 1)
        finish(j, acc)


def _resident(shape):
    return pl.BlockSpec(shape, lambda *_: (0,) * len(shape), pipeline_mode=pl.Buffered(1))


def _inproj(x2, g_attn, wa, wb, wg1, wg2, bgk, qn_t, kn_t, qmn_t, g64, g256, tm, seq_len, k_transposed):
    t = x2.shape[0]
    d = D_MODEL
    row = lambda i: (i, 0)
    if k_transposed:
        per_seq = seq_len // tm
        ka_spec = pl.BlockSpec((None, d, tm), lambda i: (i // per_seq, 0, i % per_seq))
        ka_shape = jax.ShapeDtypeStruct((t // seq_len, d, seq_len), F32)
    else:
        ka_spec = pl.BlockSpec((tm, d), row)
        ka_shape = jax.ShapeDtypeStruct((t, d), F32)
    small = [wg1, wg2, bgk, qn_t, kn_t, qmn_t, g64, g256]
    return pl.pallas_call(
        functools.partial(_inproj_kernel, k_transposed=k_transposed),
        grid=(t // tm,),
        in_specs=[pl.BlockSpec((tm, d), row), _resident((1, d)), _resident(wa.shape), _resident(wb.shape)]
        + [_resident(a.shape) for a in small],
        out_specs=[
            pl.BlockSpec((tm, N_PTILES * d), row),
            ka_spec,
            pl.BlockSpec((tm, d), row),
            pl.BlockSpec((tm, GLA_HEADS * GLA_DK), row),
        ],
        out_shape=[
            jax.ShapeDtypeStruct((t, N_PTILES * d), BF16),
            ka_shape,
            jax.ShapeDtypeStruct((t, d), F32),
            jax.ShapeDtypeStruct((t, GLA_HEADS * GLA_DK), F32),
        ],
        compiler_params=pltpu.CompilerParams(
            dimension_semantics=("parallel",), vmem_limit_bytes=VMEM_LIMIT_BYTES),
        name="inproj",
    )(x2, g_attn, wa, wb, *small)


def _memkv_kernel(m_ref, g_ref, w_ref, kn_ref, g256_ref, k_ref, v_ref):
    j = pl.program_id(0)
    h = _rms_rows(m_ref[...], g_ref[...]).astype(BF16)
    acc = _dot(h, w_ref[...])

    @pl.when(j == 0)
    def _():
        k_ref[...] = _group_rms(acc, g256_ref[...], kn_ref[...])

    @pl.when(j == 1)
    def _():
        v_ref[...] = acc


def _memkv(mem2, g_mem, w_kv, knm_t, g256):
    t = mem2.shape[0]
    d = D_MODEL
    full = lambda j: (0, 0)
    return pl.pallas_call(
        _memkv_kernel,
        grid=(2,),
        in_specs=[
            pl.BlockSpec((t, d), full),
            pl.BlockSpec((1, d), full),
            pl.BlockSpec((d, d), lambda j: (0, j)),
            pl.BlockSpec((1, d), full),
            pl.BlockSpec(g256.shape, full),
        ],
        out_specs=[pl.BlockSpec((t, d), full), pl.BlockSpec((t, d), full)],
        out_shape=[jax.ShapeDtypeStruct((t, d), F32), jax.ShapeDtypeStruct((t, d), F32)],
        compiler_params=pltpu.CompilerParams(
            dimension_semantics=("arbitrary",), vmem_limit_bytes=VMEM_LIMIT_BYTES),
        name="memkv",
    )(mem2, g_mem, w_kv, knm_t, g256)


def _lambda_value(lamp, lam_init):
    a = jnp.sum(lamp[0:1, :] * lamp[1:2, :], axis=-1, keepdims=True)
    b = jnp.sum(lamp[2:3, :] * lamp[3:4, :], axis=-1, keepdims=True)
    return jnp.exp(a) - jnp.exp(b) + lam_init


def _stack_maps(q):
    lane = lax.broadcasted_iota(jnp.int32, q.shape, 1)
    zero = jnp.zeros_like(q)
    return jnp.concatenate([jnp.where(lane < DIFF_HD, q, zero), jnp.where(lane >= DIFF_HD, q, zero)], axis=0)


def _diff_finish(acc, l, lam, sub, lam_init, tq):
    o2 = acc / l
    o = o2[:tq] - lam * o2[tq:]
    return _rms_rows(o, sub) * (1.0 - lam_init)


ONES_ROWS = 16


def _diff_prompt_kernel(q_ref, k_ref, v_ref, lamp_ref, sub_ref, o_ref, qt_scr, vt_scr, m_scr, acc_scr,
                        st0_scr, st1_scr, p0_scr, p1_scr, alpha0_scr, alpha1_scr, mx0_scr, mx1_scr,
                        *, tq, tk, lam_init):
    qi = pl.program_id(2)
    n_kv = v_ref.shape[0] // tk
    hd2 = v_ref.shape[1]
    halves = tq // tk
    assert halves == 2
    all_cgs = list(range(2 * halves))
    upper_cgs = [g * halves + 1 for g in range(2)]

    @pl.when(qi == 0)
    def _():
        for c in range(n_kv):
            vt_scr[c, 0:hd2, :] = v_ref[c * tk:(c + 1) * tk, :].astype(F32).T.astype(BF16)
            vt_scr[c, hd2:hd2 + ONES_ROWS, :] = jnp.ones((ONES_ROWS, tk), BF16)

    qt_scr[...] = _stack_maps(q_ref[...]).astype(F32).T.astype(BF16)
    m_scr[...] = jnp.full(m_scr.shape, -jnp.inf, F32)
    acc_scr[...] = jnp.zeros(acc_scr.shape, F32)

    def scores(kb, cgs):
        k = k_ref[pl.ds(pl.multiple_of(kb * tk, tk), tk), :]
        sts = [_dot(k, qt_scr[:, cg * tk:(cg + 1) * tk]) for cg in cgs]
        return sts, [jnp.max(st, axis=0, keepdims=True) for st in sts]

    def softmax(sts, maxes, slot, cgs, masked):
        pts, alphas = [], []
        for st, mx, cg, msk in zip(sts, maxes, cgs, masked):
            cols = slice(cg * tk, (cg + 1) * tk)
            if msk:
                r = lax.broadcasted_iota(jnp.int32, st.shape, 0)
                c = lax.broadcasted_iota(jnp.int32, st.shape, 1)
                st = jnp.where((r // CHUNK) <= (c // CHUNK), st, -jnp.inf)
                mx = jnp.max(st, axis=0, keepdims=True)
            m_prev = m_scr[slot, :, cols]
            m_new = jnp.maximum(m_prev, mx)
            alphas.append(jnp.exp2(m_prev - m_new))
            pts.append(jnp.exp2(st - m_new).astype(BF16))
            m_scr[slot, :, cols] = m_new
        return pts, alphas

    def accumulate(kb, slot, cgs, pts, alphas):
        vt = vt_scr[kb]
        for pt, alpha, cg in zip(pts, alphas, cgs):
            cols = slice(cg * tk, (cg + 1) * tk)
            acc_scr[slot, :, cols] = alpha * acc_scr[slot, :, cols] + _dot(vt, pt)

    st_scr, p_scr, alpha_scr = (st0_scr, st1_scr), (p0_scr, p1_scr), (alpha0_scr, alpha1_scr)
    mx_scr = (mx0_scr, mx1_scr)

    def put(scr, buf, cgs, vals):
        for cg, val in zip(cgs, vals):
            scr[buf][cg] = val

    def get(scr, buf, cgs):
        return [scr[buf][cg] for cg in cgs]

    def put_scores(buf, cgs, sts_maxes):
        put(st_scr, buf, cgs, sts_maxes[0])
        put(mx_scr, buf, cgs, sts_maxes[1])

    def trip(t, par):
        pts, alphas = softmax(get(st_scr, par, all_cgs), get(mx_scr, par, all_cgs), par, all_cgs,
                              [False] * len(all_cgs))
        put(p_scr, par, all_cgs, pts)
        put(alpha_scr, par, all_cgs, alphas)
        put_scores(1 - par, all_cgs, scores(jnp.minimum(t + 1, n_kv - 1), all_cgs))
        accumulate(jnp.maximum(t - 1, 0), 1 - par, all_cgs,
                   get(p_scr, 1 - par, all_cgs), get(alpha_scr, 1 - par, all_cgs))

    def four_trips(i, carry):
        for u in range(4):
            trip(4 * i + u, u % 2)
        return carry

    n_trips = qi * halves
    put_scores(0, all_cgs, scores(0, all_cgs))
    put(p_scr, 1, all_cgs, [jnp.zeros((tk, tk), BF16)] * len(all_cgs))
    put(alpha_scr, 1, all_cgs, [jnp.ones((1, tk), F32)] * len(all_cgs))
    lax.fori_loop(0, n_trips // 4, four_trips, 0)

    @pl.when(n_trips % 4 == 2)
    def _():
        trip(n_trips - 2, 0)
        trip(n_trips - 1, 1)

    kb0 = n_trips
    upper = scores(kb0 + 1, upper_cgs)
    accumulate(jnp.maximum(kb0 - 1, 0), 1, all_cgs, get(p_scr, 1, all_cgs), get(alpha_scr, 1, all_cgs))
    pts, alphas = softmax(get(st_scr, 0, all_cgs), get(mx_scr, 0, all_cgs), 0, all_cgs,
                          [cg % halves == 0 for cg in all_cgs])
    accumulate(kb0, 0, all_cgs, pts, alphas)
    pts, alphas = softmax(upper[0], upper[1], 1, upper_cgs, [True] * len(upper_cgs))
    accumulate(kb0 + 1, 1, upper_cgs, pts, alphas)

    m_all = jnp.maximum(m_scr[0], m_scr[1])
    acc_all = jnp.exp2(m_scr[0] - m_all) * acc_scr[0] + jnp.exp2(m_scr[1] - m_all) * acc_scr[1]

    lam = _lambda_value(lamp_ref[...], lam_init)
    o2t = acc_all[0:hd2, :] / acc_all[hd2:hd2 + 1, :]
    o = (o2t[:, :tq] - lam * o2t[:, tq:]).T
    o_ref[...] = (_rms_rows(o, sub_ref[...]) * (1.0 - lam_init)).astype(o_ref.dtype)


def _diff_prompt(p3, lamp, sub, lam_init, tq, tk):
    b, s, _ = p3.shape
    hd2 = 2 * DIFF_HD
    kern = functools.partial(_diff_prompt_kernel, tq=tq, tk=tk, lam_init=lam_init)
    per_d = D_MODEL // hd2
    n_cg = 2 * tq // tk
    return pl.pallas_call(
        kern,
        grid=(b, DIFF_HEADS, s // tq),
        in_specs=[
            pl.BlockSpec((None, tq, hd2), lambda bi, h, qi: (bi, qi, T_QA * per_d + h)),
            pl.BlockSpec((None, s, hd2), lambda bi, h, qi: (bi, 0, T_KA * per_d + h)),
            pl.BlockSpec((None, s, hd2), lambda bi, h, qi: (bi, 0, T_VA * per_d + h)),
            pl.BlockSpec(lamp.shape, lambda bi, h, qi: (0, 0)),
            pl.BlockSpec(sub.shape, lambda bi, h, qi: (0, 0)),
        ],
        out_specs=pl.BlockSpec((None, tq, hd2), lambda bi, h, qi: (bi, qi, h)),
        out_shape=jax.ShapeDtypeStruct((b, s, D_MODEL), BF16),
        scratch_shapes=[
            pltpu.VMEM((hd2, 2 * tq), BF16),
            pltpu.VMEM((s // tk, hd2 + ONES_ROWS, tk), BF16),
            pltpu.VMEM((2, 1, 2 * tq), F32),
            pltpu.VMEM((2, hd2 + ONES_ROWS, 2 * tq), F32),
            pltpu.VMEM((n_cg, tk, tk), F32),
            pltpu.VMEM((n_cg, tk, tk), F32),
            pltpu.VMEM((n_cg, tk, tk), BF16),
            pltpu.VMEM((n_cg, tk, tk), BF16),
            pltpu.VMEM((n_cg, 1, tk), F32),
            pltpu.VMEM((n_cg, 1, tk), F32),
            pltpu.VMEM((n_cg, 1, tk), F32),
            pltpu.VMEM((n_cg, 1, tk), F32),
        ],
        compiler_params=pltpu.CompilerParams(
            dimension_semantics=("parallel", "parallel", "arbitrary"), vmem_limit_bytes=VMEM_LIMIT_BYTES),
        name="diff_attn_prompt",
    )(p3, p3, p3, lamp, sub)


SAMPLE_HEAD_GROUP = 4


def _diff_sample_kernel(q_ref, kn_ref, vn_ref, kc_ref, vc_ref, lamp_ref, sub_ref, o_ref, *, lam_init):
    tq = q_ref.shape[0]
    hd2 = 2 * DIFF_HD
    n_past = kc_ref.shape[1]
    lam = _lambda_value(lamp_ref[...], lam_init)
    for h0 in range(0, DIFF_HEADS, SAMPLE_HEAD_GROUP):
        heads = range(h0, h0 + SAMPLE_HEAD_GROUP)
        cols = [slice(h * hd2, (h + 1) * hd2) for h in heads]
        q2s = [_stack_maps(q_ref[:, c]) for c in cols]
        s_cs = [_dot(q2, kc_ref[c, :].astype(BF16)) for q2, c in zip(q2s, cols)]
        s_ns = [_dot_nt(q2, kn_ref[:, c]) for q2, c in zip(q2s, cols)]
        pcs, pns, ls = [], [], []
        for s_c, s_n in zip(s_cs, s_ns):
            m = jnp.maximum(jnp.max(s_c, axis=-1, keepdims=True), jnp.max(s_n, axis=-1, keepdims=True))
            p_c = jnp.exp2(s_c - m)
            p_n = jnp.exp2(s_n - m)
            ls.append(jnp.sum(p_c, axis=-1, keepdims=True) + jnp.sum(p_n, axis=-1, keepdims=True))
            pcs.append(p_c.astype(BF16))
            pns.append(p_n.astype(BF16))
        for h, c, p_c, p_n, l in zip(heads, cols, pcs, pns, ls):
            vc = vc_ref[pl.ds(h, n_past, stride=DIFF_HEADS), :]
            acc = _dot(p_c, vc.astype(BF16)) + _dot(p_n, vn_ref[:, c])
            o_ref[:, c] = _diff_finish(acc, l, lam, sub_ref[...], lam_init, tq).astype(o_ref.dtype)


def _diff_sample(p3, kc_t, vc, lamp, sub, lam_init):
    b, l, _ = p3.shape
    n_past = kc_t.shape[2]
    hd2 = 2 * DIFF_HD
    d = D_MODEL
    kern = functools.partial(_diff_sample_kernel, lam_init=lam_init)
    return pl.pallas_call(
        kern,
        grid=(b,),
        in_specs=[
            pl.BlockSpec((None, l, d), lambda bi: (bi, 0, T_QA)),
            pl.BlockSpec((None, l, d), lambda bi: (bi, 0, T_KA)),
            pl.BlockSpec((None, l, d), lambda bi: (bi, 0, T_VA)),
            pl.BlockSpec((None, d, n_past), lambda bi: (bi, 0, 0)),
            pl.BlockSpec((None, n_past * DIFF_HEADS, hd2), lambda bi: (bi, 0, 0)),
            pl.BlockSpec(lamp.shape, lambda bi: (0, 0)),
            pl.BlockSpec(sub.shape, lambda bi: (0, 0)),
        ],
        out_specs=pl.BlockSpec((None, l, d), lambda bi: (bi, 0, 0)),
        out_shape=jax.ShapeDtypeStruct((b, l, D_MODEL), BF16),
        compiler_params=pltpu.CompilerParams(
            dimension_semantics=("parallel",), vmem_limit_bytes=VMEM_LIMIT_BYTES),
        name="diff_attn_sample",
    )(p3, p3, p3, kc_t, vc, lamp, sub)


def _bcast_rows(x, period, row):
    r, c = x.shape
    x3 = x.reshape(r // period, period, c)
    return jnp.broadcast_to(x3[:, row:row + 1, :], x3.shape).reshape(r, c)


def _gla_kernel(*refs, rows, has_init):
    if has_init:
        q_ref, k_ref, v_ref, r_ref, g_ref, sub_ref, s0_ref, o_ref, sout_ref, st_scr, kf_scr, b_scr = refs
    else:
        q_ref, k_ref, v_ref, r_ref, g_ref, sub_ref, o_ref, sout_ref, st_scr, kf_scr, b_scr = refs
        s0_ref = None
    step = pl.program_id(2)
    n_chunks = rows // CHUNK
    n_sub = CHUNK // SUB_BLOCK

    @pl.when(step == 0)
    def _():
        if has_init:
            st_scr[...] = s0_ref[...]
        else:
            st_scr[...] = jnp.zeros(st_scr.shape, F32)

    q = q_ref[...].astype(F32)
    k = k_ref[...].astype(F32)
    g = g_ref[...]
    v = v_ref[...]
    chunk_rows = [slice(c * CHUNK, (c + 1) * CHUNK) for c in range(n_chunks)]

    ri = lax.broadcasted_iota(jnp.int32, (CHUNK, CHUNK), 0)
    ci = lax.broadcasted_iota(jnp.int32, (CHUNK, CHUNK), 1)
    tri = jnp.where(ci <= ri, 1.0, 0.0).astype(BF16)
    gw = jnp.concatenate([g[sl] for sl in chunk_rows], axis=1)
    g1 = gw.astype(BF16)
    rem = gw - g1.astype(F32)
    g2 = rem.astype(BF16)
    g3 = (rem - g2.astype(F32)).astype(BF16)
    bw = _dot(tri, g1) + _dot(tri, g2) + _dot(tri, g3)
    b = jnp.concatenate([bw[:, c * GLA_DK:(c + 1) * GLA_DK] for c in range(n_chunks)], axis=0)
    bex = b - g

    b_last = _bcast_rows(b, CHUNK, CHUNK - 1)
    b_blk = _bcast_rows(bex, SUB_BLOCK, 0)
    q_blk = q * jnp.exp(b - b_blk)
    q_chk = q * jnp.exp(b)
    k_end = k * jnp.exp(b_last - b)

    rowc = lax.broadcasted_iota(jnp.int32, (rows, GLA_DK), 0) % CHUNK
    lane = lax.broadcasted_iota(jnp.int32, (rows, GLA_DK), 1)
    zero = jnp.zeros_like(q)

    lhs_parts, rhs_parts = [], []
    for blk in range(1, n_sub):
        b_ref_blk = _bcast_rows(bex, CHUNK, blk * SUB_BLOCK)
        k_blk = k * jnp.exp(jnp.where(rowc < blk * SUB_BLOCK, b_ref_blk - b, NEG_BIG))
        lhs_parts.append(jnp.where((rowc // SUB_BLOCK) == blk, q_blk, zero))
        rhs_parts.append(k_blk)
    lhs = jnp.concatenate(lhs_parts, axis=-1).astype(BF16)
    rhs = jnp.concatenate(rhs_parts, axis=-1).astype(BF16)

    pad = SUB_BLOCK
    kf_scr[0:pad, :] = jnp.zeros((pad, GLA_DK), F32)
    b_scr[0:pad, :] = jnp.zeros((pad, GLA_DK), F32)
    kf_scr[pad:pad + rows, :] = k
    b_scr[pad:pad + rows, :] = b
    ones = jnp.ones((GLA_DK, LANES), BF16)
    delta = rowc - lane
    row_sub = rowc % SUB_BLOCK
    a_diag = jnp.zeros((rows, LANES), F32)
    for d in range(SUB_BLOCK):
        kd = kf_scr[pad - d:pad - d + rows, :]
        bd = b_scr[pad - d:pad - d + rows, :]
        e = jnp.exp(jnp.where(row_sub >= d, b - bd, NEG_BIG))
        rd = _dot((q * kd * e).astype(BF16), ones)
        a_diag = a_diag + jnp.where(delta == d, rd, 0.0)

    atts = [(_dot_nt(lhs[sl], rhs[sl]) + a_diag[sl, :CHUNK]).astype(BF16) for sl in chunk_rows]
    d_sts = [_dot(k_end[sl].T.astype(BF16), v[sl]) for sl in chunk_rows]
    o_intra = [_dot(att, v[sl]) for att, sl in zip(atts, chunk_rows)]
    decs = [jnp.exp(b[sl].T[:, CHUNK - 1:CHUNK]) for sl in chunk_rows]
    st = st_scr[...]
    outs = []
    for c, sl in enumerate(chunk_rows):
        outs.append(o_intra[c] + _dot(q_chk[sl].astype(BF16), st.astype(BF16)))
        st = st * decs[c] + d_sts[c]
    st_scr[...] = st
    o = jnp.concatenate(outs, axis=0) if n_chunks > 1 else outs[0]
    o_ref[...] = (_rms_rows(o, sub_ref[...]) * r_ref[...].astype(F32)).astype(o_ref.dtype)

    @pl.when(step == pl.num_programs(2) - 1)
    def _():
        sout_ref[...] = st_scr[...]


def _gla(p3, gk3, sub, s0, rows):
    b, s, _ = p3.shape
    has_init = s0 is not None
    kq = D_MODEL // GLA_DK
    kv = D_MODEL // GLA_DV
    in_specs = [
        pl.BlockSpec((None, rows, GLA_DK), lambda bi, h, r: (bi, r, T_QKB * kq + h)),
        pl.BlockSpec((None, rows, GLA_DK), lambda bi, h, r: (bi, r, T_QKB * kq + GLA_HEADS + h)),
        pl.BlockSpec((None, rows, GLA_DV), lambda bi, h, r: (bi, r, T_VB * kv + h)),
        pl.BlockSpec((None, rows, GLA_DV), lambda bi, h, r: (bi, r, T_RB * kv + h)),
        pl.BlockSpec((None, rows, GLA_DK), lambda bi, h, r: (bi, r, h)),
        pl.BlockSpec(sub.shape, lambda bi, h, r: (0, 0)),
    ]
    args = [p3, p3, p3, p3, gk3, sub]
    if has_init:
        in_specs.append(pl.BlockSpec((None, None, GLA_DK, GLA_DV), lambda bi, h, r: (bi, h, 0, 0)))
        args.append(s0)
    kern = functools.partial(_gla_kernel, rows=rows, has_init=has_init)
    return pl.pallas_call(
        kern,
        grid=(b, GLA_HEADS, s // rows),
        in_specs=in_specs,
        out_specs=[
            pl.BlockSpec((None, rows, GLA_DV), lambda bi, h, r: (bi, r, h)),
            pl.BlockSpec((None, None, GLA_DK, GLA_DV), lambda bi, h, r: (bi, h, 0, 0)),
        ],
        out_shape=[
            jax.ShapeDtypeStruct((b, s, GLA_HEADS * GLA_DV), BF16),
            jax.ShapeDtypeStruct((b, GLA_HEADS, GLA_DK, GLA_DV), F32),
        ],
        scratch_shapes=[
            pltpu.VMEM((GLA_DK, GLA_DV), F32),
            pltpu.VMEM((SUB_BLOCK + rows, GLA_DK), F32),
            pltpu.VMEM((SUB_BLOCK + rows, GLA_DK), F32),
        ],
        compiler_params=pltpu.CompilerParams(
            dimension_semantics=("parallel", "parallel", "arbitrary"), vmem_limit_bytes=VMEM_LIMIT_BYTES),
        name="gla",
    )(*args)


def _mem_attn_kernel(q_ref, k_ref, v_ref, o_ref, *, tiled):
    def head(ref, h):
        if not tiled:
            return ref[:, h * MEM_HD:(h + 1) * MEM_HD]
        period = MEM_HEADS * MEM_HD // LANES
        n_tok = ref.shape[0] // period
        return jnp.concatenate([ref[pl.ds(half * MEM_HEADS + h, n_tok, stride=period), :]
                                for half in range(MEM_HD // LANES)], axis=1)

    outs = []
    for h in range(MEM_HEADS):
        sl = slice(h * MEM_HD, (h + 1) * MEM_HD)
        s = _dot_nt(q_ref[:, sl], head(k_ref, h).astype(BF16))
        m = jnp.max(s, axis=-1, keepdims=True)
        p = jnp.exp(s - m)
        l = jnp.sum(p, axis=-1, keepdims=True)
        outs.append(_dot(p.astype(BF16), head(v_ref, h).astype(BF16)) / l)
    o_ref[...] = jnp.concatenate(outs, axis=-1).astype(o_ref.dtype)


def _mem_attn(p3, mk, mv, tq):
    b, s, _ = p3.shape
    d = D_MODEL
    kv_block = (None,) + mk.shape[1:]
    return pl.pallas_call(
        functools.partial(_mem_attn_kernel, tiled=mk.shape[2] == LANES),
        grid=(b, s // tq),
        in_specs=[
            pl.BlockSpec((None, tq, d), lambda bi, i: (bi, i, T_QM)),
            pl.BlockSpec(kv_block, lambda bi, i: (bi, 0, 0)),
            pl.BlockSpec(kv_block, lambda bi, i: (bi, 0, 0)),
        ],
        out_specs=pl.BlockSpec((None, tq, d), lambda bi, i: (bi, i, 0)),
        out_shape=jax.ShapeDtypeStruct((b, s, d), BF16),
        compiler_params=pltpu.CompilerParams(
            dimension_semantics=("parallel", "parallel"), vmem_limit_bytes=VMEM_LIMIT_BYTES),
        name="mem_attn",
    )(p3, mk, mv)


def _mix_kernel(x_ref, oa_ref, ob_ref, om_ref, ga_ref, gb_ref, gm_ref, wd_ref, wg_ref, wm_ref, wo_ref,
                y_ref):
    m = (ga_ref[...].astype(F32) * _dot(oa_ref[...], wd_ref[...])
         + gb_ref[...].astype(F32) * _dot(ob_ref[...], wg_ref[...])
         + gm_ref[...].astype(F32) * _dot(om_ref[...], wm_ref[...]))
    y_ref[...] = x_ref[...] + _dot(m.astype(BF16), wo_ref[...])


def _mix(x2, oa, ob, om, p2, wd, wg, wm, wo, tm):
    t = x2.shape[0]
    d = D_MODEL
    row = lambda i: (i, 0)
    full = lambda i: (0, 0)
    wspec = pl.BlockSpec((d, d), full, pipeline_mode=pl.Buffered(1))
    return pl.pallas_call(
        _mix_kernel,
        grid=(t // tm,),
        in_specs=[
            pl.BlockSpec((tm, d), row), pl.BlockSpec((tm, d), row), pl.BlockSpec((tm, d), row),
            pl.BlockSpec((tm, d), row),
            pl.BlockSpec((tm, d), lambda i: (i, T_GATE)),
            pl.BlockSpec((tm, d), lambda i: (i, T_GATE + 1)),
            pl.BlockSpec((tm, d), lambda i: (i, T_GATE + 2)),
            wspec, wspec, wspec, wspec,
        ],
        out_specs=pl.BlockSpec((tm, d), row),
        out_shape=jax.ShapeDtypeStruct((t, d), F32),
        compiler_params=pltpu.CompilerParams(
            dimension_semantics=("parallel",), vmem_limit_bytes=VMEM_LIMIT_BYTES),
        name="mix_out",
    )(x2, oa, ob, om, p2, p2, p2, wd, wg, wm, wo)


FFN_CHUNK = 256


def _ffn_kernel(x_ref, g_ref, wup_ref, cw_ref, cb_ref, wd_ref, cs_ref, y_ref, cso_ref,
                u_scr, carry_scr, gv_scr, *, n_seq, seq_rows, tiles_per_seq):
    i = pl.program_id(0)
    gap = SUBLANES
    stride = seq_rows + gap
    tail = CONV_W - 1
    n_chunks = D_FF // FFN_CHUNK
    x = x_ref[...]
    h = _rms_rows(x, g_ref[...]).astype(BF16)

    first = (i % tiles_per_seq) == 0
    for s in range(n_seq):
        base = s * stride

        @pl.when(first)
        def _():
            u_scr[base:base + gap, :] = jnp.zeros((gap, D_FF), F32)
            u_scr[base + gap - tail:base + gap, :] = cs_ref[s]

        @pl.when(jnp.logical_not(first))
        def _():
            u_scr[base:base + gap, :] = carry_scr[...]

    def up(c):
        cols = slice(c * FFN_CHUNK, (c + 1) * FFN_CHUNK)
        gate_cols = slice(D_FF + c * FFN_CHUNK, D_FF + (c + 1) * FFN_CHUNK)
        return _dot(h, wup_ref[:, cols]), _dot(h, wup_ref[:, gate_cols])

    def gated(c, u, vv):
        cols = slice(c * FFN_CHUNK, (c + 1) * FFN_CHUNK)
        cw = cw_ref[:, cols]
        outs = []
        for s in range(n_seq):
            base = s * stride + gap
            u_scr[base:base + seq_rows, cols] = u[s * seq_rows:(s + 1) * seq_rows]
            conv = cb_ref[:, cols]
            for j in range(CONV_W):
                off = base - tail + j
                conv = conv + cw[j:j + 1, :] * u_scr[off:off + seq_rows, cols]
            outs.append(conv)
        uc = jnp.concatenate(outs, axis=0) if n_seq > 1 else outs[0]
        gelu = 0.5 * uc * (1.0 + jnp.tanh(math.sqrt(2.0 / math.pi) * (uc + 0.044715 * (uc * uc * uc))))
        return (gelu * vv).astype(BF16)

    nxt = up(0)
    for c in range(n_chunks):
        cur = nxt
        if c + 1 < n_chunks:
            nxt = up(c + 1)
        gv_scr[:, c * FFN_CHUNK:(c + 1) * FFN_CHUNK] = gated(c, *cur)
    y_ref[...] = x + _dot(gv_scr[...], wd_ref[...])

    for s in range(n_seq):
        base = s * stride + gap
        cso_ref[s] = u_scr[base + seq_rows - tail:base + seq_rows, :]
    carry_scr[...] = u_scr[seq_rows:seq_rows + gap, :]


def _ffn(x2, g_ffn, w_up, conv_w, conv_b, w_down, conv_state, n_seq, seq_rows, tiles_per_seq):
    t = x2.shape[0]
    d = D_MODEL
    tm = n_seq * seq_rows
    nb = conv_state.shape[0]
    tail = CONV_W - 1
    kern = functools.partial(_ffn_kernel, n_seq=n_seq, seq_rows=seq_rows, tiles_per_seq=tiles_per_seq)
    y, tails = pl.pallas_call(
        kern,
        grid=(t // tm,),
        in_specs=[
            pl.BlockSpec((tm, d), lambda i: (i, 0)),
            _resident((1, d)),
            _resident(w_up.shape),
            _resident(conv_w.shape),
            _resident(conv_b.shape),
            _resident(w_down.shape),
            pl.BlockSpec((n_seq, tail, D_FF), lambda i: (i // tiles_per_seq, 0, 0)),
        ],
        out_specs=[
            pl.BlockSpec((tm, d), lambda i: (i, 0)),
            pl.BlockSpec((n_seq, tail, D_FF), lambda i: (i, 0, 0)),
        ],
        out_shape=[
            jax.ShapeDtypeStruct((t, d), F32),
            jax.ShapeDtypeStruct((nb * tiles_per_seq, tail, D_FF), F32),
        ],
        scratch_shapes=[
            pltpu.VMEM((n_seq * (seq_rows + SUBLANES), D_FF), F32),
            pltpu.VMEM((SUBLANES, D_FF), F32),
            pltpu.VMEM((tm, D_FF), BF16),
        ],
        compiler_params=pltpu.CompilerParams(
            dimension_semantics=("arbitrary",), vmem_limit_bytes=VMEM_LIMIT_BYTES),
        name="conv_ffn",
    )(x2, g_ffn, w_up, conv_w, conv_b, w_down, conv_state)
    return y, tails.reshape(nb, tiles_per_seq, tail, D_FF)[:, -1]


def _mem_cache_rows(c):
    b, n, h, hd = c.shape
    return c.reshape(b, n, h, hd // LANES, LANES).transpose(0, 1, 3, 2, 4).reshape(b, -1, LANES)


def _tile_gain(g, reps):
    return jnp.tile(g.astype(F32), reps).reshape(1, -1)


def _layer_weights(l, g_attn, w_in, w_gk2, b_gk, qn_diff, kn_diff, lam_q1, lam_k1, lam_q2, lam_k2,
                   subln_diff, subln_gla, g_mem, w_mem_kv, qn_mem, kn_mem, w_proj_diff, w_proj_gla,
                   w_proj_mem, w_out, g_ffn, w_up, conv_w, conv_b, w_down):
    d = D_MODEL
    w = w_in[l]
    lr0 = 6 * d
    wa = w[:, :lr0].astype(BF16)
    wb = w[:, lr0 + GK_RANK:].astype(BF16)
    wg1 = jnp.pad(w[:, lr0:lr0 + GK_RANK], ((0, 0), (0, LANES - GK_RANK))).astype(BF16)
    wg2 = jnp.pad(w_gk2[l], ((0, LANES - GK_RANK), (0, 0))).astype(BF16)
    return dict(
        g_attn=g_attn[l].reshape(1, d), wa=wa, wb=wb, wg1=wg1, wg2=wg2, bgk=b_gk[l].reshape(1, -1),
        qn_t=_tile_gain(qn_diff[l], d // DIFF_HD), kn_t=_tile_gain(kn_diff[l], d // DIFF_HD),
        qmn_t=_tile_gain(qn_mem[l], d // MEM_HD), knm_t=_tile_gain(kn_mem[l], d // MEM_HD),
        g64=_group_matrix(MXU_DIM, DIFF_HD), g256=_group_matrix(MEM_HD, MEM_HD),
        lamp=jnp.stack([lam_q1[l], lam_k1[l], lam_q2[l], lam_k2[l]]).astype(F32),
        sub_diff=subln_diff[l].reshape(1, -1), sub_gla=subln_gla[l].reshape(1, -1),
        g_mem=g_mem[l].reshape(1, d), w_mem_kv=w_mem_kv[l].astype(BF16),
        wd=w_proj_diff[l].astype(BF16), wg=w_proj_gla[l].astype(BF16), wm=w_proj_mem[l].astype(BF16),
        wo=w_out[l].astype(BF16), g_ffn=g_ffn[l].reshape(1, d), w_up=w_up[l].astype(BF16),
        conv_w=conv_w[l], conv_b=conv_b[l].reshape(1, -1), w_down=w_down[l].astype(BF16),
    )


def _group(x, wts, lam_init, mem_k, mem_v, past_k, past_v, gla_state, conv_state, prompt):
    b, s, d = x.shape
    t = b * s
    x2 = x.reshape(t, d)
    tm = _pick(t, 512)
    p2, ka, va, gk = _inproj(x2, wts["g_attn"], wts["wa"], wts["wb"], wts["wg1"], wts["wg2"], wts["bgk"],
                             wts["qn_t"], wts["kn_t"], wts["qmn_t"], wts["g64"], wts["g256"],
                             _pick(s, 256) if prompt else tm, s, prompt)
    if prompt:
        ka = ka.reshape(b, DIFF_HEADS, 2, DIFF_HD, s).transpose(0, 4, 1, 2, 3)
    else:
        ka = ka.reshape(b, s, DIFF_HEADS, 2, DIFF_HD)
    p3 = p2.reshape(b, s, N_PTILES * d)
    gk3 = gk.reshape(b, s, GLA_HEADS * GLA_DK)
    if prompt:
        oa = _diff_prompt(p3, wts["lamp"], wts["sub_diff"], lam_init, _pick(s, 512), 256)
        ob, gla_new = _gla(p3, gk3, wts["sub_gla"], None, _pick(s, 512))
    else:
        oa = _diff_sample(p3, past_k, past_v, wts["lamp"], wts["sub_diff"], lam_init)
        ob, gla_new = _gla(p3, gk3, wts["sub_gla"], gla_state, s)
    om = _mem_attn(p3, mem_k, mem_v, _pick(s, 512))
    x1 = _mix(x2, oa.reshape(t, d), ob.reshape(t, d), om.reshape(t, d), p2,
              wts["wd"], wts["wg"], wts["wm"], wts["wo"], tm)
    if prompt:
        rows = _pick(s, 512)
        y, cs = _ffn(x1, wts["g_ffn"], wts["w_up"], wts["conv_w"], wts["conv_b"], wts["w_down"],
                     conv_state, 1, rows, s // rows)
    else:
        y, cs = _ffn(x1, wts["g_ffn"], wts["w_up"], wts["conv_w"], wts["conv_b"], wts["w_down"],
                     conv_state, b, s, 1)
    return y.reshape(b, s, d), ka, va, gla_new, cs


def kernel(x_prompt, x_sample, mem_prompt, cache_diff_k, cache_diff_v, cache_mem_k, cache_mem_v, state_gla, state_conv, g_attn, w_in, w_gk2, b_gk, qn_diff, kn_diff, lam_q1, lam_k1, lam_q2, lam_k2, subln_diff, subln_gla, g_mem, w_mem_kv, qn_mem, kn_mem, w_proj_diff, w_proj_gla, w_proj_mem, w_out, g_ffn, w_up, conv_w, conv_b, w_down):
    depth = g_attn.shape[0]
    d = D_MODEL
    xp, xs = x_prompt, x_sample
    bp, sp, _ = xp.shape
    bs, ss, _ = xs.shape
    n_mem = mem_prompt.shape[1]
    outs = [[] for _ in range(10)]
    for l in range(depth):
        lam_init = 0.8 - 0.6 * math.exp(-0.3 * l)
        wts = _layer_weights(l, g_attn, w_in, w_gk2, b_gk, qn_diff, kn_diff, lam_q1, lam_k1, lam_q2,
                             lam_k2, subln_diff, subln_gla, g_mem, w_mem_kv, qn_mem, kn_mem,
                             w_proj_diff, w_proj_gla, w_proj_mem, w_out, g_ffn, w_up, conv_w, conv_b,
                             w_down)
        mk, mv = _memkv(mem_prompt.reshape(bp * n_mem, d), wts["g_mem"], wts["w_mem_kv"], wts["knm_t"],
                        wts["g256"])
        mk = mk.reshape(bp, n_mem, d)
        mv = mv.reshape(bp, n_mem, d)
        xp, kp, vp, gp, cp = _group(xp, wts, lam_init, mk, mv, None, None, None,
                                    jnp.zeros((bp, CONV_W - 1, D_FF), F32), True)
        xs, ks_, vs_, gs, cs = _group(
            xs, wts, lam_init, _mem_cache_rows(cache_mem_k[l]), _mem_cache_rows(cache_mem_v[l]),
            cache_diff_k[l].transpose(0, 2, 3, 4, 1).reshape(bs, d, -1), cache_diff_v[l].reshape(bs, -1, 2 * DIFF_HD),
            state_gla[l], state_conv[l], False)
        vals = (kp, vp.reshape(bp, sp, DIFF_HEADS, 2 * DIFF_HD),
                mk.reshape(bp, n_mem, MEM_HEADS, MEM_HD), mv.reshape(bp, n_mem, MEM_HEADS, MEM_HD), gp, cp,
                ks_, vs_.reshape(bs, ss, DIFF_HEADS, 2 * DIFF_HD), gs, cs)
        for o, v in zip(outs, vals):
            o.append(v)
    return (xp, xs) + tuple(jnp.stack(o) for o in outs)
```

```python
import functools
import math

import jax
import jax.numpy as jnp
from jax import lax
from jax.experimental import pallas as pl
from jax.experimental.pallas import tpu as pltpu

F32 = jnp.float32
BF16 = jnp.bfloat16

D_MODEL = 1024
CHUNK = 64
EPS = 1e-6
DIFF_HEADS = 8
DIFF_HD = 64
DIFF_SCALE = DIFF_HD ** -0.5
LOG2E = math.log2(math.e)
GLA_HEADS = 4
GLA_DK = 128
GLA_DV = 256
GLA_SCALE = GLA_DK ** -0.5
GK_RANK = 16
GK_NORM = 16.0
MEM_HEADS = 4
MEM_HD = 256
MEM_SCALE = MEM_HD ** -0.5
D_FF = 2816
CONV_W = 3

LANES = 128
SUBLANES = 8
MXU_DIM = 256
VMEM_LIMIT_BYTES = 48 * 1024 * 1024

T_QA, T_KA, T_VA, T_QKB, T_VB, T_RB, T_QM, T_GATE = 0, 1, 2, 3, 4, 5, 6, 7
N_PTILES = 10
SUB_BLOCK = 8
NEG_BIG = -1e30


def _dot(a, b):
    return jnp.dot(a, b, preferred_element_type=F32)


def _dot_nt(a, b):
    return lax.dot_general(a, b, (((1,), (1,)), ((), ())), preferred_element_type=F32)


def _sigmoid(x):
    return 1.0 / (1.0 + jnp.exp(-x))


def _pick(n, pref):
    t = min(n, pref)
    while n % t:
        t -= 1
    return t


def _rms_rows(x, gain):
    ms = jnp.mean(x * x, axis=-1, keepdims=True)
    return x * lax.rsqrt(ms + EPS) * gain


def _group_rms(y, gmat, gain):
    slab = gmat.shape[0]
    outs = []
    for c in range(y.shape[-1] // slab):
        ys = y[:, c * slab:(c + 1) * slab]
        ms = _dot((ys * ys).astype(BF16), gmat)
        outs.append(ys * lax.rsqrt(ms + EPS))
    return jnp.concatenate(outs, axis=-1) * gain


def _group_matrix(slab, group):
    r = jnp.arange(slab) // group
    return jnp.where(r[:, None] == r[None, :], 1.0 / group, 0.0).astype(BF16)


def _inproj_kernel(x_ref, g_ref, wa_ref, wb_ref, wg1_ref, wg2_ref, bgk_ref, qn_ref, kn_ref, qmn_ref,
                   g64_ref, g256_ref, p_ref, ka_ref, vaf_ref, gk_ref, *, k_transposed):
    d = D_MODEL
    n_a = wa_ref.shape[1] // d
    h = _rms_rows(x_ref[...], g_ref[...]).astype(BF16)

    def project(j):
        if j < n_a:
            return _dot(h, wa_ref[:, j * d:(j + 1) * d])
        return _dot(h, wb_ref[:, (j - n_a) * d:(j - n_a + 1) * d])

    def finish(j, acc):
        if j == T_QA:
            out = _group_rms(acc, g64_ref[...], qn_ref[...]) * (DIFF_SCALE * LOG2E)
        elif j == T_KA:
            out = _group_rms(acc, g64_ref[...], kn_ref[...])
            ka_ref[...] = out.T if k_transposed else out
        elif j == T_VA:
            vaf_ref[...] = acc
            out = acc
        elif j == T_QKB:
            half = d // 2
            out = jnp.concatenate([acc[:, :half] * GLA_SCALE, acc[:, half:]], axis=-1)
            lr = _dot(h, wg1_ref[...])
            z = _dot(lr.astype(BF16), wg2_ref[...]) + bgk_ref[...]
            log_sig = jnp.minimum(z, 0.0) - jnp.log(1.0 + jnp.exp(-jnp.abs(z)))
            gk_ref[...] = log_sig * (1.0 / GK_NORM)
        elif j == T_VB:
            out = acc
        elif j == T_RB:
            out = acc * _sigmoid(acc)
        elif j == T_QM:
            out = _group_rms(acc, g256_ref[...], qmn_ref[...]) * MEM_SCALE
        else:
            out = _sigmoid(acc)
        p_ref[:, j * d:(j + 1) * d] = out.astype(BF16)

    acc_next = project(0)
    for j in range(N_PTILES):
        acc = acc_next
        if j + 1 < N_PTILES:
            acc_next = project(j + 1)
        finish(j, acc)


def _resident(shape):
    return pl.BlockSpec(shape, lambda *_: (0,) * len(shape), pipeline_mode=pl.Buffered(1))


def _inproj(x2, g_attn, wa, wb, wg1, wg2, bgk, qn_t, kn_t, qmn_t, g64, g256, tm, seq_len, k_transposed):
    t = x2.shape[0]
    d = D_MODEL
    row = lambda i: (i, 0)
    if k_transposed:
        per_seq = seq_len // tm
        ka_spec = pl.BlockSpec((None, d, tm), lambda i: (i // per_seq, 0, i % per_seq))
        ka_shape = jax.ShapeDtypeStruct((t // seq_len, d, seq_len), F32)
    else:
        ka_spec = pl.BlockSpec((tm, d), row)
        ka_shape = jax.ShapeDtypeStruct((t, d), F32)
    small = [wg1, wg2, bgk, qn_t, kn_t, qmn_t, g64, g256]
    return pl.pallas_call(
        functools.partial(_inproj_kernel, k_transposed=k_transposed),
        grid=(t // tm,),
        in_specs=[pl.BlockSpec((tm, d), row), _resident((1, d)), _resident(wa.shape), _resident(wb.shape)]
        + [_resident(a.shape) for a in small],
        out_specs=[
            pl.BlockSpec((tm, N_PTILES * d), row),
            ka_spec,
            pl.BlockSpec((tm, d), row),
            pl.BlockSpec((tm, GLA_HEADS * GLA_DK), row),
        ],
        out_shape=[
            jax.ShapeDtypeStruct((t, N_PTILES * d), BF16),
            ka_shape,
            jax.ShapeDtypeStruct((t, d), F32),
            jax.ShapeDtypeStruct((t, GLA_HEADS * GLA_DK), F32),
        ],
        compiler_params=pltpu.CompilerParams(
            dimension_semantics=("parallel",), vmem_limit_bytes=VMEM_LIMIT_BYTES),
        name="inproj",
    )(x2, g_attn, wa, wb, *small)


def _memkv_kernel(m_ref, g_ref, w_ref, kn_ref, g256_ref, k_ref, v_ref):
    j = pl.program_id(0)
    h = _rms_rows(m_ref[...], g_ref[...]).astype(BF16)
    acc = _dot(h, w_ref[...])

    @pl.when(j == 0)
    def _():
        k_ref[...] = _group_rms(acc, g256_ref[...], kn_ref[...])

    @pl.when(j == 1)
    def _():
        v_ref[...] = acc


def _memkv(mem2, g_mem, w_kv, knm_t, g256):
    t = mem2.shape[0]
    d = D_MODEL
    full = lambda j: (0, 0)
    return pl.pallas_call(
        _memkv_kernel,
        grid=(2,),
        in_specs=[
            pl.BlockSpec((t, d), full),
            pl.BlockSpec((1, d), full),
            pl.BlockSpec((d, d), lambda j: (0, j)),
            pl.BlockSpec((1, d), full),
            pl.BlockSpec(g256.shape, full),
        ],
        out_specs=[pl.BlockSpec((t, d), full), pl.BlockSpec((t, d), full)],
        out_shape=[jax.ShapeDtypeStruct((t, d), F32), jax.ShapeDtypeStruct((t, d), F32)],
        compiler_params=pltpu.CompilerParams(
            dimension_semantics=("arbitrary",), vmem_limit_bytes=VMEM_LIMIT_BYTES),
        name="memkv",
    )(mem2, g_mem, w_kv, knm_t, g256)


def _lambda_value(lamp, lam_init):
    a = jnp.sum(lamp[0:1, :] * lamp[1:2, :], axis=-1, keepdims=True)
    b = jnp.sum(lamp[2:3, :] * lamp[3:4, :], axis=-1, keepdims=True)
    return jnp.exp(a) - jnp.exp(b) + lam_init


def _stack_maps(q):
    lane = lax.broadcasted_iota(jnp.int32, q.shape, 1)
    zero = jnp.zeros_like(q)
    return jnp.concatenate([jnp.where(lane < DIFF_HD, q, zero), jnp.where(lane >= DIFF_HD, q, zero)], axis=0)


def _diff_finish(acc, l, lam, sub, lam_init, tq):
    o2 = acc / l
    o = o2[:tq] - lam * o2[tq:]
    return _rms_rows(o, sub) * (1.0 - lam_init)


ONES_ROWS = 16
LOOP_TRIPS = 4


def _diff_prompt_kernel(q_ref, qn_ref, k_ref, v_ref, lamp_ref, sub_ref, o_ref, qt_scr, vt_scr, m_scr, acc_scr,
                        st0_scr, st1_scr, p0_scr, p1_scr, alpha0_scr, alpha1_scr, mx0_scr, mx1_scr,
                        qtn_scr, stn_scr, mxn_scr, *, tq, tk, lam_init):
    qi = pl.program_id(2)
    n_kv = v_ref.shape[0] // tk
    hd2 = v_ref.shape[1]
    halves = tq // tk
    assert halves == 2
    all_cgs = list(range(2 * halves))
    upper_cgs = [g * halves + 1 for g in range(2)]

    @pl.when(qi == 0)
    def _():
        for c in range(n_kv):
            vt_scr[c, 0:hd2, :] = v_ref[c * tk:(c + 1) * tk, :].astype(F32).T.astype(BF16)
            vt_scr[c, hd2:hd2 + ONES_ROWS, :] = jnp.ones((ONES_ROWS, tk), BF16)

    def transposed_queries(ref):
        return _stack_maps(ref[...]).astype(F32).T.astype(BF16)

    @pl.when(qi == 0)
    def _():
        qt_scr[...] = transposed_queries(q_ref)

    @pl.when(qi > 0)
    def _():
        qt_scr[...] = qtn_scr[...]

    m_scr[...] = jnp.full(m_scr.shape, -jnp.inf, F32)
    acc_scr[...] = jnp.zeros(acc_scr.shape, F32)

    def scores(kb, cgs, qt=qt_scr):
        k = k_ref[pl.ds(pl.multiple_of(kb * tk, tk), tk), :]
        sts = [_dot(k, qt[:, cg * tk:(cg + 1) * tk]) for cg in cgs]
        return sts, [jnp.max(st, axis=0, keepdims=True) for st in sts]

    def softmax(sts, maxes, slot, cgs, masked):
        pts, alphas = [], []
        for st, mx, cg, msk in zip(sts, maxes, cgs, masked):
            cols = slice(cg * tk, (cg + 1) * tk)
            if msk:
                r = lax.broadcasted_iota(jnp.int32, st.shape, 0)
                c = lax.broadcasted_iota(jnp.int32, st.shape, 1)
                st = jnp.where((r // CHUNK) <= (c // CHUNK), st, -jnp.inf)
                mx = jnp.max(st, axis=0, keepdims=True)
            m_prev = m_scr[slot, :, cols]
            m_new = jnp.maximum(m_prev, mx)
            alphas.append(jnp.exp2(m_prev - m_new))
            pts.append(jnp.exp2(st - m_new).astype(BF16))
            m_scr[slot, :, cols] = m_new
        return pts, alphas

    def accumulate(kb, slot, cgs, pts, alphas):
        vt = vt_scr[kb]
        for pt, alpha, cg in zip(pts, alphas, cgs):
            cols = slice(cg * tk, (cg + 1) * tk)
            acc_scr[slot, :, cols] = alpha * acc_scr[slot, :, cols] + _dot(vt, pt)

    st_scr, p_scr, alpha_scr = (st0_scr, st1_scr), (p0_scr, p1_scr), (alpha0_scr, alpha1_scr)
    mx_scr = (mx0_scr, mx1_scr)

    def put(scr, buf, cgs, vals):
        for cg, val in zip(cgs, vals):
            scr[buf][cg] = val

    def get(scr, buf, cgs):
        return [scr[buf][cg] for cg in cgs]

    def put_scores(buf, cgs, sts_maxes):
        put(st_scr, buf, cgs, sts_maxes[0])
        put(mx_scr, buf, cgs, sts_maxes[1])

    def trip(t, par):
        pts, alphas = softmax(get(st_scr, par, all_cgs), get(mx_scr, par, all_cgs), par, all_cgs,
                              [False] * len(all_cgs))
        put(p_scr, par, all_cgs, pts)
        put(alpha_scr, par, all_cgs, alphas)
        put_scores(1 - par, all_cgs, scores(jnp.minimum(t + 1, n_kv - 1), all_cgs))
        accumulate(jnp.maximum(t - 1, 0), 1 - par, all_cgs,
                   get(p_scr, 1 - par, all_cgs), get(alpha_scr, 1 - par, all_cgs))

    def loop_body(i, carry):
        for u in range(LOOP_TRIPS):
            trip(LOOP_TRIPS * i + u, u % 2)
        return carry

    n_trips = qi * halves

    @pl.when(qi == 0)
    def _():
        put_scores(0, all_cgs, scores(0, all_cgs))

    @pl.when(qi > 0)
    def _():
        put_scores(0, all_cgs, ([stn_scr[cg] for cg in all_cgs], [mxn_scr[cg] for cg in all_cgs]))

    put(p_scr, 1, all_cgs, [jnp.zeros((tk, tk), BF16)] * len(all_cgs))
    put(alpha_scr, 1, all_cgs, [jnp.ones((1, tk), F32)] * len(all_cgs))
    assert LOOP_TRIPS == 4
    lax.fori_loop(0, n_trips // LOOP_TRIPS, loop_body, 0)

    @pl.when(n_trips % LOOP_TRIPS == 2)
    def _():
        trip(n_trips - 2, 0)
        trip(n_trips - 1, 1)

    kb0 = n_trips
    upper = scores(kb0 + 1, upper_cgs)
    qtn_scr[...] = transposed_queries(qn_ref)
    nxt = scores(0, all_cgs, qt=qtn_scr)
    for cg in all_cgs:
        stn_scr[cg] = nxt[0][cg]
        mxn_scr[cg] = nxt[1][cg]
    accumulate(jnp.maximum(kb0 - 1, 0), 1, all_cgs, get(p_scr, 1, all_cgs), get(alpha_scr, 1, all_cgs))
    pts, alphas = softmax(get(st_scr, 0, all_cgs), get(mx_scr, 0, all_cgs), 0, all_cgs,
                          [cg % halves == 0 for cg in all_cgs])
    accumulate(kb0, 0, all_cgs, pts, alphas)
    pts, alphas = softmax(upper[0], upper[1], 1, upper_cgs, [True] * len(upper_cgs))
    accumulate(kb0 + 1, 1, upper_cgs, pts, alphas)

    m_all = jnp.maximum(m_scr[0], m_scr[1])
    acc_all = jnp.exp2(m_scr[0] - m_all) * acc_scr[0] + jnp.exp2(m_scr[1] - m_all) * acc_scr[1]

    lam = _lambda_value(lamp_ref[...], lam_init)
    o2t = acc_all[0:hd2, :] / acc_all[hd2:hd2 + 1, :]
    o = (o2t[:, :tq] - lam * o2t[:, tq:]).T
    o_ref[...] = (_rms_rows(o, sub_ref[...]) * (1.0 - lam_init)).astype(o_ref.dtype)


def _diff_prompt(p3, lamp, sub, lam_init, tq, tk):
    b, s, _ = p3.shape
    hd2 = 2 * DIFF_HD
    kern = functools.partial(_diff_prompt_kernel, tq=tq, tk=tk, lam_init=lam_init)
    per_d = D_MODEL // hd2
    n_cg = 2 * tq // tk
    return pl.pallas_call(
        kern,
        grid=(b, DIFF_HEADS, s // tq),
        in_specs=[
            pl.BlockSpec((None, tq, hd2), lambda bi, h, qi: (bi, qi, T_QA * per_d + h)),
            pl.BlockSpec((None, tq, hd2), lambda bi, h, qi: (bi, jnp.minimum(qi + 1, s // tq - 1), T_QA * per_d + h)),
            pl.BlockSpec((None, s, hd2), lambda bi, h, qi: (bi, 0, T_KA * per_d + h)),
            pl.BlockSpec((None, s, hd2), lambda bi, h, qi: (bi, 0, T_VA * per_d + h)),
            pl.BlockSpec(lamp.shape, lambda bi, h, qi: (0, 0)),
            pl.BlockSpec(sub.shape, lambda bi, h, qi: (0, 0)),
        ],
        out_specs=pl.BlockSpec((None, tq, hd2), lambda bi, h, qi: (bi, qi, h)),
        out_shape=jax.ShapeDtypeStruct((b, s, D_MODEL), BF16),
        scratch_shapes=[
            pltpu.VMEM((hd2, 2 * tq), BF16),
            pltpu.VMEM((s // tk, hd2 + ONES_ROWS, tk), BF16),
            pltpu.VMEM((2, 1, 2 * tq), F32),
            pltpu.VMEM((2, hd2 + ONES_ROWS, 2 * tq), F32),
            pltpu.VMEM((n_cg, tk, tk), F32),
            pltpu.VMEM((n_cg, tk, tk), F32),
            pltpu.VMEM((n_cg, tk, tk), BF16),
            pltpu.VMEM((n_cg, tk, tk), BF16),
            pltpu.VMEM((n_cg, 1, tk), F32),
            pltpu.VMEM((n_cg, 1, tk), F32),
            pltpu.VMEM((n_cg, 1, tk), F32),
            pltpu.VMEM((n_cg, 1, tk), F32),
            pltpu.VMEM((hd2, 2 * tq), BF16),
            pltpu.VMEM((n_cg, tk, tk), F32),
            pltpu.VMEM((n_cg, 1, tk), F32),
        ],
        compiler_params=pltpu.CompilerParams(
            dimension_semantics=("parallel", "parallel", "arbitrary"), vmem_limit_bytes=VMEM_LIMIT_BYTES),
        name="diff_attn_prompt",
    )(p3, p3, p3, p3, lamp, sub)


SAMPLE_HEAD_GROUP = 4


def _diff_sample_kernel(q_ref, kn_ref, vn_ref, kc_ref, vc_ref, lamp_ref, sub_ref, o_ref, *, lam_init):
    tq = q_ref.shape[0]
    hd2 = 2 * DIFF_HD
    n_past = kc_ref.shape[1]
    lam = _lambda_value(lamp_ref[...], lam_init)
    for h0 in range(0, DIFF_HEADS, SAMPLE_HEAD_GROUP):
        heads = range(h0, h0 + SAMPLE_HEAD_GROUP)
        cols = [slice(h * hd2, (h + 1) * hd2) for h in heads]
        q2s = [_stack_maps(q_ref[:, c]) for c in cols]
        s_cs = [_dot(q2, kc_ref[c, :].astype(BF16)) for q2, c in zip(q2s, cols)]
        s_ns = [_dot_nt(q2, kn_ref[:, c]) for q2, c in zip(q2s, cols)]
        pcs, pns, ls = [], [], []
        for s_c, s_n in zip(s_cs, s_ns):
            m = jnp.maximum(jnp.max(s_c, axis=-1, keepdims=True), jnp.max(s_n, axis=-1, keepdims=True))
            p_c = jnp.exp2(s_c - m)
            p_n = jnp.exp2(s_n - m)
            ls.append(jnp.sum(p_c, axis=-1, keepdims=True) + jnp.sum(p_n, axis=-1, keepdims=True))
            pcs.append(p_c.astype(BF16))
            pns.append(p_n.astype(BF16))
        for h, c, p_c, p_n, l in zip(heads, cols, pcs, pns, ls):
            vc = vc_ref[pl.ds(h, n_past, stride=DIFF_HEADS), :]
            acc = _dot(p_c, vc.astype(BF16)) + _dot(p_n, vn_ref[:, c])
            o_ref[:, c] = _diff_finish(acc, l, lam, sub_ref[...], lam_init, tq).astype(o_ref.dtype)


def _diff_sample(p3, kc_t, vc, lamp, sub, lam_init):
    b, l, _ = p3.shape
    n_past = kc_t.shape[2]
    hd2 = 2 * DIFF_HD
    d = D_MODEL
    kern = functools.partial(_diff_sample_kernel, lam_init=lam_init)
    return pl.pallas_call(
        kern,
        grid=(b,),
        in_specs=[
            pl.BlockSpec((None, l, d), lambda bi: (bi, 0, T_QA)),
            pl.BlockSpec((None, l, d), lambda bi: (bi, 0, T_KA)),
            pl.BlockSpec((None, l, d), lambda bi: (bi, 0, T_VA)),
            pl.BlockSpec((None, d, n_past), lambda bi: (bi, 0, 0)),
            pl.BlockSpec((None, n_past * DIFF_HEADS, hd2), lambda bi: (bi, 0, 0)),
            pl.BlockSpec(lamp.shape, lambda bi: (0, 0)),
            pl.BlockSpec(sub.shape, lambda bi: (0, 0)),
        ],
        out_specs=pl.BlockSpec((None, l, d), lambda bi: (bi, 0, 0)),
        out_shape=jax.ShapeDtypeStruct((b, l, D_MODEL), BF16),
        compiler_params=pltpu.CompilerParams(
            dimension_semantics=("parallel",), vmem_limit_bytes=VMEM_LIMIT_BYTES),
        name="diff_attn_sample",
    )(p3, p3, p3, kc_t, vc, lamp, sub)


def _bcast_rows(x, period, row):
    r, c = x.shape
    x3 = x.reshape(r // period, period, c)
    return jnp.broadcast_to(x3[:, row:row + 1, :], x3.shape).reshape(r, c)


def _gla_kernel(*refs, rows, has_init):
    if has_init:
        q_ref, k_ref, v_ref, r_ref, g_ref, sub_ref, s0_ref, o_ref, sout_ref, st_scr, kf_scr, b_scr = refs
    else:
        q_ref, k_ref, v_ref, r_ref, g_ref, sub_ref, o_ref, sout_ref, st_scr, kf_scr, b_scr = refs
        s0_ref = None
    step = pl.program_id(2)
    n_chunks = rows // CHUNK
    n_sub = CHUNK // SUB_BLOCK

    @pl.when(step == 0)
    def _():
        if has_init:
            st_scr[...] = s0_ref[...]
        else:
            st_scr[...] = jnp.zeros(st_scr.shape, F32)

    q = q_ref[...].astype(F32)
    k = k_ref[...].astype(F32)
    g = g_ref[...]
    v = v_ref[...]
    chunk_rows = [slice(c * CHUNK, (c + 1) * CHUNK) for c in range(n_chunks)]

    ri = lax.broadcasted_iota(jnp.int32, (CHUNK, CHUNK), 0)
    ci = lax.broadcasted_iota(jnp.int32, (CHUNK, CHUNK), 1)
    tri = jnp.where(ci <= ri, 1.0, 0.0).astype(BF16)
    gw = jnp.concatenate([g[sl] for sl in chunk_rows], axis=1)
    g1 = gw.astype(BF16)
    rem = gw - g1.astype(F32)
    g2 = rem.astype(BF16)
    g3 = (rem - g2.astype(F32)).astype(BF16)
    bw = _dot(tri, g1) + _dot(tri, g2) + _dot(tri, g3)
    b = jnp.concatenate([bw[:, c * GLA_DK:(c + 1) * GLA_DK] for c in range(n_chunks)], axis=0)
    bex = b - g

    b_last = _bcast_rows(b, CHUNK, CHUNK - 1)
    b_blk = _bcast_rows(bex, SUB_BLOCK, 0)
    q_blk = q * jnp.exp(b - b_blk)
    q_chk = q * jnp.exp(b)
    k_end = k * jnp.exp(b_last - b)

    rowc = lax.broadcasted_iota(jnp.int32, (rows, GLA_DK), 0) % CHUNK
    lane = lax.broadcasted_iota(jnp.int32, (rows, GLA_DK), 1)
    zero = jnp.zeros_like(q)

    lhs_parts, rhs_parts = [], []
    for blk in range(1, n_sub):
        b_ref_blk = _bcast_rows(bex, CHUNK, blk * SUB_BLOCK)
        k_blk = k * jnp.exp(jnp.where(rowc < blk * SUB_BLOCK, b_ref_blk - b, NEG_BIG))
        lhs_parts.append(jnp.where((rowc // SUB_BLOCK) == blk, q_blk, zero))
        rhs_parts.append(k_blk)
    lhs = jnp.concatenate(lhs_parts, axis=-1).astype(BF16)
    rhs = jnp.concatenate(rhs_parts, axis=-1).astype(BF16)

    pad = SUB_BLOCK
    kf_scr[0:pad, :] = jnp.zeros((pad, GLA_DK), F32)
    b_scr[0:pad, :] = jnp.zeros((pad, GLA_DK), F32)
    kf_scr[pad:pad + rows, :] = k
    b_scr[pad:pad + rows, :] = b
    ones = jnp.ones((GLA_DK, LANES), BF16)
    delta = rowc - lane
    row_sub = rowc % SUB_BLOCK
    a_diag = jnp.zeros((rows, LANES), F32)
    for d in range(SUB_BLOCK):
        kd = kf_scr[pad - d:pad - d + rows, :]
        bd = b_scr[pad - d:pad - d + rows, :]
        e = jnp.exp(jnp.where(row_sub >= d, b - bd, NEG_BIG))
        rd = _dot((q * kd * e).astype(BF16), ones)
        a_diag = a_diag + jnp.where(delta == d, rd, 0.0)

    atts = [(_dot_nt(lhs[sl], rhs[sl]) + a_diag[sl, :CHUNK]).astype(BF16) for sl in chunk_rows]
    d_sts = [_dot(k_end[sl].T.astype(BF16), v[sl]) for sl in chunk_rows]
    o_intra = [_dot(att, v[sl]) for att, sl in zip(atts, chunk_rows)]
    decs = [jnp.exp(b[sl].T[:, CHUNK - 1:CHUNK]) for sl in chunk_rows]
    st = st_scr[...]
    outs = []
    for c, sl in enumerate(chunk_rows):
        outs.append(o_intra[c] + _dot(q_chk[sl].astype(BF16), st.astype(BF16)))
        st = st * decs[c] + d_sts[c]
    st_scr[...] = st
    o = jnp.concatenate(outs, axis=0) if n_chunks > 1 else outs[0]
    o_ref[...] = (_rms_rows(o, sub_ref[...]) * r_ref[...].astype(F32)).astype(o_ref.dtype)

    @pl.when(step == pl.num_programs(2) - 1)
    def _():
        sout_ref[...] = st_scr[...]


def _gla(p3, gk3, sub, s0, rows):
    b, s, _ = p3.shape
    has_init = s0 is not None
    kq = D_MODEL // GLA_DK
    kv = D_MODEL // GLA_DV
    in_specs = [
        pl.BlockSpec((None, rows, GLA_DK), lambda bi, h, r: (bi, r, T_QKB * kq + h)),
        pl.BlockSpec((None, rows, GLA_DK), lambda bi, h, r: (bi, r, T_QKB * kq + GLA_HEADS + h)),
        pl.BlockSpec((None, rows, GLA_DV), lambda bi, h, r: (bi, r, T_VB * kv + h)),
        pl.BlockSpec((None, rows, GLA_DV), lambda bi, h, r: (bi, r, T_RB * kv + h)),
        pl.BlockSpec((None, rows, GLA_DK), lambda bi, h, r: (bi, r, h)),
        pl.BlockSpec(sub.shape, lambda bi, h, r: (0, 0)),
    ]
    args = [p3, p3, p3, p3, gk3, sub]
    if has_init:
        in_specs.append(pl.BlockSpec((None, None, GLA_DK, GLA_DV), lambda bi, h, r: (bi, h, 0, 0)))
        args.append(s0)
    kern = functools.partial(_gla_kernel, rows=rows, has_init=has_init)
    return pl.pallas_call(
        kern,
        grid=(b, GLA_HEADS, s // rows),
        in_specs=in_specs,
        out_specs=[
            pl.BlockSpec((None, rows, GLA_DV), lambda bi, h, r: (bi, r, h)),
            pl.BlockSpec((None, None, GLA_DK, GLA_DV), lambda bi, h, r: (bi, h, 0, 0)),
        ],
        out_shape=[
            jax.ShapeDtypeStruct((b, s, GLA_HEADS * GLA_DV), BF16),
            jax.ShapeDtypeStruct((b, GLA_HEADS, GLA_DK, GLA_DV), F32),
        ],
        scratch_shapes=[
            pltpu.VMEM((GLA_DK, GLA_DV), F32),
            pltpu.VMEM((SUB_BLOCK + rows, GLA_DK), F32),
            pltpu.VMEM((SUB_BLOCK + rows, GLA_DK), F32),
        ],
        compiler_params=pltpu.CompilerParams(
            dimension_semantics=("parallel", "parallel", "arbitrary"), vmem_limit_bytes=VMEM_LIMIT_BYTES),
        name="gla",
    )(*args)


def _mem_attn_kernel(q_ref, k_ref, v_ref, o_ref, *, tiled):
    def head(ref, h):
        if not tiled:
            return ref[:, h * MEM_HD:(h + 1) * MEM_HD]
        period = MEM_HEADS * MEM_HD // LANES
        n_tok = ref.shape[0] // period
        return jnp.concatenate([ref[pl.ds(half * MEM_HEADS + h, n_tok, stride=period), :]
                                for half in range(MEM_HD // LANES)], axis=1)

    outs = []
    for h in range(MEM_HEADS):
        sl = slice(h * MEM_HD, (h + 1) * MEM_HD)
        s = _dot_nt(q_ref[:, sl], head(k_ref, h).astype(BF16))
        m = jnp.max(s, axis=-1, keepdims=True)
        p = jnp.exp(s - m)
        l = jnp.sum(p, axis=-1, keepdims=True)
        outs.append(_dot(p.astype(BF16), head(v_ref, h).astype(BF16)) / l)
    o_ref[...] = jnp.concatenate(outs, axis=-1).astype(o_ref.dtype)


def _mem_attn(p3, mk, mv, tq):
    b, s, _ = p3.shape
    d = D_MODEL
    kv_block = (None,) + mk.shape[1:]
    return pl.pallas_call(
        functools.partial(_mem_attn_kernel, tiled=mk.shape[2] == LANES),
        grid=(b, s // tq),
        in_specs=[
            pl.BlockSpec((None, tq, d), lambda bi, i: (bi, i, T_QM)),
            pl.BlockSpec(kv_block, lambda bi, i: (bi, 0, 0)),
            pl.BlockSpec(kv_block, lambda bi, i: (bi, 0, 0)),
        ],
        out_specs=pl.BlockSpec((None, tq, d), lambda bi, i: (bi, i, 0)),
        out_shape=jax.ShapeDtypeStruct((b, s, d), BF16),
        compiler_params=pltpu.CompilerParams(
            dimension_semantics=("parallel", "parallel"), vmem_limit_bytes=VMEM_LIMIT_BYTES),
        name="mem_attn",
    )(p3, mk, mv)


def _mix_kernel(x_ref, oa_ref, ob_ref, om_ref, ga_ref, gb_ref, gm_ref, wd_ref, wg_ref, wm_ref, wo_ref,
                y_ref):
    m = (ga_ref[...].astype(F32) * _dot(oa_ref[...], wd_ref[...])
         + gb_ref[...].astype(F32) * _dot(ob_ref[...], wg_ref[...])
         + gm_ref[...].astype(F32) * _dot(om_ref[...], wm_ref[...]))
    y_ref[...] = x_ref[...] + _dot(m.astype(BF16), wo_ref[...])


def _mix(x2, oa, ob, om, p2, wd, wg, wm, wo, tm):
    t = x2.shape[0]
    d = D_MODEL
    row = lambda i: (i, 0)
    full = lambda i: (0, 0)
    wspec = pl.BlockSpec((d, d), full, pipeline_mode=pl.Buffered(1))
    return pl.pallas_call(
        _mix_kernel,
        grid=(t // tm,),
        in_specs=[
            pl.BlockSpec((tm, d), row), pl.BlockSpec((tm, d), row), pl.BlockSpec((tm, d), row),
            pl.BlockSpec((tm, d), row),
            pl.BlockSpec((tm, d), lambda i: (i, T_GATE)),
            pl.BlockSpec((tm, d), lambda i: (i, T_GATE + 1)),
            pl.BlockSpec((tm, d), lambda i: (i, T_GATE + 2)),
            wspec, wspec, wspec, wspec,
        ],
        out_specs=pl.BlockSpec((tm, d), row),
        out_shape=jax.ShapeDtypeStruct((t, d), F32),
        compiler_params=pltpu.CompilerParams(
            dimension_semantics=("parallel",), vmem_limit_bytes=VMEM_LIMIT_BYTES),
        name="mix_out",
    )(x2, oa, ob, om, p2, p2, p2, wd, wg, wm, wo)


FFN_CHUNK = 256


def _ffn_kernel(x_ref, g_ref, wup_ref, cw_ref, cb_ref, wd_ref, cs_ref, y_ref, cso_ref,
                u_scr, carry_scr, gv_scr, *, n_seq, seq_rows, tiles_per_seq):
    i = pl.program_id(0)
    gap = SUBLANES
    stride = seq_rows + gap
    tail = CONV_W - 1
    n_chunks = D_FF // FFN_CHUNK
    x = x_ref[...]
    h = _rms_rows(x, g_ref[...]).astype(BF16)

    first = (i % tiles_per_seq) == 0
    for s in range(n_seq):
        base = s * stride

        @pl.when(first)
        def _():
            u_scr[base:base + gap, :] = jnp.zeros((gap, D_FF), F32)
            u_scr[base + gap - tail:base + gap, :] = cs_ref[s]

        @pl.when(jnp.logical_not(first))
        def _():
            u_scr[base:base + gap, :] = carry_scr[...]

    def up(c):
        cols = slice(c * FFN_CHUNK, (c + 1) * FFN_CHUNK)
        gate_cols = slice(D_FF + c * FFN_CHUNK, D_FF + (c + 1) * FFN_CHUNK)
        return _dot(h, wup_ref[:, cols]), _dot(h, wup_ref[:, gate_cols])

    def gated(c, u, vv):
        cols = slice(c * FFN_CHUNK, (c + 1) * FFN_CHUNK)
        cw = cw_ref[:, cols]
        outs = []
        for s in range(n_seq):
            base = s * stride + gap
            u_scr[base:base + seq_rows, cols] = u[s * seq_rows:(s + 1) * seq_rows]
            conv = cb_ref[:, cols]
            for j in range(CONV_W):
                off = base - tail + j
                conv = conv + cw[j:j + 1, :] * u_scr[off:off + seq_rows, cols]
            outs.append(conv)
        uc = jnp.concatenate(outs, axis=0) if n_seq > 1 else outs[0]
        gelu = 0.5 * uc * (1.0 + jnp.tanh(math.sqrt(2.0 / math.pi) * (uc + 0.044715 * (uc * uc * uc))))
        return (gelu * vv).astype(BF16)

    nxt = up(0)
    for c in range(n_chunks):
        cur = nxt
        if c + 1 < n_chunks:
            nxt = up(c + 1)
        gv_scr[:, c * FFN_CHUNK:(c + 1) * FFN_CHUNK] = gated(c, *cur)
    y_ref[...] = x + _dot(gv_scr[...], wd_ref[...])

    for s in range(n_seq):
        base = s * stride + gap
        cso_ref[s] = u_scr[base + seq_rows - tail:base + seq_rows, :]
    carry_scr[...] = u_scr[seq_rows:seq_rows + gap, :]


def _ffn(x2, g_ffn, w_up, conv_w, conv_b, w_down, conv_state, n_seq, seq_rows, tiles_per_seq):
    t = x2.shape[0]
    d = D_MODEL
    tm = n_seq * seq_rows
    nb = conv_state.shape[0]
    tail = CONV_W - 1
    kern = functools.partial(_ffn_kernel, n_seq=n_seq, seq_rows=seq_rows, tiles_per_seq=tiles_per_seq)
    y, tails = pl.pallas_call(
        kern,
        grid=(t // tm,),
        in_specs=[
            pl.BlockSpec((tm, d), lambda i: (i, 0)),
            _resident((1, d)),
            _resident(w_up.shape),
            _resident(conv_w.shape),
            _resident(conv_b.shape),
            _resident(w_down.shape),
            pl.BlockSpec((n_seq, tail, D_FF), lambda i: (i // tiles_per_seq, 0, 0)),
        ],
        out_specs=[
            pl.BlockSpec((tm, d), lambda i: (i, 0)),
            pl.BlockSpec((n_seq, tail, D_FF), lambda i: (i, 0, 0)),
        ],
        out_shape=[
            jax.ShapeDtypeStruct((t, d), F32),
            jax.ShapeDtypeStruct((nb * tiles_per_seq, tail, D_FF), F32),
        ],
        scratch_shapes=[
            pltpu.VMEM((n_seq * (seq_rows + SUBLANES), D_FF), F32),
            pltpu.VMEM((SUBLANES, D_FF), F32),
            pltpu.VMEM((tm, D_FF), BF16),
        ],
        compiler_params=pltpu.CompilerParams(
            dimension_semantics=("arbitrary",), vmem_limit_bytes=VMEM_LIMIT_BYTES),
        name="conv_ffn",
    )(x2, g_ffn, w_up, conv_w, conv_b, w_down, conv_state)
    return y, tails.reshape(nb, tiles_per_seq, tail, D_FF)[:, -1]


def _mem_cache_rows(c):
    b, n, h, hd = c.shape
    return c.reshape(b, n, h, hd // LANES, LANES).transpose(0, 1, 3, 2, 4).reshape(b, -1, LANES)


def _tile_gain(g, reps):
    return jnp.tile(g.astype(F32), reps).reshape(1, -1)


def _layer_weights(l, g_attn, w_in, w_gk2, b_gk, qn_diff, kn_diff, lam_q1, lam_k1, lam_q2, lam_k2,
                   subln_diff, subln_gla, g_mem, w_mem_kv, qn_mem, kn_mem, w_proj_diff, w_proj_gla,
                   w_proj_mem, w_out, g_ffn, w_up, conv_w, conv_b, w_down):
    d = D_MODEL
    w = w_in[l]
    lr0 = 6 * d
    wa = w[:, :lr0].astype(BF16)
    wb = w[:, lr0 + GK_RANK:].astype(BF16)
    wg1 = jnp.pad(w[:, lr0:lr0 + GK_RANK], ((0, 0), (0, LANES - GK_RANK))).astype(BF16)
    wg2 = jnp.pad(w_gk2[l], ((0, LANES - GK_RANK), (0, 0))).astype(BF16)
    return dict(
        g_attn=g_attn[l].reshape(1, d), wa=wa, wb=wb, wg1=wg1, wg2=wg2, bgk=b_gk[l].reshape(1, -1),
        qn_t=_tile_gain(qn_diff[l], d // DIFF_HD), kn_t=_tile_gain(kn_diff[l], d // DIFF_HD),
        qmn_t=_tile_gain(qn_mem[l], d // MEM_HD), knm_t=_tile_gain(kn_mem[l], d // MEM_HD),
        g64=_group_matrix(MXU_DIM, DIFF_HD), g256=_group_matrix(MEM_HD, MEM_HD),
        lamp=jnp.stack([lam_q1[l], lam_k1[l], lam_q2[l], lam_k2[l]]).astype(F32),
        sub_diff=subln_diff[l].reshape(1, -1), sub_gla=subln_gla[l].reshape(1, -1),
        g_mem=g_mem[l].reshape(1, d), w_mem_kv=w_mem_kv[l].astype(BF16),
        wd=w_proj_diff[l].astype(BF16), wg=w_proj_gla[l].astype(BF16), wm=w_proj_mem[l].astype(BF16),
        wo=w_out[l].astype(BF16), g_ffn=g_ffn[l].reshape(1, d), w_up=w_up[l].astype(BF16),
        conv_w=conv_w[l], conv_b=conv_b[l].reshape(1, -1), w_down=w_down[l].astype(BF16),
    )


def _group(x, wts, lam_init, mem_k, mem_v, past_k, past_v, gla_state, conv_state, prompt):
    b, s, d = x.shape
    t = b * s
    x2 = x.reshape(t, d)
    tm = _pick(t, 512)
    p2, ka, va, gk = _inproj(x2, wts["g_attn"], wts["wa"], wts["wb"], wts["wg1"], wts["wg2"], wts["bgk"],
                             wts["qn_t"], wts["kn_t"], wts["qmn_t"], wts["g64"], wts["g256"],
                             _pick(s, 256) if prompt else _pick(t, 256), s, prompt)
    if prompt:
        ka = ka.reshape(b, DIFF_HEADS, 2, DIFF_HD, s).transpose(0, 4, 1, 2, 3)
    else:
        ka = ka.reshape(b, s, DIFF_HEADS, 2, DIFF_HD)
    p3 = p2.reshape(b, s, N_PTILES * d)
    gk3 = gk.reshape(b, s, GLA_HEADS * GLA_DK)
    if prompt:
        oa = _diff_prompt(p3, wts["lamp"], wts["sub_diff"], lam_init, _pick(s, 512), 256)
        ob, gla_new = _gla(p3, gk3, wts["sub_gla"], None, _pick(s, 512))
    else:
        oa = _diff_sample(p3, past_k, past_v, wts["lamp"], wts["sub_diff"], lam_init)
        ob, gla_new = _gla(p3, gk3, wts["sub_gla"], gla_state, s)
    om = _mem_attn(p3, mem_k, mem_v, _pick(s, 512))
    x1 = _mix(x2, oa.reshape(t, d), ob.reshape(t, d), om.reshape(t, d), p2,
              wts["wd"], wts["wg"], wts["wm"], wts["wo"], tm)
    if prompt:
        rows = _pick(s, 512)
        y, cs = _ffn(x1, wts["g_ffn"], wts["w_up"], wts["conv_w"], wts["conv_b"], wts["w_down"],
                     conv_state, 1, rows, s // rows)
    else:
        y, cs = _ffn(x1, wts["g_ffn"], wts["w_up"], wts["conv_w"], wts["conv_b"], wts["w_down"],
                     conv_state, b, s, 1)
    return y.reshape(b, s, d), ka, va, gla_new, cs


def kernel(x_prompt, x_sample, mem_prompt, cache_diff_k, cache_diff_v, cache_mem_k, cache_mem_v, state_gla, state_conv, g_attn, w_in, w_gk2, b_gk, qn_diff, kn_diff, lam_q1, lam_k1, lam_q2, lam_k2, subln_diff, subln_gla, g_mem, w_mem_kv, qn_mem, kn_mem, w_proj_diff, w_proj_gla, w_proj_mem, w_out, g_ffn, w_up, conv_w, conv_b, w_down):
    depth = g_attn.shape[0]
    d = D_MODEL
    xp, xs = x_prompt, x_sample
    bp, sp, _ = xp.shape
    bs, ss, _ = xs.shape
    n_mem = mem_prompt.shape[1]
    outs = [[] for _ in range(10)]
    for l in range(depth):
        lam_init = 0.8 - 0.6 * math.exp(-0.3 * l)
        wts = _layer_weights(l, g_attn, w_in, w_gk2, b_gk, qn_diff, kn_diff, lam_q1, lam_k1, lam_q2,
                             lam_k2, subln_diff, subln_gla, g_mem, w_mem_kv, qn_mem, kn_mem,
                             w_proj_diff, w_proj_gla, w_proj_mem, w_out, g_ffn, w_up, conv_w, conv_b,
                             w_down)
        mk, mv = _memkv(mem_prompt.reshape(bp * n_mem, d), wts["g_mem"], wts["w_mem_kv"], wts["knm_t"],
                        wts["g256"])
        mk = mk.reshape(bp, n_mem, d)
        mv = mv.reshape(bp, n_mem, d)
        xp, kp, vp, gp, cp = _group(xp, wts, lam_init, mk, mv, None, None, None,
                                    jnp.zeros((bp, CONV_W - 1, D_FF), F32), True)
        xs, ks_, vs_, gs, cs = _group(
            xs, wts, lam_init, _mem_cache_rows(cache_mem_k[l]), _mem_cache_rows(cache_mem_v[l]),
            cache_diff_k[l].transpose(0, 2, 3, 4, 1).reshape(bs, d, -1), cache_diff_v[l].reshape(bs, -1, 2 * DIFF_HD),
            state_gla[l], state_conv[l], False)
        vals = (kp, vp.reshape(bp, sp, DIFF_HEADS, 2 * DIFF_HD),
                mk.reshape(bp, n_mem, MEM_HEADS, MEM_HD), mv.reshape(bp, n_mem, MEM_HEADS, MEM_HD), gp, cp,
                ks_, vs_.reshape(bs, ss, DIFF_HEADS, 2 * DIFF_HD), gs, cs)
        for o, v in zip(outs, vals):
            o.append(v)
    return (xp, xs) + tuple(jnp.stack(o) for o in outs)
```

```python
import functools
import math

import jax
import jax.numpy as jnp
from jax import lax
from jax.experimental import pallas as pl
from jax.experimental.pallas import tpu as pltpu

F32 = jnp.float32
BF16 = jnp.bfloat16

D_MODEL = 1024
CHUNK = 64
EPS = 1e-6
DIFF_HEADS = 8
DIFF_HD = 64
DIFF_SCALE = DIFF_HD ** -0.5
LOG2E = math.log2(math.e)
GLA_HEADS = 4
GLA_DK = 128
GLA_DV = 256
GLA_SCALE = GLA_DK ** -0.5
GK_RANK = 16
GK_NORM = 16.0
MEM_HEADS = 4
MEM_HD = 256
MEM_SCALE = MEM_HD ** -0.5
D_FF = 2816
CONV_W = 3

LANES = 128
SUBLANES = 8
MXU_DIM = 256
VMEM_LIMIT_BYTES = 48 * 1024 * 1024

T_QA, T_KA, T_VA, T_QKB, T_VB, T_RB, T_QM, T_GATE = 0, 1, 2, 3, 4, 5, 6, 7
N_PTILES = 10
SUB_BLOCK = 8
NEG_BIG = -1e30


def _dot(a, b):
    return jnp.dot(a, b, preferred_element_type=F32)


def _dot_nt(a, b):
    return lax.dot_general(a, b, (((1,), (1,)), ((), ())), preferred_element_type=F32)


def _sigmoid(x):
    return 1.0 / (1.0 + jnp.exp(-x))


def _pick(n, pref):
    t = min(n, pref)
    while n % t:
        t -= 1
    return t


def _rms_rows(x, gain):
    ms = jnp.mean(x * x, axis=-1, keepdims=True)
    return x * lax.rsqrt(ms + EPS) * gain


def _group_rms(y, gmat, gain):
    slab = gmat.shape[0]
    outs = []
    for c in range(y.shape[-1] // slab):
        ys = y[:, c * slab:(c + 1) * slab]
        ms = _dot((ys * ys).astype(BF16), gmat)
        outs.append(ys * lax.rsqrt(ms + EPS))
    return jnp.concatenate(outs, axis=-1) * gain


def _group_matrix(slab, group):
    r = jnp.arange(slab) // group
    return jnp.where(r[:, None] == r[None, :], 1.0 / group, 0.0).astype(BF16)


def _inproj_kernel(x_ref, g_ref, wa_ref, wb_ref, wg1_ref, wg2_ref, bgk_ref, qn_ref, kn_ref, qmn_ref,
                   g64_ref, g256_ref, p_ref, ka_ref, vaf_ref, gk_ref, *, k_transposed):
    d = D_MODEL
    n_a = wa_ref.shape[1] // d
    h = _rms_rows(x_ref[...], g_ref[...]).astype(BF16)

    def project(j):
        if j < n_a:
            return _dot(h, wa_ref[:, j * d:(j + 1) * d])
        return _dot(h, wb_ref[:, (j - n_a) * d:(j - n_a + 1) * d])

    def finish(j, acc):
        if j == T_QA:
            out = _group_rms(acc, g64_ref[...], qn_ref[...]) * (DIFF_SCALE * LOG2E)
        elif j == T_KA:
            out = _group_rms(acc, g64_ref[...], kn_ref[...])
            ka_ref[...] = out.T if k_transposed else out
        elif j == T_VA:
            vaf_ref[...] = acc
            out = acc
        elif j == T_QKB:
            half = d // 2
            out = jnp.concatenate([acc[:, :half] * GLA_SCALE, acc[:, half:]], axis=-1)
            lr = _dot(h, wg1_ref[...])
            z = _dot(lr.astype(BF16), wg2_ref[...]) + bgk_ref[...]
            log_sig = jnp.minimum(z, 0.0) - jnp.log(1.0 + jnp.exp(-jnp.abs(z)))
            gk_ref[...] = log_sig * (1.0 / GK_NORM)
        elif j == T_VB:
            out = acc
        elif j == T_RB:
            out = acc * _sigmoid(acc)
        elif j == T_QM:
            out = _group_rms(acc, g256_ref[...], qmn_ref[...]) * MEM_SCALE
        else:
            out = _sigmoid(acc)
        p_ref[:, j * d:(j + 1) * d] = out.astype(BF16)

    acc_next = project(0)
    for j in range(N_PTILES):
        acc = acc_next
        if j + 1 < N_PTILES:
            acc_next = project(j + 1)
        finish(j, acc)


def _resident(shape):
    return pl.BlockSpec(shape, lambda *_: (0,) * len(shape), pipeline_mode=pl.Buffered(1))


def _inproj(x2, g_attn, wa, wb, wg1, wg2, bgk, qn_t, kn_t, qmn_t, g64, g256, tm, seq_len, k_transposed):
    t = x2.shape[0]
    d = D_MODEL
    row = lambda i: (i, 0)
    if k_transposed:
        per_seq = seq_len // tm
        ka_spec = pl.BlockSpec((None, d, tm), lambda i: (i // per_seq, 0, i % per_seq))
        ka_shape = jax.ShapeDtypeStruct((t // seq_len, d, seq_len), F32)
    else:
        ka_spec = pl.BlockSpec((tm, d), row)
        ka_shape = jax.ShapeDtypeStruct((t, d), F32)
    small = [wg1, wg2, bgk, qn_t, kn_t, qmn_t, g64, g256]
    return pl.pallas_call(
        functools.partial(_inproj_kernel, k_transposed=k_transposed),
        grid=(t // tm,),
        in_specs=[pl.BlockSpec((tm, d), row), _resident((1, d)), _resident(wa.shape), _resident(wb.shape)]
        + [_resident(a.shape) for a in small],
        out_specs=[
            pl.BlockSpec((tm, N_PTILES * d), row),
            ka_spec,
            pl.BlockSpec((tm, d), row),
            pl.BlockSpec((tm, GLA_HEADS * GLA_DK), row),
        ],
        out_shape=[
            jax.ShapeDtypeStruct((t, N_PTILES * d), BF16),
            ka_shape,
            jax.ShapeDtypeStruct((t, d), F32),
            jax.ShapeDtypeStruct((t, GLA_HEADS * GLA_DK), F32),
        ],
        compiler_params=pltpu.CompilerParams(
            dimension_semantics=("parallel",), vmem_limit_bytes=VMEM_LIMIT_BYTES),
        name="inproj",
    )(x2, g_attn, wa, wb, *small)


def _memkv_kernel(m_ref, g_ref, w_ref, kn_ref, g256_ref, k_ref, v_ref):
    j = pl.program_id(0)
    h = _rms_rows(m_ref[...], g_ref[...]).astype(BF16)
    acc = _dot(h, w_ref[...])

    @pl.when(j == 0)
    def _():
        k_ref[...] = _group_rms(acc, g256_ref[...], kn_ref[...])

    @pl.when(j == 1)
    def _():
        v_ref[...] = acc


def _memkv(mem2, g_mem, w_kv, knm_t, g256):
    t = mem2.shape[0]
    d = D_MODEL
    full = lambda j: (0, 0)
    return pl.pallas_call(
        _memkv_kernel,
        grid=(2,),
        in_specs=[
            pl.BlockSpec((t, d), full),
            pl.BlockSpec((1, d), full),
            pl.BlockSpec((d, d), lambda j: (0, j)),
            pl.BlockSpec((1, d), full),
            pl.BlockSpec(g256.shape, full),
        ],
        out_specs=[pl.BlockSpec((t, d), full), pl.BlockSpec((t, d), full)],
        out_shape=[jax.ShapeDtypeStruct((t, d), F32), jax.ShapeDtypeStruct((t, d), F32)],
        compiler_params=pltpu.CompilerParams(
            dimension_semantics=("arbitrary",), vmem_limit_bytes=VMEM_LIMIT_BYTES),
        name="memkv",
    )(mem2, g_mem, w_kv, knm_t, g256)


def _lambda_value(lamp, lam_init):
    a = jnp.sum(lamp[0:1, :] * lamp[1:2, :], axis=-1, keepdims=True)
    b = jnp.sum(lamp[2:3, :] * lamp[3:4, :], axis=-1, keepdims=True)
    return jnp.exp(a) - jnp.exp(b) + lam_init


def _stack_maps(q):
    lane = lax.broadcasted_iota(jnp.int32, q.shape, 1)
    zero = jnp.zeros_like(q)
    return jnp.concatenate([jnp.where(lane < DIFF_HD, q, zero), jnp.where(lane >= DIFF_HD, q, zero)], axis=0)


def _diff_finish(acc, l, lam, sub, lam_init, tq):
    o2 = acc / l
    o = o2[:tq] - lam * o2[tq:]
    return _rms_rows(o, sub) * (1.0 - lam_init)


ONES_ROWS = 16
LOOP_TRIPS = 4


def _diff_prompt_kernel(q_ref, qn_ref, k_ref, v_ref, lamp_ref, sub_ref, o_ref, qt_scr, vt_scr, m_scr, acc_scr,
                        st0_scr, st1_scr, p0_scr, p1_scr, alpha0_scr, alpha1_scr, mx0_scr, mx1_scr,
                        qtn_scr, stn_scr, mxn_scr, *, tq, tk, lam_init):
    qi = pl.program_id(2)
    n_kv = v_ref.shape[0] // tk
    hd2 = v_ref.shape[1]
    halves = tq // tk
    assert halves == 2
    all_cgs = list(range(2 * halves))
    upper_cgs = [g * halves + 1 for g in range(2)]

    @pl.when(qi == 0)
    def _():
        for c in range(n_kv):
            vt_scr[c, 0:hd2, :] = v_ref[c * tk:(c + 1) * tk, :].astype(F32).T.astype(BF16)
            vt_scr[c, hd2:hd2 + ONES_ROWS, :] = jnp.ones((ONES_ROWS, tk), BF16)

    def transposed_queries(ref):
        return _stack_maps(ref[...]).astype(F32).T.astype(BF16)

    @pl.when(qi == 0)
    def _():
        qt_scr[...] = transposed_queries(q_ref)

    @pl.when(qi > 0)
    def _():
        qt_scr[...] = qtn_scr[...]

    m_scr[...] = jnp.full(m_scr.shape, -jnp.inf, F32)
    acc_scr[...] = jnp.zeros(acc_scr.shape, F32)

    def scores(kb, cgs, qt=qt_scr):
        k = k_ref[pl.ds(pl.multiple_of(kb * tk, tk), tk), :]
        sts = [_dot(k, qt[:, cg * tk:(cg + 1) * tk]) for cg in cgs]
        return sts, [jnp.max(st, axis=0, keepdims=True) for st in sts]

    def softmax(sts, maxes, slot, cgs, masked):
        pts, alphas = [], []
        for st, mx, cg, msk in zip(sts, maxes, cgs, masked):
            cols = slice(cg * tk, (cg + 1) * tk)
            if msk:
                r = lax.broadcasted_iota(jnp.int32, st.shape, 0)
                c = lax.broadcasted_iota(jnp.int32, st.shape, 1)
                st = jnp.where((r // CHUNK) <= (c // CHUNK), st, -jnp.inf)
                mx = jnp.max(st, axis=0, keepdims=True)
            m_prev = m_scr[slot, :, cols]
            m_new = jnp.maximum(m_prev, mx)
            alphas.append(jnp.exp2(m_prev - m_new))
            pts.append(jnp.exp2(st - m_new).astype(BF16))
            m_scr[slot, :, cols] = m_new
        return pts, alphas

    def accumulate(kb, slot, cgs, pts, alphas):
        vt = vt_scr[kb]
        for pt, alpha, cg in zip(pts, alphas, cgs):
            cols = slice(cg * tk, (cg + 1) * tk)
            acc_scr[slot, :, cols] = alpha * acc_scr[slot, :, cols] + _dot(vt, pt)

    st_scr, p_scr, alpha_scr = (st0_scr, st1_scr), (p0_scr, p1_scr), (alpha0_scr, alpha1_scr)
    mx_scr = (mx0_scr, mx1_scr)

    def put(scr, buf, cgs, vals):
        for cg, val in zip(cgs, vals):
            scr[buf][cg] = val

    def get(scr, buf, cgs):
        return [scr[buf][cg] for cg in cgs]

    def put_scores(buf, cgs, sts_maxes):
        put(st_scr, buf, cgs, sts_maxes[0])
        put(mx_scr, buf, cgs, sts_maxes[1])

    def trip(t, par):
        pts, alphas = softmax(get(st_scr, par, all_cgs), get(mx_scr, par, all_cgs), par, all_cgs,
                              [False] * len(all_cgs))
        put(p_scr, par, all_cgs, pts)
        put(alpha_scr, par, all_cgs, alphas)
        put_scores(1 - par, all_cgs, scores(jnp.minimum(t + 1, n_kv - 1), all_cgs))
        accumulate(jnp.maximum(t - 1, 0), 1 - par, all_cgs,
                   get(p_scr, 1 - par, all_cgs), get(alpha_scr, 1 - par, all_cgs))

    def loop_body(i, carry):
        for u in range(LOOP_TRIPS):
            trip(LOOP_TRIPS * i + u, u % 2)
        return carry

    n_trips = qi * halves

    @pl.when(qi == 0)
    def _():
        put_scores(0, all_cgs, scores(0, all_cgs))

    @pl.when(qi > 0)
    def _():
        put_scores(0, all_cgs, ([stn_scr[cg] for cg in all_cgs], [mxn_scr[cg] for cg in all_cgs]))

    put(p_scr, 1, all_cgs, [jnp.zeros((tk, tk), BF16)] * len(all_cgs))
    put(alpha_scr, 1, all_cgs, [jnp.ones((1, tk), F32)] * len(all_cgs))
    assert LOOP_TRIPS == 4
    lax.fori_loop(0, n_trips // LOOP_TRIPS, loop_body, 0)

    @pl.when(n_trips % LOOP_TRIPS == 2)
    def _():
        trip(n_trips - 2, 0)
        trip(n_trips - 1, 1)

    kb0 = n_trips
    upper = scores(kb0 + 1, upper_cgs)
    qtn_scr[...] = transposed_queries(qn_ref)
    nxt = scores(0, all_cgs, qt=qtn_scr)
    for cg in all_cgs:
        stn_scr[cg] = nxt[0][cg]
        mxn_scr[cg] = nxt[1][cg]
    accumulate(jnp.maximum(kb0 - 1, 0), 1, all_cgs, get(p_scr, 1, all_cgs), get(alpha_scr, 1, all_cgs))
    pts, alphas = softmax(get(st_scr, 0, all_cgs), get(mx_scr, 0, all_cgs), 0, all_cgs,
                          [cg % halves == 0 for cg in all_cgs])
    accumulate(kb0, 0, all_cgs, pts, alphas)
    pts, alphas = softmax(upper[0], upper[1], 1, upper_cgs, [True] * len(upper_cgs))
    accumulate(kb0 + 1, 1, upper_cgs, pts, alphas)

    m_all = jnp.maximum(m_scr[0], m_scr[1])
    acc_all = jnp.exp2(m_scr[0] - m_all) * acc_scr[0] + jnp.exp2(m_scr[1] - m_all) * acc_scr[1]

    lam = _lambda_value(lamp_ref[...], lam_init)
    o2t = acc_all[0:hd2, :] / acc_all[hd2:hd2 + 1, :]
    o = (o2t[:, :tq] - lam * o2t[:, tq:]).T
    o_ref[...] = (_rms_rows(o, sub_ref[...]) * (1.0 - lam_init)).astype(o_ref.dtype)


def _diff_prompt(p3, lamp, sub, lam_init, tq, tk):
    b, s, _ = p3.shape
    hd2 = 2 * DIFF_HD
    kern = functools.partial(_diff_prompt_kernel, tq=tq, tk=tk, lam_init=lam_init)
    per_d = D_MODEL // hd2
    n_cg = 2 * tq // tk
    return pl.pallas_call(
        kern,
        grid=(b, DIFF_HEADS, s // tq),
        in_specs=[
            pl.BlockSpec((None, tq, hd2), lambda bi, h, qi: (bi, qi, T_QA * per_d + h)),
            pl.BlockSpec((None, tq, hd2), lambda bi, h, qi: (bi, jnp.minimum(qi + 1, s // tq - 1), T_QA * per_d + h)),
            pl.BlockSpec((None, s, hd2), lambda bi, h, qi: (bi, 0, T_KA * per_d + h)),
            pl.BlockSpec((None, s, hd2), lambda bi, h, qi: (bi, 0, T_VA * per_d + h)),
            pl.BlockSpec(lamp.shape, lambda bi, h, qi: (0, 0)),
            pl.BlockSpec(sub.shape, lambda bi, h, qi: (0, 0)),
        ],
        out_specs=pl.BlockSpec((None, tq, hd2), lambda bi, h, qi: (bi, qi, h)),
        out_shape=jax.ShapeDtypeStruct((b, s, D_MODEL), BF16),
        scratch_shapes=[
            pltpu.VMEM((hd2, 2 * tq), BF16),
            pltpu.VMEM((s // tk, hd2 + ONES_ROWS, tk), BF16),
            pltpu.VMEM((2, 1, 2 * tq), F32),
            pltpu.VMEM((2, hd2 + ONES_ROWS, 2 * tq), F32),
            pltpu.VMEM((n_cg, tk, tk), F32),
            pltpu.VMEM((n_cg, tk, tk), F32),
            pltpu.VMEM((n_cg, tk, tk), BF16),
            pltpu.VMEM((n_cg, tk, tk), BF16),
            pltpu.VMEM((n_cg, 1, tk), F32),
            pltpu.VMEM((n_cg, 1, tk), F32),
            pltpu.VMEM((n_cg, 1, tk), F32),
            pltpu.VMEM((n_cg, 1, tk), F32),
            pltpu.VMEM((hd2, 2 * tq), BF16),
            pltpu.VMEM((n_cg, tk, tk), F32),
            pltpu.VMEM((n_cg, 1, tk), F32),
        ],
        compiler_params=pltpu.CompilerParams(
            dimension_semantics=("parallel", "parallel", "arbitrary"), vmem_limit_bytes=VMEM_LIMIT_BYTES),
        name="diff_attn_prompt",
    )(p3, p3, p3, p3, lamp, sub)


SAMPLE_HEAD_GROUP = 4


def _diff_sample_kernel(q_ref, kn_ref, vn_ref, kc_ref, vc_ref, lamp_ref, sub_ref, o_ref, *, lam_init):
    tq = q_ref.shape[0]
    hd2 = 2 * DIFF_HD
    n_past = kc_ref.shape[1]
    lam = _lambda_value(lamp_ref[...], lam_init)
    for h0 in range(0, DIFF_HEADS, SAMPLE_HEAD_GROUP):
        heads = range(h0, h0 + SAMPLE_HEAD_GROUP)
        cols = [slice(h * hd2, (h + 1) * hd2) for h in heads]
        q2s = [_stack_maps(q_ref[:, c]) for c in cols]
        s_cs = [_dot(q2, kc_ref[c, :].astype(BF16)) for q2, c in zip(q2s, cols)]
        s_ns = [_dot_nt(q2, kn_ref[:, c]) for q2, c in zip(q2s, cols)]
        pcs, pns, ls = [], [], []
        for s_c, s_n in zip(s_cs, s_ns):
            m = jnp.maximum(jnp.max(s_c, axis=-1, keepdims=True), jnp.max(s_n, axis=-1, keepdims=True))
            p_c = jnp.exp2(s_c - m)
            p_n = jnp.exp2(s_n - m)
            ls.append(jnp.sum(p_c, axis=-1, keepdims=True) + jnp.sum(p_n, axis=-1, keepdims=True))
            pcs.append(p_c.astype(BF16))
            pns.append(p_n.astype(BF16))
        for h, c, p_c, p_n, l in zip(heads, cols, pcs, pns, ls):
            vc = vc_ref[pl.ds(h, n_past, stride=DIFF_HEADS), :]
            acc = _dot(p_c, vc.astype(BF16)) + _dot(p_n, vn_ref[:, c])
            o_ref[:, c] = _diff_finish(acc, l, lam, sub_ref[...], lam_init, tq).astype(o_ref.dtype)


def _diff_sample(p3, kc_t, vc, lamp, sub, lam_init):
    b, l, _ = p3.shape
    n_past = kc_t.shape[2]
    hd2 = 2 * DIFF_HD
    d = D_MODEL
    kern = functools.partial(_diff_sample_kernel, lam_init=lam_init)
    return pl.pallas_call(
        kern,
        grid=(b,),
        in_specs=[
            pl.BlockSpec((None, l, d), lambda bi: (bi, 0, T_QA)),
            pl.BlockSpec((None, l, d), lambda bi: (bi, 0, T_KA)),
            pl.BlockSpec((None, l, d), lambda bi: (bi, 0, T_VA)),
            pl.BlockSpec((None, d, n_past), lambda bi: (bi, 0, 0)),
            pl.BlockSpec((None, n_past * DIFF_HEADS, hd2), lambda bi: (bi, 0, 0)),
            pl.BlockSpec(lamp.shape, lambda bi: (0, 0)),
            pl.BlockSpec(sub.shape, lambda bi: (0, 0)),
        ],
        out_specs=pl.BlockSpec((None, l, d), lambda bi: (bi, 0, 0)),
        out_shape=jax.ShapeDtypeStruct((b, l, D_MODEL), BF16),
        compiler_params=pltpu.CompilerParams(
            dimension_semantics=("parallel",), vmem_limit_bytes=VMEM_LIMIT_BYTES),
        name="diff_attn_sample",
    )(p3, p3, p3, kc_t, vc, lamp, sub)


def _bcast_rows(x, period, row):
    r, c = x.shape
    x3 = x.reshape(r // period, period, c)
    return jnp.broadcast_to(x3[:, row:row + 1, :], x3.shape).reshape(r, c)


def _gla_kernel(*refs, rows, has_init):
    if has_init:
        q_ref, k_ref, v_ref, r_ref, g_ref, sub_ref, s0_ref, o_ref, sout_ref, st_scr, kf_scr, b_scr = refs
    else:
        q_ref, k_ref, v_ref, r_ref, g_ref, sub_ref, o_ref, sout_ref, st_scr, kf_scr, b_scr = refs
        s0_ref = None
    step = pl.program_id(2)
    n_chunks = rows // CHUNK
    n_sub = CHUNK // SUB_BLOCK

    @pl.when(step == 0)
    def _():
        if has_init:
            st_scr[...] = s0_ref[...]
        else:
            st_scr[...] = jnp.zeros(st_scr.shape, F32)

    q = q_ref[...].astype(F32)
    k = k_ref[...].astype(F32)
    g = g_ref[...]
    v = v_ref[...]
    chunk_rows = [slice(c * CHUNK, (c + 1) * CHUNK) for c in range(n_chunks)]

    ri = lax.broadcasted_iota(jnp.int32, (CHUNK, CHUNK), 0)
    ci = lax.broadcasted_iota(jnp.int32, (CHUNK, CHUNK), 1)
    tri = jnp.where(ci <= ri, 1.0, 0.0).astype(BF16)
    gw = jnp.concatenate([g[sl] for sl in chunk_rows], axis=1)
    g1 = gw.astype(BF16)
    rem = gw - g1.astype(F32)
    g2 = rem.astype(BF16)
    g3 = (rem - g2.astype(F32)).astype(BF16)
    bw = _dot(tri, g1) + _dot(tri, g2) + _dot(tri, g3)
    b = jnp.concatenate([bw[:, c * GLA_DK:(c + 1) * GLA_DK] for c in range(n_chunks)], axis=0)
    bex = b - g

    b_last = _bcast_rows(b, CHUNK, CHUNK - 1)
    b_blk = _bcast_rows(bex, SUB_BLOCK, 0)
    q_blk = q * jnp.exp(b - b_blk)
    q_chk = q * jnp.exp(b)
    k_end = k * jnp.exp(b_last - b)

    rowc = lax.broadcasted_iota(jnp.int32, (rows, GLA_DK), 0) % CHUNK
    zero = jnp.zeros_like(q)

    lhs_parts, rhs_parts = [], []
    for blk in range(1, n_sub):
        b_ref_blk = _bcast_rows(bex, CHUNK, blk * SUB_BLOCK)
        k_blk = k * jnp.exp(jnp.where(rowc < blk * SUB_BLOCK, b_ref_blk - b, NEG_BIG))
        lhs_parts.append(jnp.where((rowc // SUB_BLOCK) == blk, q_blk, zero))
        rhs_parts.append(k_blk)
    lhs = jnp.concatenate(lhs_parts, axis=-1).astype(BF16)
    rhs = jnp.concatenate(rhs_parts, axis=-1).astype(BF16)

    pad = SUB_BLOCK
    kf_scr[0:pad, :] = jnp.zeros((pad, GLA_DK), F32)
    b_scr[0:pad, :] = jnp.zeros((pad, GLA_DK), F32)
    kf_scr[pad:pad + rows, :] = k
    b_scr[pad:pad + rows, :] = b
    row_sub = rowc % SUB_BLOCK
    terms = []
    for d in range(SUB_BLOCK):
        kd = kf_scr[pad - d:pad - d + rows, :]
        bd = b_scr[pad - d:pad - d + rows, :]
        e = jnp.exp(jnp.where(row_sub >= d, b - bd, NEG_BIG))
        terms.append((q * kd * e).astype(BF16))
    sel_r = lax.broadcasted_iota(jnp.int32, (SUB_BLOCK * GLA_DK, LANES), 0) // GLA_DK
    sel_c = lax.broadcasted_iota(jnp.int32, (SUB_BLOCK * GLA_DK, LANES), 1)
    selector = jnp.where(sel_r + sel_c == SUB_BLOCK - 1, 1.0, 0.0).astype(BF16)
    diag_sums = _dot(jnp.concatenate(terms, axis=-1), selector)

    def skew(x):
        return pltpu.roll(x, LANES - (SUB_BLOCK - 1), 1, stride=1, stride_axis=0)

    atts = [(_dot_nt(lhs[sl], rhs[sl]) + skew(diag_sums[sl])[:, :CHUNK]).astype(BF16) for sl in chunk_rows]
    d_sts = [_dot(k_end[sl].T.astype(BF16), v[sl]) for sl in chunk_rows]
    o_intra = [_dot(att, v[sl]) for att, sl in zip(atts, chunk_rows)]
    decs = [jnp.exp(b[sl].T[:, CHUNK - 1:CHUNK]) for sl in chunk_rows]
    st = st_scr[...]
    outs = []
    for c, sl in enumerate(chunk_rows):
        outs.append(o_intra[c] + _dot(q_chk[sl].astype(BF16), st.astype(BF16)))
        st = st * decs[c] + d_sts[c]
    st_scr[...] = st
    o = jnp.concatenate(outs, axis=0) if n_chunks > 1 else outs[0]
    o_ref[...] = (_rms_rows(o, sub_ref[...]) * r_ref[...].astype(F32)).astype(o_ref.dtype)

    @pl.when(step == pl.num_programs(2) - 1)
    def _():
        sout_ref[...] = st_scr[...]


def _gla(p3, gk3, sub, s0, rows):
    b, s, _ = p3.shape
    has_init = s0 is not None
    kq = D_MODEL // GLA_DK
    kv = D_MODEL // GLA_DV
    in_specs = [
        pl.BlockSpec((None, rows, GLA_DK), lambda bi, h, r: (bi, r, T_QKB * kq + h)),
        pl.BlockSpec((None, rows, GLA_DK), lambda bi, h, r: (bi, r, T_QKB * kq + GLA_HEADS + h)),
        pl.BlockSpec((None, rows, GLA_DV), lambda bi, h, r: (bi, r, T_VB * kv + h)),
        pl.BlockSpec((None, rows, GLA_DV), lambda bi, h, r: (bi, r, T_RB * kv + h)),
        pl.BlockSpec((None, rows, GLA_DK), lambda bi, h, r: (bi, r, h)),
        pl.BlockSpec(sub.shape, lambda bi, h, r: (0, 0)),
    ]
    args = [p3, p3, p3, p3, gk3, sub]
    if has_init:
        in_specs.append(pl.BlockSpec((None, None, GLA_DK, GLA_DV), lambda bi, h, r: (bi, h, 0, 0)))
        args.append(s0)
    kern = functools.partial(_gla_kernel, rows=rows, has_init=has_init)
    return pl.pallas_call(
        kern,
        grid=(b, GLA_HEADS, s // rows),
        in_specs=in_specs,
        out_specs=[
            pl.BlockSpec((None, rows, GLA_DV), lambda bi, h, r: (bi, r, h)),
            pl.BlockSpec((None, None, GLA_DK, GLA_DV), lambda bi, h, r: (bi, h, 0, 0)),
        ],
        out_shape=[
            jax.ShapeDtypeStruct((b, s, GLA_HEADS * GLA_DV), BF16),
            jax.ShapeDtypeStruct((b, GLA_HEADS, GLA_DK, GLA_DV), F32),
        ],
        scratch_shapes=[
            pltpu.VMEM((GLA_DK, GLA_DV), F32),
            pltpu.VMEM((SUB_BLOCK + rows, GLA_DK), F32),
            pltpu.VMEM((SUB_BLOCK + rows, GLA_DK), F32),
        ],
        compiler_params=pltpu.CompilerParams(
            dimension_semantics=("parallel", "parallel", "arbitrary"), vmem_limit_bytes=VMEM_LIMIT_BYTES),
        name="gla",
    )(*args)


def _mem_attn_kernel(q_ref, k_ref, v_ref, o_ref, *, tiled):
    def head(ref, h):
        if not tiled:
            return ref[:, h * MEM_HD:(h + 1) * MEM_HD]
        period = MEM_HEADS * MEM_HD // LANES
        n_tok = ref.shape[0] // period
        return jnp.concatenate([ref[pl.ds(half * MEM_HEADS + h, n_tok, stride=period), :]
                                for half in range(MEM_HD // LANES)], axis=1)

    heads = range(MEM_HEADS)
    scores = [_dot_nt(q_ref[:, h * MEM_HD:(h + 1) * MEM_HD], head(k_ref, h).astype(BF16)) for h in heads]
    probs, sums = [], []
    for s in scores:
        p = jnp.exp(s - jnp.max(s, axis=-1, keepdims=True))
        sums.append(jnp.sum(p, axis=-1, keepdims=True))
        probs.append(p.astype(BF16))
    outs = [_dot(p, head(v_ref, h).astype(BF16)) / l for h, p, l in zip(heads, probs, sums)]
    o_ref[...] = jnp.concatenate(outs, axis=-1).astype(o_ref.dtype)


def _mem_attn(p3, mk, mv, tq):
    b, s, _ = p3.shape
    d = D_MODEL
    kv_block = (None,) + mk.shape[1:]
    return pl.pallas_call(
        functools.partial(_mem_attn_kernel, tiled=mk.shape[2] == LANES),
        grid=(b, s // tq),
        in_specs=[
            pl.BlockSpec((None, tq, d), lambda bi, i: (bi, i, T_QM)),
            pl.BlockSpec(kv_block, lambda bi, i: (bi, 0, 0)),
            pl.BlockSpec(kv_block, lambda bi, i: (bi, 0, 0)),
        ],
        out_specs=pl.BlockSpec((None, tq, d), lambda bi, i: (bi, i, 0)),
        out_shape=jax.ShapeDtypeStruct((b, s, d), BF16),
        compiler_params=pltpu.CompilerParams(
            dimension_semantics=("parallel", "parallel"), vmem_limit_bytes=VMEM_LIMIT_BYTES),
        name="mem_attn",
    )(p3, mk, mv)


def _mix_kernel(x_ref, oa_ref, ob_ref, om_ref, ga_ref, gb_ref, gm_ref, wd_ref, wg_ref, wm_ref, wo_ref,
                y_ref):
    m = (ga_ref[...].astype(F32) * _dot(oa_ref[...], wd_ref[...])
         + gb_ref[...].astype(F32) * _dot(ob_ref[...], wg_ref[...])
         + gm_ref[...].astype(F32) * _dot(om_ref[...], wm_ref[...]))
    y_ref[...] = x_ref[...] + _dot(m.astype(BF16), wo_ref[...])


def _mix(x2, oa, ob, om, p2, wd, wg, wm, wo, tm):
    t = x2.shape[0]
    d = D_MODEL
    row = lambda i: (i, 0)
    full = lambda i: (0, 0)
    wspec = pl.BlockSpec((d, d), full, pipeline_mode=pl.Buffered(1))
    return pl.pallas_call(
        _mix_kernel,
        grid=(t // tm,),
        in_specs=[
            pl.BlockSpec((tm, d), row), pl.BlockSpec((tm, d), row), pl.BlockSpec((tm, d), row),
            pl.BlockSpec((tm, d), row),
            pl.BlockSpec((tm, d), lambda i: (i, T_GATE)),
            pl.BlockSpec((tm, d), lambda i: (i, T_GATE + 1)),
            pl.BlockSpec((tm, d), lambda i: (i, T_GATE + 2)),
            wspec, wspec, wspec, wspec,
        ],
        out_specs=pl.BlockSpec((tm, d), row),
        out_shape=jax.ShapeDtypeStruct((t, d), F32),
        compiler_params=pltpu.CompilerParams(
            dimension_semantics=("parallel",), vmem_limit_bytes=VMEM_LIMIT_BYTES),
        name="mix_out",
    )(x2, oa, ob, om, p2, p2, p2, wd, wg, wm, wo)


FFN_CHUNK = 256


def _ffn_kernel(x_ref, g_ref, wup_ref, cw_ref, cb_ref, wd_ref, cs_ref, y_ref, cso_ref,
                u_scr, carry_scr, gv_scr, *, n_seq, seq_rows, tiles_per_seq):
    i = pl.program_id(0)
    gap = SUBLANES
    stride = seq_rows + gap
    tail = CONV_W - 1
    n_chunks = D_FF // FFN_CHUNK
    x = x_ref[...]
    h = _rms_rows(x, g_ref[...]).astype(BF16)

    first = (i % tiles_per_seq) == 0
    for s in range(n_seq):
        base = s * stride

        @pl.when(first)
        def _():
            u_scr[base:base + gap, :] = jnp.zeros((gap, D_FF), F32)
            u_scr[base + gap - tail:base + gap, :] = cs_ref[s]

        @pl.when(jnp.logical_not(first))
        def _():
            u_scr[base:base + gap, :] = carry_scr[...]

    def up(c):
        cols = slice(c * FFN_CHUNK, (c + 1) * FFN_CHUNK)
        gate_cols = slice(D_FF + c * FFN_CHUNK, D_FF + (c + 1) * FFN_CHUNK)
        return _dot(h, wup_ref[:, cols]), _dot(h, wup_ref[:, gate_cols])

    def gated(c, u, vv):
        cols = slice(c * FFN_CHUNK, (c + 1) * FFN_CHUNK)
        cw = cw_ref[:, cols]
        outs = []
        for s in range(n_seq):
            base = s * stride + gap
            u_scr[base:base + seq_rows, cols] = u[s * seq_rows:(s + 1) * seq_rows]
            conv = cb_ref[:, cols]
            for j in range(CONV_W):
                off = base - tail + j
                conv = conv + cw[j:j + 1, :] * u_scr[off:off + seq_rows, cols]
            outs.append(conv)
        uc = jnp.concatenate(outs, axis=0) if n_seq > 1 else outs[0]
        gelu = 0.5 * uc * (1.0 + jnp.tanh(math.sqrt(2.0 / math.pi) * (uc + 0.044715 * (uc * uc * uc))))
        return (gelu * vv).astype(BF16)

    nxt = up(0)
    for c in range(n_chunks):
        cur = nxt
        if c + 1 < n_chunks:
            nxt = up(c + 1)
        gv_scr[:, c * FFN_CHUNK:(c + 1) * FFN_CHUNK] = gated(c, *cur)
    y_ref[...] = x + _dot(gv_scr[...], wd_ref[...])

    for s in range(n_seq):
        base = s * stride + gap
        cso_ref[s] = u_scr[base + seq_rows - tail:base + seq_rows, :]
    carry_scr[...] = u_scr[seq_rows:seq_rows + gap, :]


def _ffn(x2, g_ffn, w_up, conv_w, conv_b, w_down, conv_state, n_seq, seq_rows, tiles_per_seq):
    t = x2.shape[0]
    d = D_MODEL
    tm = n_seq * seq_rows
    nb = conv_state.shape[0]
    tail = CONV_W - 1
    kern = functools.partial(_ffn_kernel, n_seq=n_seq, seq_rows=seq_rows, tiles_per_seq=tiles_per_seq)
    y, tails = pl.pallas_call(
        kern,
        grid=(t // tm,),
        in_specs=[
            pl.BlockSpec((tm, d), lambda i: (i, 0)),
            _resident((1, d)),
            _resident(w_up.shape),
            _resident(conv_w.shape),
            _resident(conv_b.shape),
            _resident(w_down.shape),
            pl.BlockSpec((n_seq, tail, D_FF), lambda i: (i // tiles_per_seq, 0, 0)),
        ],
        out_specs=[
            pl.BlockSpec((tm, d), lambda i: (i, 0)),
            pl.BlockSpec((n_seq, tail, D_FF), lambda i: (i, 0, 0)),
        ],
        out_shape=[
            jax.ShapeDtypeStruct((t, d), F32),
            jax.ShapeDtypeStruct((nb * tiles_per_seq, tail, D_FF), F32),
        ],
        scratch_shapes=[
            pltpu.VMEM((n_seq * (seq_rows + SUBLANES), D_FF), F32),
            pltpu.VMEM((SUBLANES, D_FF), F32),
            pltpu.VMEM((tm, D_FF), BF16),
        ],
        compiler_params=pltpu.CompilerParams(
            dimension_semantics=("arbitrary",), vmem_limit_bytes=VMEM_LIMIT_BYTES),
        name="conv_ffn",
    )(x2, g_ffn, w_up, conv_w, conv_b, w_down, conv_state)
    return y, tails.reshape(nb, tiles_per_seq, tail, D_FF)[:, -1]


def _mem_cache_rows(c):
    b, n, h, hd = c.shape
    return c.reshape(b, n, h, hd // LANES, LANES).transpose(0, 1, 3, 2, 4).reshape(b, -1, LANES)


def _tile_gain(g, reps):
    return jnp.tile(g.astype(F32), reps).reshape(1, -1)


def _layer_weights(l, g_attn, w_in, w_gk2, b_gk, qn_diff, kn_diff, lam_q1, lam_k1, lam_q2, lam_k2,
                   subln_diff, subln_gla, g_mem, w_mem_kv, qn_mem, kn_mem, w_proj_diff, w_proj_gla,
                   w_proj_mem, w_out, g_ffn, w_up, conv_w, conv_b, w_down):
    d = D_MODEL
    w = w_in[l]
    lr0 = 6 * d
    wa = w[:, :lr0].astype(BF16)
    wb = w[:, lr0 + GK_RANK:].astype(BF16)
    wg1 = jnp.pad(w[:, lr0:lr0 + GK_RANK], ((0, 0), (0, LANES - GK_RANK))).astype(BF16)
    wg2 = jnp.pad(w_gk2[l], ((0, LANES - GK_RANK), (0, 0))).astype(BF16)
    return dict(
        g_attn=g_attn[l].reshape(1, d), wa=wa, wb=wb, wg1=wg1, wg2=wg2, bgk=b_gk[l].reshape(1, -1),
        qn_t=_tile_gain(qn_diff[l], d // DIFF_HD), kn_t=_tile_gain(kn_diff[l], d // DIFF_HD),
        qmn_t=_tile_gain(qn_mem[l], d // MEM_HD), knm_t=_tile_gain(kn_mem[l], d // MEM_HD),
        g64=_group_matrix(MXU_DIM, DIFF_HD), g256=_group_matrix(MEM_HD, MEM_HD),
        lamp=jnp.stack([lam_q1[l], lam_k1[l], lam_q2[l], lam_k2[l]]).astype(F32),
        sub_diff=subln_diff[l].reshape(1, -1), sub_gla=subln_gla[l].reshape(1, -1),
        g_mem=g_mem[l].reshape(1, d), w_mem_kv=w_mem_kv[l].astype(BF16),
        wd=w_proj_diff[l].astype(BF16), wg=w_proj_gla[l].astype(BF16), wm=w_proj_mem[l].astype(BF16),
        wo=w_out[l].astype(BF16), g_ffn=g_ffn[l].reshape(1, d), w_up=w_up[l].astype(BF16),
        conv_w=conv_w[l], conv_b=conv_b[l].reshape(1, -1), w_down=w_down[l].astype(BF16),
    )


def _group(x, wts, lam_init, mem_k, mem_v, past_k, past_v, gla_state, conv_state, prompt):
    b, s, d = x.shape
    t = b * s
    x2 = x.reshape(t, d)
    tm = _pick(t, 512)
    p2, ka, va, gk = _inproj(x2, wts["g_attn"], wts["wa"], wts["wb"], wts["wg1"], wts["wg2"], wts["bgk"],
                             wts["qn_t"], wts["kn_t"], wts["qmn_t"], wts["g64"], wts["g256"],
                             _pick(s, 256) if prompt else _pick(t, 256), s, prompt)
    if prompt:
        ka = ka.reshape(b, DIFF_HEADS, 2, DIFF_HD, s).transpose(0, 4, 1, 2, 3)
    else:
        ka = ka.reshape(b, s, DIFF_HEADS, 2, DIFF_HD)
    p3 = p2.reshape(b, s, N_PTILES * d)
    gk3 = gk.reshape(b, s, GLA_HEADS * GLA_DK)
    if prompt:
        oa = _diff_prompt(p3, wts["lamp"], wts["sub_diff"], lam_init, _pick(s, 512), 256)
        ob, gla_new = _gla(p3, gk3, wts["sub_gla"], None, _pick(s, 512))
    else:
        oa = _diff_sample(p3, past_k, past_v, wts["lamp"], wts["sub_diff"], lam_init)
        ob, gla_new = _gla(p3, gk3, wts["sub_gla"], gla_state, s)
    om = _mem_attn(p3, mem_k, mem_v, _pick(s, 512))
    x1 = _mix(x2, oa.reshape(t, d), ob.reshape(t, d), om.reshape(t, d), p2,
              wts["wd"], wts["wg"], wts["wm"], wts["wo"], tm)
    if prompt:
        rows = _pick(s, 512)
        y, cs = _ffn(x1, wts["g_ffn"], wts["w_up"], wts["conv_w"], wts["conv_b"], wts["w_down"],
                     conv_state, 1, rows, s // rows)
    else:
        y, cs = _ffn(x1, wts["g_ffn"], wts["w_up"], wts["conv_w"], wts["conv_b"], wts["w_down"],
                     conv_state, b, s, 1)
    return y.reshape(b, s, d), ka, va, gla_new, cs


def kernel(x_prompt, x_sample, mem_prompt, cache_diff_k, cache_diff_v, cache_mem_k, cache_mem_v, state_gla, state_conv, g_attn, w_in, w_gk2, b_gk, qn_diff, kn_diff, lam_q1, lam_k1, lam_q2, lam_k2, subln_diff, subln_gla, g_mem, w_mem_kv, qn_mem, kn_mem, w_proj_diff, w_proj_gla, w_proj_mem, w_out, g_ffn, w_up, conv_w, conv_b, w_down):
    depth = g_attn.shape[0]
    d = D_MODEL
    xp, xs = x_prompt, x_sample
    bp, sp, _ = xp.shape
    bs, ss, _ = xs.shape
    n_mem = mem_prompt.shape[1]
    outs = [[] for _ in range(10)]
    for l in range(depth):
        lam_init = 0.8 - 0.6 * math.exp(-0.3 * l)
        wts = _layer_weights(l, g_attn, w_in, w_gk2, b_gk, qn_diff, kn_diff, lam_q1, lam_k1, lam_q2,
                             lam_k2, subln_diff, subln_gla, g_mem, w_mem_kv, qn_mem, kn_mem,
                             w_proj_diff, w_proj_gla, w_proj_mem, w_out, g_ffn, w_up, conv_w, conv_b,
                             w_down)
        mk, mv = _memkv(mem_prompt.reshape(bp * n_mem, d), wts["g_mem"], wts["w_mem_kv"], wts["knm_t"],
                        wts["g256"])
        mk = mk.reshape(bp, n_mem, d)
        mv = mv.reshape(bp, n_mem, d)
        xp, kp, vp, gp, cp = _group(xp, wts, lam_init, mk, mv, None, None, None,
                                    jnp.zeros((bp, CONV_W - 1, D_FF), F32), True)
        xs, ks_, vs_, gs, cs = _group(
            xs, wts, lam_init, _mem_cache_rows(cache_mem_k[l]), _mem_cache_rows(cache_mem_v[l]),
            cache_diff_k[l].transpose(0, 2, 3, 4, 1).reshape(bs, d, -1), cache_diff_v[l].reshape(bs, -1, 2 * DIFF_HD),
            state_gla[l], state_conv[l], False)
        vals = (kp, vp.reshape(bp, sp, DIFF_HEADS, 2 * DIFF_HD),
                mk.reshape(bp, n_mem, MEM_HEADS, MEM_HD), mv.reshape(bp, n_mem, MEM_HEADS, MEM_HD), gp, cp,
                ks_, vs_.reshape(bs, ss, DIFF_HEADS, 2 * DIFF_HD), gs, cs)
        for o, v in zip(outs, vals):
            o.append(v)
    return (xp, xs) + tuple(jnp.stack(o) for o in outs)
```

```python
import functools
import math

import jax
import jax.numpy as jnp
from jax import lax
from jax.experimental import pallas as pl
from jax.experimental.pallas import tpu as pltpu

F32 = jnp.float32
BF16 = jnp.bfloat16

D_MODEL = 1024
CHUNK = 64
EPS = 1e-6
DIFF_HEADS = 8
DIFF_HD = 64
DIFF_SCALE = DIFF_HD ** -0.5
LOG2E = math.log2(math.e)
GLA_HEADS = 4
GLA_DK = 128
GLA_DV = 256
GLA_SCALE = GLA_DK ** -0.5
GK_RANK = 16
GK_NORM = 16.0
MEM_HEADS = 4
MEM_HD = 256
MEM_SCALE = MEM_HD ** -0.5
D_FF = 2816
CONV_W = 3

LANES = 128
SUBLANES = 8
MXU_DIM = 256
VMEM_LIMIT_BYTES = 48 * 1024 * 1024

T_QA, T_KA, T_VA, T_QKB, T_VB, T_RB, T_QM, T_GATE = 0, 1, 2, 3, 4, 5, 6, 7
N_PTILES = 10
SUB_BLOCK = 8
NEG_BIG = -1e30


def _dot(a, b):
    return jnp.dot(a, b, preferred_element_type=F32)


def _dot_nt(a, b):
    return lax.dot_general(a, b, (((1,), (1,)), ((), ())), preferred_element_type=F32)


def _sigmoid(x):
    return 1.0 / (1.0 + jnp.exp(-x))


def _pick(n, pref):
    t = min(n, pref)
    while n % t:
        t -= 1
    return t


def _rms_rows(x, gain):
    ms = jnp.mean(x * x, axis=-1, keepdims=True)
    return x * lax.rsqrt(ms + EPS) * gain


def _group_rms(y, gmat, gain):
    slab = gmat.shape[0]
    outs = []
    for c in range(y.shape[-1] // slab):
        ys = y[:, c * slab:(c + 1) * slab]
        ms = _dot((ys * ys).astype(BF16), gmat)
        outs.append(ys * lax.rsqrt(ms + EPS))
    return jnp.concatenate(outs, axis=-1) * gain


def _group_matrix(slab, group):
    r = jnp.arange(slab) // group
    return jnp.where(r[:, None] == r[None, :], 1.0 / group, 0.0).astype(BF16)


def _inproj_kernel(x_ref, g_ref, wa_ref, wb_ref, wg1_ref, wg2_ref, bgk_ref, qn_ref, kn_ref, qmn_ref,
                   g64_ref, g256_ref, p_ref, ka_ref, vaf_ref, gk_ref, *, k_transposed):
    d = D_MODEL
    n_a = wa_ref.shape[1] // d
    h = _rms_rows(x_ref[...], g_ref[...]).astype(BF16)

    def project(j):
        if j < n_a:
            return _dot(h, wa_ref[:, j * d:(j + 1) * d])
        return _dot(h, wb_ref[:, (j - n_a) * d:(j - n_a + 1) * d])

    def finish(j, acc):
        if j == T_QA:
            out = _group_rms(acc, g64_ref[...], qn_ref[...]) * (DIFF_SCALE * LOG2E)
        elif j == T_KA:
            out = _group_rms(acc, g64_ref[...], kn_ref[...])
            ka_ref[...] = out.T if k_transposed else out
        elif j == T_VA:
            vaf_ref[...] = acc
            out = acc
        elif j == T_QKB:
            half = d // 2
            out = jnp.concatenate([acc[:, :half] * GLA_SCALE, acc[:, half:]], axis=-1)
            lr = _dot(h, wg1_ref[...])
            z = _dot(lr.astype(BF16), wg2_ref[...]) + bgk_ref[...]
            log_sig = jnp.minimum(z, 0.0) - jnp.log(1.0 + jnp.exp(-jnp.abs(z)))
            gk_ref[...] = log_sig * (1.0 / GK_NORM)
        elif j == T_VB:
            out = acc
        elif j == T_RB:
            out = acc * _sigmoid(acc)
        elif j == T_QM:
            out = _group_rms(acc, g256_ref[...], qmn_ref[...]) * MEM_SCALE
        else:
            out = _sigmoid(acc)
        p_ref[:, j * d:(j + 1) * d] = out.astype(BF16)

    acc_next = project(0)
    for j in range(N_PTILES):
        acc = acc_next
        if j + 1 < N_PTILES:
            acc_next = project(j + 1)
        finish(j, acc)


def _resident(shape):
    return pl.BlockSpec(shape, lambda *_: (0,) * len(shape), pipeline_mode=pl.Buffered(1))


def _inproj(x2, g_attn, wa, wb, wg1, wg2, bgk, qn_t, kn_t, qmn_t, g64, g256, tm, seq_len, k_transposed):
    t = x2.shape[0]
    d = D_MODEL
    row = lambda i: (i, 0)
    if k_transposed:
        per_seq = seq_len // tm
        ka_spec = pl.BlockSpec((None, d, tm), lambda i: (i // per_seq, 0, i % per_seq))
        ka_shape = jax.ShapeDtypeStruct((t // seq_len, d, seq_len), F32)
    else:
        ka_spec = pl.BlockSpec((tm, d), row)
        ka_shape = jax.ShapeDtypeStruct((t, d), F32)
    small = [wg1, wg2, bgk, qn_t, kn_t, qmn_t, g64, g256]
    return pl.pallas_call(
        functools.partial(_inproj_kernel, k_transposed=k_transposed),
        grid=(t // tm,),
        in_specs=[pl.BlockSpec((tm, d), row), _resident((1, d)), _resident(wa.shape), _resident(wb.shape)]
        + [_resident(a.shape) for a in small],
        out_specs=[
            pl.BlockSpec((tm, N_PTILES * d), row),
            ka_spec,
            pl.BlockSpec((tm, d), row),
            pl.BlockSpec((tm, GLA_HEADS * GLA_DK), row),
        ],
        out_shape=[
            jax.ShapeDtypeStruct((t, N_PTILES * d), BF16),
            ka_shape,
            jax.ShapeDtypeStruct((t, d), F32),
            jax.ShapeDtypeStruct((t, GLA_HEADS * GLA_DK), F32),
        ],
        compiler_params=pltpu.CompilerParams(
            dimension_semantics=("parallel",), vmem_limit_bytes=VMEM_LIMIT_BYTES),
        name="inproj",
    )(x2, g_attn, wa, wb, *small)


def _memkv_kernel(m_ref, g_ref, w_ref, kn_ref, g256_ref, k_ref, v_ref):
    j = pl.program_id(0)
    h = _rms_rows(m_ref[...], g_ref[...]).astype(BF16)
    acc = _dot(h, w_ref[...])

    @pl.when(j == 0)
    def _():
        k_ref[...] = _group_rms(acc, g256_ref[...], kn_ref[...])

    @pl.when(j == 1)
    def _():
        v_ref[...] = acc


def _memkv(mem2, g_mem, w_kv, knm_t, g256):
    t = mem2.shape[0]
    d = D_MODEL
    full = lambda j: (0, 0)
    return pl.pallas_call(
        _memkv_kernel,
        grid=(2,),
        in_specs=[
            pl.BlockSpec((t, d), full),
            pl.BlockSpec((1, d), full),
            pl.BlockSpec((d, d), lambda j: (0, j)),
            pl.BlockSpec((1, d), full),
            pl.BlockSpec(g256.shape, full),
        ],
        out_specs=[pl.BlockSpec((t, d), full), pl.BlockSpec((t, d), full)],
        out_shape=[jax.ShapeDtypeStruct((t, d), F32), jax.ShapeDtypeStruct((t, d), F32)],
        compiler_params=pltpu.CompilerParams(
            dimension_semantics=("arbitrary",), vmem_limit_bytes=VMEM_LIMIT_BYTES),
        name="memkv",
    )(mem2, g_mem, w_kv, knm_t, g256)


def _lambda_value(lamp, lam_init):
    a = jnp.sum(lamp[0:1, :] * lamp[1:2, :], axis=-1, keepdims=True)
    b = jnp.sum(lamp[2:3, :] * lamp[3:4, :], axis=-1, keepdims=True)
    return jnp.exp(a) - jnp.exp(b) + lam_init


def _stack_maps(q):
    lane = lax.broadcasted_iota(jnp.int32, q.shape, 1)
    zero = jnp.zeros_like(q)
    return jnp.concatenate([jnp.where(lane < DIFF_HD, q, zero), jnp.where(lane >= DIFF_HD, q, zero)], axis=0)


def _diff_finish(acc, l, lam, sub, lam_init, tq):
    o2 = acc / l
    o = o2[:tq] - lam * o2[tq:]
    return _rms_rows(o, sub) * (1.0 - lam_init)


ONES_ROWS = 16
LOOP_TRIPS = 4
XPOSE_ROWS = 512


def _diff_prompt_kernel(q_ref, qn_ref, k_ref, v_ref, lamp_ref, sub_ref, o_ref, qt_scr, vt_scr, m_scr, acc_scr,
                        st0_scr, st1_scr, p0_scr, p1_scr, alpha0_scr, alpha1_scr, mx0_scr, mx1_scr,
                        qtn_scr, stn_scr, mxn_scr, p2_scr, p3_scr, alpha2_scr, alpha3_scr,
                        *, tq, tk, lam_init):
    qi = pl.program_id(2)
    n_kv = v_ref.shape[0] // tk
    hd2 = v_ref.shape[1]
    halves = tq // tk
    assert halves % 2 == 0
    all_cgs = list(range(2 * halves))

    @pl.when(qi == 0)
    def _():
        for c in range(n_kv):
            vt_scr[c, 0:hd2, :] = v_ref[c * tk:(c + 1) * tk, :].astype(F32).T.astype(BF16)
            vt_scr[c, hd2:hd2 + ONES_ROWS, :] = jnp.ones((ONES_ROWS, tk), BF16)

    def transposed_queries(ref):
        parts = []
        for r0 in range(0, tq, XPOSE_ROWS):
            parts.append(_stack_maps(ref[r0:r0 + XPOSE_ROWS, :]).astype(F32).T.astype(BF16))
        return jnp.concatenate([p[:, :XPOSE_ROWS] for p in parts] + [p[:, XPOSE_ROWS:] for p in parts], axis=1)

    @pl.when(qi == 0)
    def _():
        qt_scr[...] = transposed_queries(q_ref)

    @pl.when(qi > 0)
    def _():
        qt_scr[...] = qtn_scr[...]

    m_scr[...] = jnp.full(m_scr.shape, -jnp.inf, F32)
    acc_scr[...] = jnp.zeros(acc_scr.shape, F32)

    def scores(kb, cgs, qt=qt_scr):
        k = k_ref[pl.ds(pl.multiple_of(kb * tk, tk), tk), :]
        sts = [_dot(k, qt[:, cg * tk:(cg + 1) * tk]) for cg in cgs]
        return sts, [jnp.max(st, axis=0, keepdims=True) for st in sts]

    def softmax(sts, maxes, slot, cgs, masked):
        pts, alphas = [], []
        for st, mx, cg, msk in zip(sts, maxes, cgs, masked):
            cols = slice(cg * tk, (cg + 1) * tk)
            if msk:
                r = lax.broadcasted_iota(jnp.int32, st.shape, 0)
                c = lax.broadcasted_iota(jnp.int32, st.shape, 1)
                st = jnp.where((r // CHUNK) <= (c // CHUNK), st, -jnp.inf)
                mx = jnp.max(st, axis=0, keepdims=True)
            m_prev = m_scr[slot, :, cols]
            m_new = jnp.maximum(m_prev, mx)
            alphas.append(jnp.exp2(m_prev - m_new))
            pts.append(jnp.exp2(st - m_new).astype(BF16))
            m_scr[slot, :, cols] = m_new
        return pts, alphas

    def accumulate(kb, slot, cgs, pts, alphas):
        vt = vt_scr[kb]
        for pt, alpha, cg in zip(pts, alphas, cgs):
            cols = slice(cg * tk, (cg + 1) * tk)
            acc_scr[slot, :, cols] = alpha * acc_scr[slot, :, cols] + _dot(vt, pt)

    st_scr, mx_scr = (st0_scr, st1_scr), (mx0_scr, mx1_scr)
    p_scr = (p0_scr, p1_scr, p2_scr, p3_scr)
    alpha_scr = (alpha0_scr, alpha1_scr, alpha2_scr, alpha3_scr)
    last = LOOP_TRIPS - 1

    def put(scr, buf, cgs, vals):
        for cg, val in zip(cgs, vals):
            scr[buf][cg] = val

    def get(scr, buf, cgs):
        return [scr[buf][cg] for cg in cgs]

    def put_scores(buf, cgs, sts_maxes):
        put(st_scr, buf, cgs, sts_maxes[0])
        put(mx_scr, buf, cgs, sts_maxes[1])

    def trip(t, u):
        par, prev = u % 2, (u - 1) % LOOP_TRIPS
        pts, alphas = softmax(get(st_scr, par, all_cgs), get(mx_scr, par, all_cgs), par, all_cgs,
                              [False] * len(all_cgs))
        put(p_scr, u, all_cgs, pts)
        put(alpha_scr, u, all_cgs, alphas)
        put_scores(1 - par, all_cgs, scores(jnp.minimum(t + 1, n_kv - 1), all_cgs))
        accumulate(jnp.maximum(t - 1, 0), 1 - par, all_cgs,
                   get(p_scr, prev, all_cgs), get(alpha_scr, prev, all_cgs))

    def loop_body(i, carry):
        for u in range(LOOP_TRIPS):
            trip(LOOP_TRIPS * i + u, u)
        return carry

    n_trips = qi * halves

    @pl.when(qi == 0)
    def _():
        put_scores(0, all_cgs, scores(0, all_cgs))

    @pl.when(qi > 0)
    def _():
        put_scores(0, all_cgs, ([stn_scr[cg] for cg in all_cgs], [mxn_scr[cg] for cg in all_cgs]))

    put(p_scr, last, all_cgs, [jnp.zeros((tk, tk), BF16)] * len(all_cgs))
    put(alpha_scr, last, all_cgs, [jnp.ones((1, tk), F32)] * len(all_cgs))
    assert halves % LOOP_TRIPS == 0
    lax.fori_loop(0, n_trips // LOOP_TRIPS, loop_body, 0)

    kb0 = n_trips

    def visible(j):
        return [cg for cg in all_cgs if cg % halves >= j]

    later = scores(kb0 + 1, visible(1))
    qtn_scr[...] = transposed_queries(qn_ref)
    nxt = scores(0, all_cgs, qt=qtn_scr)
    for cg in all_cgs:
        stn_scr[cg] = nxt[0][cg]
        mxn_scr[cg] = nxt[1][cg]
    accumulate(jnp.maximum(kb0 - 1, 0), 1, all_cgs, get(p_scr, last, all_cgs), get(alpha_scr, last, all_cgs))
    cur = (get(st_scr, 0, all_cgs), get(mx_scr, 0, all_cgs))
    for j in range(halves):
        cgs_j = visible(j)
        pts, alphas = softmax(cur[0], cur[1], j % 2, cgs_j, [cg % halves == j for cg in cgs_j])
        cur = later
        if j + 2 < halves:
            later = scores(kb0 + j + 2, visible(j + 2))
        accumulate(kb0 + j, j % 2, cgs_j, pts, alphas)

    m_all = jnp.maximum(m_scr[0], m_scr[1])
    acc_all = jnp.exp2(m_scr[0] - m_all) * acc_scr[0] + jnp.exp2(m_scr[1] - m_all) * acc_scr[1]

    lam = _lambda_value(lamp_ref[...], lam_init)
    o2t = acc_all[0:hd2, :] / acc_all[hd2:hd2 + 1, :]
    ot = o2t[:, :tq] - lam * o2t[:, tq:]
    for r0 in range(0, tq, XPOSE_ROWS):
        o = ot[:, r0:r0 + XPOSE_ROWS].T
        o_ref[r0:r0 + XPOSE_ROWS, :] = (_rms_rows(o, sub_ref[...]) * (1.0 - lam_init)).astype(o_ref.dtype)


def _diff_prompt(p3, lamp, sub, lam_init, tq, tk):
    b, s, _ = p3.shape
    hd2 = 2 * DIFF_HD
    kern = functools.partial(_diff_prompt_kernel, tq=tq, tk=tk, lam_init=lam_init)
    per_d = D_MODEL // hd2
    n_cg = 2 * tq // tk
    return pl.pallas_call(
        kern,
        grid=(b, DIFF_HEADS, s // tq),
        in_specs=[
            pl.BlockSpec((None, tq, hd2), lambda bi, h, qi: (bi, qi, T_QA * per_d + h)),
            pl.BlockSpec((None, tq, hd2), lambda bi, h, qi: (bi, jnp.minimum(qi + 1, s // tq - 1), T_QA * per_d + h)),
            pl.BlockSpec((None, s, hd2), lambda bi, h, qi: (bi, 0, T_KA * per_d + h)),
            pl.BlockSpec((None, s, hd2), lambda bi, h, qi: (bi, 0, T_VA * per_d + h)),
            pl.BlockSpec(lamp.shape, lambda bi, h, qi: (0, 0)),
            pl.BlockSpec(sub.shape, lambda bi, h, qi: (0, 0)),
        ],
        out_specs=pl.BlockSpec((None, tq, hd2), lambda bi, h, qi: (bi, qi, h)),
        out_shape=jax.ShapeDtypeStruct((b, s, D_MODEL), BF16),
        scratch_shapes=[
            pltpu.VMEM((hd2, 2 * tq), BF16),
            pltpu.VMEM((s // tk, hd2 + ONES_ROWS, tk), BF16),
            pltpu.VMEM((2, 1, 2 * tq), F32),
            pltpu.VMEM((2, hd2 + ONES_ROWS, 2 * tq), F32),
            pltpu.VMEM((n_cg, tk, tk), F32),
            pltpu.VMEM((n_cg, tk, tk), F32),
            pltpu.VMEM((n_cg, tk, tk), BF16),
            pltpu.VMEM((n_cg, tk, tk), BF16),
            pltpu.VMEM((n_cg, 1, tk), F32),
            pltpu.VMEM((n_cg, 1, tk), F32),
            pltpu.VMEM((n_cg, 1, tk), F32),
            pltpu.VMEM((n_cg, 1, tk), F32),
            pltpu.VMEM((hd2, 2 * tq), BF16),
            pltpu.VMEM((n_cg, tk, tk), F32),
            pltpu.VMEM((n_cg, 1, tk), F32),
            pltpu.VMEM((n_cg, tk, tk), BF16),
            pltpu.VMEM((n_cg, tk, tk), BF16),
            pltpu.VMEM((n_cg, 1, tk), F32),
            pltpu.VMEM((n_cg, 1, tk), F32),
        ],
        compiler_params=pltpu.CompilerParams(
            dimension_semantics=("parallel", "parallel", "arbitrary"), vmem_limit_bytes=VMEM_LIMIT_BYTES),
        name="diff_attn_prompt",
    )(p3, p3, p3, p3, lamp, sub)


SAMPLE_HEAD_GROUP = 4


def _diff_sample_kernel(q_ref, kn_ref, vn_ref, kc_ref, vc_ref, lamp_ref, sub_ref, o_ref, *, lam_init):
    tq = q_ref.shape[0]
    hd2 = 2 * DIFF_HD
    n_past = kc_ref.shape[1]
    lam = _lambda_value(lamp_ref[...], lam_init)
    for h0 in range(0, DIFF_HEADS, SAMPLE_HEAD_GROUP):
        heads = range(h0, h0 + SAMPLE_HEAD_GROUP)
        cols = [slice(h * hd2, (h + 1) * hd2) for h in heads]
        q2s = [_stack_maps(q_ref[:, c]) for c in cols]
        s_cs = [_dot(q2, kc_ref[c, :].astype(BF16)) for q2, c in zip(q2s, cols)]
        s_ns = [_dot_nt(q2, kn_ref[:, c]) for q2, c in zip(q2s, cols)]
        pcs, pns, ls = [], [], []
        for s_c, s_n in zip(s_cs, s_ns):
            m = jnp.maximum(jnp.max(s_c, axis=-1, keepdims=True), jnp.max(s_n, axis=-1, keepdims=True))
            p_c = jnp.exp2(s_c - m)
            p_n = jnp.exp2(s_n - m)
            ls.append(jnp.sum(p_c, axis=-1, keepdims=True) + jnp.sum(p_n, axis=-1, keepdims=True))
            pcs.append(p_c.astype(BF16))
            pns.append(p_n.astype(BF16))
        for h, c, p_c, p_n, l in zip(heads, cols, pcs, pns, ls):
            vc = vc_ref[pl.ds(h, n_past, stride=DIFF_HEADS), :]
            acc = _dot(p_c, vc.astype(BF16)) + _dot(p_n, vn_ref[:, c])
            o_ref[:, c] = _diff_finish(acc, l, lam, sub_ref[...], lam_init, tq).astype(o_ref.dtype)


def _diff_sample(p3, kc_t, vc, lamp, sub, lam_init):
    b, l, _ = p3.shape
    n_past = kc_t.shape[2]
    hd2 = 2 * DIFF_HD
    d = D_MODEL
    kern = functools.partial(_diff_sample_kernel, lam_init=lam_init)
    return pl.pallas_call(
        kern,
        grid=(b,),
        in_specs=[
            pl.BlockSpec((None, l, d), lambda bi: (bi, 0, T_QA)),
            pl.BlockSpec((None, l, d), lambda bi: (bi, 0, T_KA)),
            pl.BlockSpec((None, l, d), lambda bi: (bi, 0, T_VA)),
            pl.BlockSpec((None, d, n_past), lambda bi: (bi, 0, 0)),
            pl.BlockSpec((None, n_past * DIFF_HEADS, hd2), lambda bi: (bi, 0, 0)),
            pl.BlockSpec(lamp.shape, lambda bi: (0, 0)),
            pl.BlockSpec(sub.shape, lambda bi: (0, 0)),
        ],
        out_specs=pl.BlockSpec((None, l, d), lambda bi: (bi, 0, 0)),
        out_shape=jax.ShapeDtypeStruct((b, l, D_MODEL), BF16),
        compiler_params=pltpu.CompilerParams(
            dimension_semantics=("parallel",), vmem_limit_bytes=VMEM_LIMIT_BYTES),
        name="diff_attn_sample",
    )(p3, p3, p3, kc_t, vc, lamp, sub)


def _bcast_rows(x, period, row):
    r, c = x.shape
    x3 = x.reshape(r // period, period, c)
    return jnp.broadcast_to(x3[:, row:row + 1, :], x3.shape).reshape(r, c)


def _gla_kernel(*refs, rows, has_init):
    if has_init:
        q_ref, k_ref, v_ref, r_ref, g_ref, sub_ref, s0_ref, o_ref, sout_ref, st_scr, kf_scr, b_scr = refs
    else:
        q_ref, k_ref, v_ref, r_ref, g_ref, sub_ref, o_ref, sout_ref, st_scr, kf_scr, b_scr = refs
        s0_ref = None
    step = pl.program_id(2)
    n_chunks = rows // CHUNK
    n_sub = CHUNK // SUB_BLOCK

    @pl.when(step == 0)
    def _():
        if has_init:
            st_scr[...] = s0_ref[...]
        else:
            st_scr[...] = jnp.zeros(st_scr.shape, F32)

    q = q_ref[...].astype(F32)
    k = k_ref[...].astype(F32)
    g = g_ref[...]
    v = v_ref[...]
    chunk_rows = [slice(c * CHUNK, (c + 1) * CHUNK) for c in range(n_chunks)]

    ri = lax.broadcasted_iota(jnp.int32, (CHUNK, CHUNK), 0)
    ci = lax.broadcasted_iota(jnp.int32, (CHUNK, CHUNK), 1)
    tri = jnp.where(ci <= ri, 1.0, 0.0).astype(BF16)
    gw = jnp.concatenate([g[sl] for sl in chunk_rows], axis=1)
    g1 = gw.astype(BF16)
    rem = gw - g1.astype(F32)
    g2 = rem.astype(BF16)
    g3 = (rem - g2.astype(F32)).astype(BF16)
    bw = _dot(tri, g1) + _dot(tri, g2) + _dot(tri, g3)
    b = jnp.concatenate([bw[:, c * GLA_DK:(c + 1) * GLA_DK] for c in range(n_chunks)], axis=0)
    bex = b - g

    b_last = _bcast_rows(b, CHUNK, CHUNK - 1)
    b_blk = _bcast_rows(bex, SUB_BLOCK, 0)
    q_blk = q * jnp.exp(b - b_blk)
    q_chk = q * jnp.exp(b)
    k_end = k * jnp.exp(b_last - b)

    rowc = lax.broadcasted_iota(jnp.int32, (rows, GLA_DK), 0) % CHUNK
    zero = jnp.zeros_like(q)

    lhs_parts, rhs_parts = [], []
    for blk in range(1, n_sub):
        b_ref_blk = _bcast_rows(bex, CHUNK, blk * SUB_BLOCK)
        k_blk = k * jnp.exp(jnp.where(rowc < blk * SUB_BLOCK, b_ref_blk - b, NEG_BIG))
        lhs_parts.append(jnp.where((rowc // SUB_BLOCK) == blk, q_blk, zero))
        rhs_parts.append(k_blk)
    lhs = jnp.concatenate(lhs_parts, axis=-1).astype(BF16)
    rhs = jnp.concatenate(rhs_parts, axis=-1).astype(BF16)

    pad = SUB_BLOCK
    kf_scr[0:pad, :] = jnp.zeros((pad, GLA_DK), F32)
    b_scr[0:pad, :] = jnp.zeros((pad, GLA_DK), F32)
    kf_scr[pad:pad + rows, :] = k
    b_scr[pad:pad + rows, :] = b
    row_sub = rowc % SUB_BLOCK
    terms = []
    for d in range(SUB_BLOCK):
        kd = kf_scr[pad - d:pad - d + rows, :]
        bd = b_scr[pad - d:pad - d + rows, :]
        e = jnp.exp(jnp.where(row_sub >= d, b - bd, NEG_BIG))
        terms.append((q * kd * e).astype(BF16))
    sel_r = lax.broadcasted_iota(jnp.int32, (SUB_BLOCK * GLA_DK, LANES), 0) // GLA_DK
    sel_c = lax.broadcasted_iota(jnp.int32, (SUB_BLOCK * GLA_DK, LANES), 1)
    selector = jnp.where(sel_r + sel_c == SUB_BLOCK - 1, 1.0, 0.0).astype(BF16)
    diag_sums = _dot(jnp.concatenate(terms, axis=-1), selector)

    def skew(x):
        return pltpu.roll(x, LANES - (SUB_BLOCK - 1), 1, stride=1, stride_axis=0)

    atts = [(_dot_nt(lhs[sl], rhs[sl]) + skew(diag_sums[sl])[:, :CHUNK]).astype(BF16) for sl in chunk_rows]
    d_sts = [_dot(k_end[sl].T.astype(BF16), v[sl]) for sl in chunk_rows]
    o_intra = [_dot(att, v[sl]) for att, sl in zip(atts, chunk_rows)]
    decs = [jnp.exp(b[sl].T[:, CHUNK - 1:CHUNK]) for sl in chunk_rows]
    st = st_scr[...]
    outs = []
    for c, sl in enumerate(chunk_rows):
        outs.append(o_intra[c] + _dot(q_chk[sl].astype(BF16), st.astype(BF16)))
        st = st * decs[c] + d_sts[c]
    st_scr[...] = st
    o = jnp.concatenate(outs, axis=0) if n_chunks > 1 else outs[0]
    o_ref[...] = (_rms_rows(o, sub_ref[...]) * r_ref[...].astype(F32)).astype(o_ref.dtype)

    @pl.when(step == pl.num_programs(2) - 1)
    def _():
        sout_ref[...] = st_scr[...]


def _gla(p3, gk3, sub, s0, rows):
    b, s, _ = p3.shape
    has_init = s0 is not None
    kq = D_MODEL // GLA_DK
    kv = D_MODEL // GLA_DV
    in_specs = [
        pl.BlockSpec((None, rows, GLA_DK), lambda bi, h, r: (bi, r, T_QKB * kq + h)),
        pl.BlockSpec((None, rows, GLA_DK), lambda bi, h, r: (bi, r, T_QKB * kq + GLA_HEADS + h)),
        pl.BlockSpec((None, rows, GLA_DV), lambda bi, h, r: (bi, r, T_VB * kv + h)),
        pl.BlockSpec((None, rows, GLA_DV), lambda bi, h, r: (bi, r, T_RB * kv + h)),
        pl.BlockSpec((None, rows, GLA_DK), lambda bi, h, r: (bi, r, h)),
        pl.BlockSpec(sub.shape, lambda bi, h, r: (0, 0)),
    ]
    args = [p3, p3, p3, p3, gk3, sub]
    if has_init:
        in_specs.append(pl.BlockSpec((None, None, GLA_DK, GLA_DV), lambda bi, h, r: (bi, h, 0, 0)))
        args.append(s0)
    kern = functools.partial(_gla_kernel, rows=rows, has_init=has_init)
    return pl.pallas_call(
        kern,
        grid=(b, GLA_HEADS, s // rows),
        in_specs=in_specs,
        out_specs=[
            pl.BlockSpec((None, rows, GLA_DV), lambda bi, h, r: (bi, r, h)),
            pl.BlockSpec((None, None, GLA_DK, GLA_DV), lambda bi, h, r: (bi, h, 0, 0)),
        ],
        out_shape=[
            jax.ShapeDtypeStruct((b, s, GLA_HEADS * GLA_DV), BF16),
            jax.ShapeDtypeStruct((b, GLA_HEADS, GLA_DK, GLA_DV), F32),
        ],
        scratch_shapes=[
            pltpu.VMEM((GLA_DK, GLA_DV), F32),
            pltpu.VMEM((SUB_BLOCK + rows, GLA_DK), F32),
            pltpu.VMEM((SUB_BLOCK + rows, GLA_DK), F32),
        ],
        compiler_params=pltpu.CompilerParams(
            dimension_semantics=("parallel", "parallel", "arbitrary"), vmem_limit_bytes=VMEM_LIMIT_BYTES),
        name="gla",
    )(*args)


def _mem_attn_kernel(q_ref, k_ref, v_ref, o_ref, *, tiled):
    def head(ref, h):
        if not tiled:
            return ref[:, h * MEM_HD:(h + 1) * MEM_HD]
        period = MEM_HEADS * MEM_HD // LANES
        n_tok = ref.shape[0] // period
        return jnp.concatenate([ref[pl.ds(half * MEM_HEADS + h, n_tok, stride=period), :]
                                for half in range(MEM_HD // LANES)], axis=1)

    heads = range(MEM_HEADS)
    scores = [_dot_nt(q_ref[:, h * MEM_HD:(h + 1) * MEM_HD], head(k_ref, h).astype(BF16)) for h in heads]
    probs, sums = [], []
    for s in scores:
        p = jnp.exp(s - jnp.max(s, axis=-1, keepdims=True))
        sums.append(jnp.sum(p, axis=-1, keepdims=True))
        probs.append(p.astype(BF16))
    outs = [_dot(p, head(v_ref, h).astype(BF16)) / l for h, p, l in zip(heads, probs, sums)]
    o_ref[...] = jnp.concatenate(outs, axis=-1).astype(o_ref.dtype)


def _mem_attn(p3, mk, mv, tq):
    b, s, _ = p3.shape
    d = D_MODEL
    kv_block = (None,) + mk.shape[1:]
    return pl.pallas_call(
        functools.partial(_mem_attn_kernel, tiled=mk.shape[2] == LANES),
        grid=(b, s // tq),
        in_specs=[
            pl.BlockSpec((None, tq, d), lambda bi, i: (bi, i, T_QM)),
            pl.BlockSpec(kv_block, lambda bi, i: (bi, 0, 0)),
            pl.BlockSpec(kv_block, lambda bi, i: (bi, 0, 0)),
        ],
        out_specs=pl.BlockSpec((None, tq, d), lambda bi, i: (bi, i, 0)),
        out_shape=jax.ShapeDtypeStruct((b, s, d), BF16),
        compiler_params=pltpu.CompilerParams(
            dimension_semantics=("parallel", "parallel"), vmem_limit_bytes=VMEM_LIMIT_BYTES),
        name="mem_attn",
    )(p3, mk, mv)


def _mix_kernel(x_ref, oa_ref, ob_ref, om_ref, ga_ref, gb_ref, gm_ref, wd_ref, wg_ref, wm_ref, wo_ref,
                y_ref):
    m = (ga_ref[...].astype(F32) * _dot(oa_ref[...], wd_ref[...])
         + gb_ref[...].astype(F32) * _dot(ob_ref[...], wg_ref[...])
         + gm_ref[...].astype(F32) * _dot(om_ref[...], wm_ref[...]))
    y_ref[...] = x_ref[...] + _dot(m.astype(BF16), wo_ref[...])


def _mix(x2, oa, ob, om, p2, wd, wg, wm, wo, tm):
    t = x2.shape[0]
    d = D_MODEL
    row = lambda i: (i, 0)
    full = lambda i: (0, 0)
    wspec = pl.BlockSpec((d, d), full, pipeline_mode=pl.Buffered(1))
    return pl.pallas_call(
        _mix_kernel,
        grid=(t // tm,),
        in_specs=[
            pl.BlockSpec((tm, d), row), pl.BlockSpec((tm, d), row), pl.BlockSpec((tm, d), row),
            pl.BlockSpec((tm, d), row),
            pl.BlockSpec((tm, d), lambda i: (i, T_GATE)),
            pl.BlockSpec((tm, d), lambda i: (i, T_GATE + 1)),
            pl.BlockSpec((tm, d), lambda i: (i, T_GATE + 2)),
            wspec, wspec, wspec, wspec,
        ],
        out_specs=pl.BlockSpec((tm, d), row),
        out_shape=jax.ShapeDtypeStruct((t, d), F32),
        compiler_params=pltpu.CompilerParams(
            dimension_semantics=("parallel",), vmem_limit_bytes=VMEM_LIMIT_BYTES),
        name="mix_out",
    )(x2, oa, ob, om, p2, p2, p2, wd, wg, wm, wo)


FFN_CHUNK = 256


def _ffn_kernel(x_ref, g_ref, wup_ref, cw_ref, cb_ref, wd_ref, cs_ref, y_ref, cso_ref,
                u_scr, carry_scr, gv_scr, *, n_seq, seq_rows, tiles_per_seq):
    i = pl.program_id(0)
    gap = SUBLANES
    stride = seq_rows + gap
    tail = CONV_W - 1
    n_chunks = D_FF // FFN_CHUNK
    x = x_ref[...]
    h = _rms_rows(x, g_ref[...]).astype(BF16)

    first = (i % tiles_per_seq) == 0
    for s in range(n_seq):
        base = s * stride

        @pl.when(first)
        def _():
            u_scr[base:base + gap, :] = jnp.zeros((gap, D_FF), F32)
            u_scr[base + gap - tail:base + gap, :] = cs_ref[s]

        @pl.when(jnp.logical_not(first))
        def _():
            u_scr[base:base + gap, :] = carry_scr[...]

    def up(c):
        cols = slice(c * FFN_CHUNK, (c + 1) * FFN_CHUNK)
        gate_cols = slice(D_FF + c * FFN_CHUNK, D_FF + (c + 1) * FFN_CHUNK)
        return _dot(h, wup_ref[:, cols]), _dot(h, wup_ref[:, gate_cols])

    def gated(c, u, vv):
        cols = slice(c * FFN_CHUNK, (c + 1) * FFN_CHUNK)
        cw = cw_ref[:, cols]
        outs = []
        for s in range(n_seq):
            base = s * stride + gap
            u_scr[base:base + seq_rows, cols] = u[s * seq_rows:(s + 1) * seq_rows]
            conv = cb_ref[:, cols]
            for j in range(CONV_W):
                off = base - tail + j
                conv = conv + cw[j:j + 1, :] * u_scr[off:off + seq_rows, cols]
            outs.append(conv)
        uc = jnp.concatenate(outs, axis=0) if n_seq > 1 else outs[0]
        gelu = 0.5 * uc * (1.0 + jnp.tanh(math.sqrt(2.0 / math.pi) * (uc + 0.044715 * (uc * uc * uc))))
        return (gelu * vv).astype(BF16)

    nxt = up(0)
    for c in range(n_chunks):
        cur = nxt
        if c + 1 < n_chunks:
            nxt = up(c + 1)
        gv_scr[:, c * FFN_CHUNK:(c + 1) * FFN_CHUNK] = gated(c, *cur)
    y_ref[...] = x + _dot(gv_scr[...], wd_ref[...])

    for s in range(n_seq):
        base = s * stride + gap
        cso_ref[s] = u_scr[base + seq_rows - tail:base + seq_rows, :]
    carry_scr[...] = u_scr[seq_rows:seq_rows + gap, :]


def _ffn(x2, g_ffn, w_up, conv_w, conv_b, w_down, conv_state, n_seq, seq_rows, tiles_per_seq):
    t = x2.shape[0]
    d = D_MODEL
    tm = n_seq * seq_rows
    nb = conv_state.shape[0]
    tail = CONV_W - 1
    kern = functools.partial(_ffn_kernel, n_seq=n_seq, seq_rows=seq_rows, tiles_per_seq=tiles_per_seq)
    y, tails = pl.pallas_call(
        kern,
        grid=(t // tm,),
        in_specs=[
            pl.BlockSpec((tm, d), lambda i: (i, 0)),
            _resident((1, d)),
            _resident(w_up.shape),
            _resident(conv_w.shape),
            _resident(conv_b.shape),
            _resident(w_down.shape),
            pl.BlockSpec((n_seq, tail, D_FF), lambda i: (i // tiles_per_seq, 0, 0)),
        ],
        out_specs=[
            pl.BlockSpec((tm, d), lambda i: (i, 0)),
            pl.BlockSpec((n_seq, tail, D_FF), lambda i: (i, 0, 0)),
        ],
        out_shape=[
            jax.ShapeDtypeStruct((t, d), F32),
            jax.ShapeDtypeStruct((nb * tiles_per_seq, tail, D_FF), F32),
        ],
        scratch_shapes=[
            pltpu.VMEM((n_seq * (seq_rows + SUBLANES), D_FF), F32),
            pltpu.VMEM((SUBLANES, D_FF), F32),
            pltpu.VMEM((tm, D_FF), BF16),
        ],
        compiler_params=pltpu.CompilerParams(
            dimension_semantics=("arbitrary",), vmem_limit_bytes=VMEM_LIMIT_BYTES),
        name="conv_ffn",
    )(x2, g_ffn, w_up, conv_w, conv_b, w_down, conv_state)
    return y, tails.reshape(nb, tiles_per_seq, tail, D_FF)[:, -1]


def _mem_cache_rows(c):
    b, n, h, hd = c.shape
    return c.reshape(b, n, h, hd // LANES, LANES).transpose(0, 1, 3, 2, 4).reshape(b, -1, LANES)


def _tile_gain(g, reps):
    return jnp.tile(g.astype(F32), reps).reshape(1, -1)


def _layer_weights(l, g_attn, w_in, w_gk2, b_gk, qn_diff, kn_diff, lam_q1, lam_k1, lam_q2, lam_k2,
                   subln_diff, subln_gla, g_mem, w_mem_kv, qn_mem, kn_mem, w_proj_diff, w_proj_gla,
                   w_proj_mem, w_out, g_ffn, w_up, conv_w, conv_b, w_down):
    d = D_MODEL
    w = w_in[l]
    lr0 = 6 * d
    wa = w[:, :lr0].astype(BF16)
    wb = w[:, lr0 + GK_RANK:].astype(BF16)
    wg1 = jnp.pad(w[:, lr0:lr0 + GK_RANK], ((0, 0), (0, LANES - GK_RANK))).astype(BF16)
    wg2 = jnp.pad(w_gk2[l], ((0, LANES - GK_RANK), (0, 0))).astype(BF16)
    return dict(
        g_attn=g_attn[l].reshape(1, d), wa=wa, wb=wb, wg1=wg1, wg2=wg2, bgk=b_gk[l].reshape(1, -1),
        qn_t=_tile_gain(qn_diff[l], d // DIFF_HD), kn_t=_tile_gain(kn_diff[l], d // DIFF_HD),
        qmn_t=_tile_gain(qn_mem[l], d // MEM_HD), knm_t=_tile_gain(kn_mem[l], d // MEM_HD),
        g64=_group_matrix(MXU_DIM, DIFF_HD), g256=_group_matrix(MEM_HD, MEM_HD),
        lamp=jnp.stack([lam_q1[l], lam_k1[l], lam_q2[l], lam_k2[l]]).astype(F32),
        sub_diff=subln_diff[l].reshape(1, -1), sub_gla=subln_gla[l].reshape(1, -1),
        g_mem=g_mem[l].reshape(1, d), w_mem_kv=w_mem_kv[l].astype(BF16),
        wd=w_proj_diff[l].astype(BF16), wg=w_proj_gla[l].astype(BF16), wm=w_proj_mem[l].astype(BF16),
        wo=w_out[l].astype(BF16), g_ffn=g_ffn[l].reshape(1, d), w_up=w_up[l].astype(BF16),
        conv_w=conv_w[l], conv_b=conv_b[l].reshape(1, -1), w_down=w_down[l].astype(BF16),
    )


def _group(x, wts, lam_init, mem_k, mem_v, past_k, past_v, gla_state, conv_state, prompt):
    b, s, d = x.shape
    t = b * s
    x2 = x.reshape(t, d)
    tm = _pick(t, 512)
    p2, ka, va, gk = _inproj(x2, wts["g_attn"], wts["wa"], wts["wb"], wts["wg1"], wts["wg2"], wts["bgk"],
                             wts["qn_t"], wts["kn_t"], wts["qmn_t"], wts["g64"], wts["g256"],
                             _pick(s, 256) if prompt else _pick(t, 256), s, prompt)
    if prompt:
        ka = ka.reshape(b, DIFF_HEADS, 2, DIFF_HD, s).transpose(0, 4, 1, 2, 3)
    else:
        ka = ka.reshape(b, s, DIFF_HEADS, 2, DIFF_HD)
    p3 = p2.reshape(b, s, N_PTILES * d)
    gk3 = gk.reshape(b, s, GLA_HEADS * GLA_DK)
    if prompt:
        oa = _diff_prompt(p3, wts["lamp"], wts["sub_diff"], lam_init, _pick(s, 1024), 256)
        ob, gla_new = _gla(p3, gk3, wts["sub_gla"], None, _pick(s, 512))
    else:
        oa = _diff_sample(p3, past_k, past_v, wts["lamp"], wts["sub_diff"], lam_init)
        ob, gla_new = _gla(p3, gk3, wts["sub_gla"], gla_state, s)
    om = _mem_attn(p3, mem_k, mem_v, _pick(s, 512))
    x1 = _mix(x2, oa.reshape(t, d), ob.reshape(t, d), om.reshape(t, d), p2,
              wts["wd"], wts["wg"], wts["wm"], wts["wo"], tm)
    if prompt:
        rows = _pick(s, 512)
        y, cs = _ffn(x1, wts["g_ffn"], wts["w_up"], wts["conv_w"], wts["conv_b"], wts["w_down"],
                     conv_state, 1, rows, s // rows)
    else:
        y, cs = _ffn(x1, wts["g_ffn"], wts["w_up"], wts["conv_w"], wts["conv_b"], wts["w_down"],
                     conv_state, b, s, 1)
    return y.reshape(b, s, d), ka, va, gla_new, cs


def kernel(x_prompt, x_sample, mem_prompt, cache_diff_k, cache_diff_v, cache_mem_k, cache_mem_v, state_gla, state_conv, g_attn, w_in, w_gk2, b_gk, qn_diff, kn_diff, lam_q1, lam_k1, lam_q2, lam_k2, subln_diff, subln_gla, g_mem, w_mem_kv, qn_mem, kn_mem, w_proj_diff, w_proj_gla, w_proj_mem, w_out, g_ffn, w_up, conv_w, conv_b, w_down):
    depth = g_attn.shape[0]
    d = D_MODEL
    xp, xs = x_prompt, x_sample
    bp, sp, _ = xp.shape
    bs, ss, _ = xs.shape
    n_mem = mem_prompt.shape[1]
    outs = [[] for _ in range(10)]
    for l in range(depth):
        lam_init = 0.8 - 0.6 * math.exp(-0.3 * l)
        wts = _layer_weights(l, g_attn, w_in, w_gk2, b_gk, qn_diff, kn_diff, lam_q1, lam_k1, lam_q2,
                             lam_k2, subln_diff, subln_gla, g_mem, w_mem_kv, qn_mem, kn_mem,
                             w_proj_diff, w_proj_gla, w_proj_mem, w_out, g_ffn, w_up, conv_w, conv_b,
                             w_down)
        mk, mv = _memkv(mem_prompt.reshape(bp * n_mem, d), wts["g_mem"], wts["w_mem_kv"], wts["knm_t"],
                        wts["g256"])
        mk = mk.reshape(bp, n_mem, d)
        mv = mv.reshape(bp, n_mem, d)
        xp, kp, vp, gp, cp = _group(xp, wts, lam_init, mk, mv, None, None, None,
                                    jnp.zeros((bp, CONV_W - 1, D_FF), F32), True)
        xs, ks_, vs_, gs, cs = _group(
            xs, wts, lam_init, _mem_cache_rows(cache_mem_k[l]), _mem_cache_rows(cache_mem_v[l]),
            cache_diff_k[l].transpose(0, 2, 3, 4, 1).reshape(bs, d, -1), cache_diff_v[l].reshape(bs, -1, 2 * DIFF_HD),
            state_gla[l], state_conv[l], False)
        vals = (kp, vp.reshape(bp, sp, DIFF_HEADS, 2 * DIFF_HD),
                mk.reshape(bp, n_mem, MEM_HEADS, MEM_HD), mv.reshape(bp, n_mem, MEM_HEADS, MEM_HD), gp, cp,
                ks_, vs_.reshape(bs, ss, DIFF_HEADS, 2 * DIFF_HD), gs, cs)
        for o, v in zip(outs, vals):
            o.append(v)
    return (xp, xs) + tuple(jnp.stack(o) for o in outs)
```

```python
import functools
import math

import jax
import jax.numpy as jnp
from jax import lax
from jax.experimental import pallas as pl
from jax.experimental.pallas import tpu as pltpu

F32 = jnp.float32
BF16 = jnp.bfloat16

D_MODEL = 1024
CHUNK = 64
EPS = 1e-6
DIFF_HEADS = 8
DIFF_HD = 64
DIFF_SCALE = DIFF_HD ** -0.5
LOG2E = math.log2(math.e)
GLA_HEADS = 4
GLA_DK = 128
GLA_DV = 256
GLA_SCALE = GLA_DK ** -0.5
GK_RANK = 16
GK_NORM = 16.0
MEM_HEADS = 4
MEM_HD = 256
MEM_SCALE = MEM_HD ** -0.5
D_FF = 2816
CONV_W = 3

LANES = 128
SUBLANES = 8
MXU_DIM = 256
VMEM_LIMIT_BYTES = 48 * 1024 * 1024

T_QA, T_KA, T_VA, T_QKB, T_VB, T_RB, T_QM, T_GATE = 0, 1, 2, 3, 4, 5, 6, 7
N_PTILES = 10
SUB_BLOCK = 8
NEG_BIG = -1e30


def _dot(a, b):
    return jnp.dot(a, b, preferred_element_type=F32)


def _dot_nt(a, b):
    return lax.dot_general(a, b, (((1,), (1,)), ((), ())), preferred_element_type=F32)


def _sigmoid(x):
    return 1.0 / (1.0 + jnp.exp(-x))


def _pick(n, pref):
    t = min(n, pref)
    while n % t:
        t -= 1
    return t


def _rms_rows(x, gain):
    ms = jnp.mean(x * x, axis=-1, keepdims=True)
    return x * lax.rsqrt(ms + EPS) * gain


def _group_rms(y, gmat, gain):
    slab = gmat.shape[0]
    outs = []
    for c in range(y.shape[-1] // slab):
        ys = y[:, c * slab:(c + 1) * slab]
        ms = _dot((ys * ys).astype(BF16), gmat)
        outs.append(ys * lax.rsqrt(ms + EPS))
    return jnp.concatenate(outs, axis=-1) * gain


def _group_matrix(slab, group):
    r = jnp.arange(slab) // group
    return jnp.where(r[:, None] == r[None, :], 1.0 / group, 0.0).astype(BF16)


def _inproj_kernel(x_ref, g_ref, wa_ref, wb_ref, wg1_ref, wg2_ref, bgk_ref, qn_ref, kn_ref, qmn_ref,
                   g64_ref, g256_ref, p_ref, ka_ref, vaf_ref, gk_ref, *, k_transposed):
    d = D_MODEL
    n_a = wa_ref.shape[1] // d
    h = _rms_rows(x_ref[...], g_ref[...]).astype(BF16)

    def project(j):
        if j < n_a:
            return _dot(h, wa_ref[:, j * d:(j + 1) * d])
        return _dot(h, wb_ref[:, (j - n_a) * d:(j - n_a + 1) * d])

    def finish(j, acc):
        if j == T_QA:
            out = _group_rms(acc, g64_ref[...], qn_ref[...]) * (DIFF_SCALE * LOG2E)
        elif j == T_KA:
            out = _group_rms(acc, g64_ref[...], kn_ref[...])
            ka_ref[...] = out.T if k_transposed else out
        elif j == T_VA:
            vaf_ref[...] = acc
            out = acc
        elif j == T_QKB:
            half = d // 2
            out = jnp.concatenate([acc[:, :half] * GLA_SCALE, acc[:, half:]], axis=-1)
            lr = _dot(h, wg1_ref[...])
            z = _dot(lr.astype(BF16), wg2_ref[...]) + bgk_ref[...]
            log_sig = jnp.minimum(z, 0.0) - jnp.log(1.0 + jnp.exp(-jnp.abs(z)))
            gk_ref[...] = log_sig * (1.0 / GK_NORM)
        elif j == T_VB:
            out = acc
        elif j == T_RB:
            out = acc * _sigmoid(acc)
        elif j == T_QM:
            out = _group_rms(acc, g256_ref[...], qmn_ref[...]) * MEM_SCALE
        else:
            out = _sigmoid(acc)
        p_ref[:, j * d:(j + 1) * d] = out.astype(BF16)

    acc_next = project(0)
    for j in range(N_PTILES):
        acc = acc_next
        if j + 1 < N_PTILES:
            acc_next = project(j + 1)
        finish(j, acc)


def _resident(shape):
    return pl.BlockSpec(shape, lambda *_: (0,) * len(shape), pipeline_mode=pl.Buffered(1))


def _inproj(x2, g_attn, wa, wb, wg1, wg2, bgk, qn_t, kn_t, qmn_t, g64, g256, tm, seq_len, k_transposed):
    t = x2.shape[0]
    d = D_MODEL
    row = lambda i: (i, 0)
    if k_transposed:
        per_seq = seq_len // tm
        ka_spec = pl.BlockSpec((None, d, tm), lambda i: (i // per_seq, 0, i % per_seq))
        ka_shape = jax.ShapeDtypeStruct((t // seq_len, d, seq_len), F32)
    else:
        ka_spec = pl.BlockSpec((tm, d), row)
        ka_shape = jax.ShapeDtypeStruct((t, d), F32)
    small = [wg1, wg2, bgk, qn_t, kn_t, qmn_t, g64, g256]
    return pl.pallas_call(
        functools.partial(_inproj_kernel, k_transposed=k_transposed),
        grid=(t // tm,),
        in_specs=[pl.BlockSpec((tm, d), row), _resident((1, d)), _resident(wa.shape), _resident(wb.shape)]
        + [_resident(a.shape) for a in small],
        out_specs=[
            pl.BlockSpec((tm, N_PTILES * d), row),
            ka_spec,
            pl.BlockSpec((tm, d), row),
            pl.BlockSpec((tm, GLA_HEADS * GLA_DK), row),
        ],
        out_shape=[
            jax.ShapeDtypeStruct((t, N_PTILES * d), BF16),
            ka_shape,
            jax.ShapeDtypeStruct((t, d), F32),
            jax.ShapeDtypeStruct((t, GLA_HEADS * GLA_DK), F32),
        ],
        compiler_params=pltpu.CompilerParams(
            dimension_semantics=("parallel",), vmem_limit_bytes=VMEM_LIMIT_BYTES),
        name="inproj",
    )(x2, g_attn, wa, wb, *small)


def _memkv_kernel(m_ref, g_ref, w_ref, kn_ref, g256_ref, k_ref, v_ref):
    j = pl.program_id(0)
    h = _rms_rows(m_ref[...], g_ref[...]).astype(BF16)
    acc = _dot(h, w_ref[...])

    @pl.when(j == 0)
    def _():
        k_ref[...] = _group_rms(acc, g256_ref[...], kn_ref[...])

    @pl.when(j == 1)
    def _():
        v_ref[...] = acc


def _memkv(mem2, g_mem, w_kv, knm_t, g256):
    t = mem2.shape[0]
    d = D_MODEL
    full = lambda j: (0, 0)
    return pl.pallas_call(
        _memkv_kernel,
        grid=(2,),
        in_specs=[
            pl.BlockSpec((t, d), full),
            pl.BlockSpec((1, d), full),
            pl.BlockSpec((d, d), lambda j: (0, j)),
            pl.BlockSpec((1, d), full),
            pl.BlockSpec(g256.shape, full),
        ],
        out_specs=[pl.BlockSpec((t, d), full), pl.BlockSpec((t, d), full)],
        out_shape=[jax.ShapeDtypeStruct((t, d), F32), jax.ShapeDtypeStruct((t, d), F32)],
        compiler_params=pltpu.CompilerParams(
            dimension_semantics=("arbitrary",), vmem_limit_bytes=VMEM_LIMIT_BYTES),
        name="memkv",
    )(mem2, g_mem, w_kv, knm_t, g256)


def _lambda_value(lamp, lam_init):
    a = jnp.sum(lamp[0:1, :] * lamp[1:2, :], axis=-1, keepdims=True)
    b = jnp.sum(lamp[2:3, :] * lamp[3:4, :], axis=-1, keepdims=True)
    return jnp.exp(a) - jnp.exp(b) + lam_init


def _stack_maps(q):
    lane = lax.broadcasted_iota(jnp.int32, q.shape, 1)
    zero = jnp.zeros_like(q)
    return jnp.concatenate([jnp.where(lane < DIFF_HD, q, zero), jnp.where(lane >= DIFF_HD, q, zero)], axis=0)


def _diff_finish(acc, l, lam, sub, lam_init, tq):
    o2 = acc / l
    o = o2[:tq] - lam * o2[tq:]
    return _rms_rows(o, sub) * (1.0 - lam_init)


ONES_ROWS = 16
LOOP_TRIPS = 8
XPOSE_ROWS = 512


def _diff_prompt_kernel(q_ref, qn_ref, k_ref, v_ref, lamp_ref, sub_ref, o_ref, qt_scr, vt_scr, m_scr, acc_scr,
                        st0_scr, st1_scr, p0_scr, p1_scr, alpha0_scr, alpha1_scr, mx0_scr, mx1_scr,
                        qtn_scr, stn_scr, mxn_scr, *more_scr, tq, tk, lam_init):
    qi = pl.program_id(2)
    n_kv = v_ref.shape[0] // tk
    hd2 = v_ref.shape[1]
    halves = tq // tk
    assert halves % 2 == 0
    all_cgs = list(range(2 * halves))

    @pl.when(qi == 0)
    def _():
        for c in range(n_kv):
            vt_scr[c, 0:hd2, :] = v_ref[c * tk:(c + 1) * tk, :].astype(F32).T.astype(BF16)
            vt_scr[c, hd2:hd2 + ONES_ROWS, :] = jnp.ones((ONES_ROWS, tk), BF16)

    def transposed_queries(ref):
        parts = []
        for r0 in range(0, tq, XPOSE_ROWS):
            parts.append(_stack_maps(ref[r0:r0 + XPOSE_ROWS, :]).astype(F32).T.astype(BF16))
        return jnp.concatenate([p[:, :XPOSE_ROWS] for p in parts] + [p[:, XPOSE_ROWS:] for p in parts], axis=1)

    @pl.when(qi == 0)
    def _():
        qt_scr[...] = transposed_queries(q_ref)

    @pl.when(qi > 0)
    def _():
        qt_scr[...] = qtn_scr[...]

    m_scr[...] = jnp.full(m_scr.shape, -jnp.inf, F32)
    acc_scr[...] = jnp.zeros(acc_scr.shape, F32)

    def scores(kb, cgs, qt=qt_scr):
        k = k_ref[pl.ds(pl.multiple_of(kb * tk, tk), tk), :]
        sts = [_dot(k, qt[:, cg * tk:(cg + 1) * tk]) for cg in cgs]
        return sts, [jnp.max(st, axis=0, keepdims=True) for st in sts]

    def softmax(sts, maxes, slot, cgs, masked):
        pts, alphas = [], []
        for st, mx, cg, msk in zip(sts, maxes, cgs, masked):
            cols = slice(cg * tk, (cg + 1) * tk)
            if msk:
                r = lax.broadcasted_iota(jnp.int32, st.shape, 0)
                c = lax.broadcasted_iota(jnp.int32, st.shape, 1)
                st = jnp.where((r // CHUNK) <= (c // CHUNK), st, -jnp.inf)
                mx = jnp.max(st, axis=0, keepdims=True)
            m_prev = m_scr[slot, :, cols]
            m_new = jnp.maximum(m_prev, mx)
            alphas.append(jnp.exp2(m_prev - m_new))
            pts.append(jnp.exp2(st - m_new).astype(BF16))
            m_scr[slot, :, cols] = m_new
        return pts, alphas

    def accumulate(kb, slot, cgs, pts, alphas):
        vt = vt_scr[kb]
        for pt, alpha, cg in zip(pts, alphas, cgs):
            cols = slice(cg * tk, (cg + 1) * tk)
            acc_scr[slot, :, cols] = alpha * acc_scr[slot, :, cols] + _dot(vt, pt)

    st_scr, mx_scr = (st0_scr, st1_scr), (mx0_scr, mx1_scr)
    n_more = LOOP_TRIPS - 2
    p_scr = (p0_scr, p1_scr) + tuple(more_scr[:n_more])
    alpha_scr = (alpha0_scr, alpha1_scr) + tuple(more_scr[n_more:])
    last = LOOP_TRIPS - 1

    def put(scr, buf, cgs, vals):
        for cg, val in zip(cgs, vals):
            scr[buf][cg] = val

    def get(scr, buf, cgs):
        return [scr[buf][cg] for cg in cgs]

    def put_scores(buf, cgs, sts_maxes):
        put(st_scr, buf, cgs, sts_maxes[0])
        put(mx_scr, buf, cgs, sts_maxes[1])

    def trip(t, u, prev=None):
        par = u % 2
        prev = (u - 1) % LOOP_TRIPS if prev is None else prev
        pts, alphas = softmax(get(st_scr, par, all_cgs), get(mx_scr, par, all_cgs), par, all_cgs,
                              [False] * len(all_cgs))
        put(p_scr, u, all_cgs, pts)
        put(alpha_scr, u, all_cgs, alphas)
        put_scores(1 - par, all_cgs, scores(jnp.minimum(t + 1, n_kv - 1), all_cgs))
        accumulate(jnp.maximum(t - 1, 0), 1 - par, all_cgs,
                   get(p_scr, prev, all_cgs), get(alpha_scr, prev, all_cgs))

    def loop_body(i, carry):
        for u in range(LOOP_TRIPS):
            trip(LOOP_TRIPS * i + u, u)
        return carry

    n_trips = qi * halves

    @pl.when(qi == 0)
    def _():
        put_scores(0, all_cgs, scores(0, all_cgs))

    @pl.when(qi > 0)
    def _():
        put_scores(0, all_cgs, ([stn_scr[cg] for cg in all_cgs], [mxn_scr[cg] for cg in all_cgs]))

    put(p_scr, last, all_cgs, [jnp.zeros((tk, tk), BF16)] * len(all_cgs))
    put(alpha_scr, last, all_cgs, [jnp.ones((1, tk), F32)] * len(all_cgs))
    lax.fori_loop(0, n_trips // LOOP_TRIPS, loop_body, 0)
    half_body = LOOP_TRIPS // 2
    assert half_body % 2 == 0 and halves % half_body == 0

    if halves % LOOP_TRIPS:
        @pl.when(n_trips % LOOP_TRIPS == half_body)
        def _():
            for u in range(half_body, LOOP_TRIPS):
                trip(n_trips - LOOP_TRIPS + u, u, prev=last if u == half_body else None)

    kb0 = n_trips

    def visible(j):
        return [cg for cg in all_cgs if cg % halves >= j]

    later = scores(kb0 + 1, visible(1))
    qtn_scr[...] = transposed_queries(qn_ref)
    nxt = scores(0, all_cgs, qt=qtn_scr)
    for cg in all_cgs:
        stn_scr[cg] = nxt[0][cg]
        mxn_scr[cg] = nxt[1][cg]
    accumulate(jnp.maximum(kb0 - 1, 0), 1, all_cgs, get(p_scr, last, all_cgs), get(alpha_scr, last, all_cgs))
    cur = (get(st_scr, 0, all_cgs), get(mx_scr, 0, all_cgs))
    for j in range(halves):
        cgs_j = visible(j)
        pts, alphas = softmax(cur[0], cur[1], j % 2, cgs_j, [cg % halves == j for cg in cgs_j])
        cur = later
        if j + 2 < halves:
            later = scores(kb0 + j + 2, visible(j + 2))
        accumulate(kb0 + j, j % 2, cgs_j, pts, alphas)

    m_all = jnp.maximum(m_scr[0], m_scr[1])
    acc_all = jnp.exp2(m_scr[0] - m_all) * acc_scr[0] + jnp.exp2(m_scr[1] - m_all) * acc_scr[1]

    lam = _lambda_value(lamp_ref[...], lam_init)
    o2t = acc_all[0:hd2, :] / acc_all[hd2:hd2 + 1, :]
    ot = o2t[:, :tq] - lam * o2t[:, tq:]
    for r0 in range(0, tq, XPOSE_ROWS):
        o = ot[:, r0:r0 + XPOSE_ROWS].T
        o_ref[r0:r0 + XPOSE_ROWS, :] = (_rms_rows(o, sub_ref[...]) * (1.0 - lam_init)).astype(o_ref.dtype)


def _diff_prompt(p3, lamp, sub, lam_init, tq, tk):
    b, s, _ = p3.shape
    hd2 = 2 * DIFF_HD
    kern = functools.partial(_diff_prompt_kernel, tq=tq, tk=tk, lam_init=lam_init)
    per_d = D_MODEL // hd2
    n_cg = 2 * tq // tk
    return pl.pallas_call(
        kern,
        grid=(b, DIFF_HEADS, s // tq),
        in_specs=[
            pl.BlockSpec((None, tq, hd2), lambda bi, h, qi: (bi, qi, T_QA * per_d + h)),
            pl.BlockSpec((None, tq, hd2), lambda bi, h, qi: (bi, jnp.minimum(qi + 1, s // tq - 1), T_QA * per_d + h)),
            pl.BlockSpec((None, s, hd2), lambda bi, h, qi: (bi, 0, T_KA * per_d + h)),
            pl.BlockSpec((None, s, hd2), lambda bi, h, qi: (bi, 0, T_VA * per_d + h)),
            pl.BlockSpec(lamp.shape, lambda bi, h, qi: (0, 0)),
            pl.BlockSpec(sub.shape, lambda bi, h, qi: (0, 0)),
        ],
        out_specs=pl.BlockSpec((None, tq, hd2), lambda bi, h, qi: (bi, qi, h)),
        out_shape=jax.ShapeDtypeStruct((b, s, D_MODEL), BF16),
        scratch_shapes=[
            pltpu.VMEM((hd2, 2 * tq), BF16),
            pltpu.VMEM((s // tk, hd2 + ONES_ROWS, tk), BF16),
            pltpu.VMEM((2, 1, 2 * tq), F32),
            pltpu.VMEM((2, hd2 + ONES_ROWS, 2 * tq), F32),
            pltpu.VMEM((n_cg, tk, tk), F32),
            pltpu.VMEM((n_cg, tk, tk), F32),
            pltpu.VMEM((n_cg, tk, tk), BF16),
            pltpu.VMEM((n_cg, tk, tk), BF16),
            pltpu.VMEM((n_cg, 1, tk), F32),
            pltpu.VMEM((n_cg, 1, tk), F32),
            pltpu.VMEM((n_cg, 1, tk), F32),
            pltpu.VMEM((n_cg, 1, tk), F32),
            pltpu.VMEM((hd2, 2 * tq), BF16),
            pltpu.VMEM((n_cg, tk, tk), F32),
            pltpu.VMEM((n_cg, 1, tk), F32),
        ] + [pltpu.VMEM((n_cg, tk, tk), BF16)] * (LOOP_TRIPS - 2) + [pltpu.VMEM((n_cg, 1, tk), F32)] * (LOOP_TRIPS - 2),
        compiler_params=pltpu.CompilerParams(
            dimension_semantics=("parallel", "parallel", "arbitrary"), vmem_limit_bytes=VMEM_LIMIT_BYTES),
        name="diff_attn_prompt",
    )(p3, p3, p3, p3, lamp, sub)


SAMPLE_HEAD_GROUP = 4


def _diff_sample_kernel(q_ref, kn_ref, vn_ref, kc_ref, vc_ref, lamp_ref, sub_ref, o_ref, *, lam_init):
    tq = q_ref.shape[0]
    hd2 = 2 * DIFF_HD
    n_past = kc_ref.shape[1]
    lam = _lambda_value(lamp_ref[...], lam_init)
    for h0 in range(0, DIFF_HEADS, SAMPLE_HEAD_GROUP):
        heads = range(h0, h0 + SAMPLE_HEAD_GROUP)
        cols = [slice(h * hd2, (h + 1) * hd2) for h in heads]
        q2s = [_stack_maps(q_ref[:, c]) for c in cols]
        s_cs = [_dot(q2, kc_ref[c, :].astype(BF16)) for q2, c in zip(q2s, cols)]
        s_ns = [_dot_nt(q2, kn_ref[:, c]) for q2, c in zip(q2s, cols)]
        pcs, pns, ls = [], [], []
        for s_c, s_n in zip(s_cs, s_ns):
            m = jnp.maximum(jnp.max(s_c, axis=-1, keepdims=True), jnp.max(s_n, axis=-1, keepdims=True))
            p_c = jnp.exp2(s_c - m)
            p_n = jnp.exp2(s_n - m)
            ls.append(jnp.sum(p_c, axis=-1, keepdims=True) + jnp.sum(p_n, axis=-1, keepdims=True))
            pcs.append(p_c.astype(BF16))
            pns.append(p_n.astype(BF16))
        for h, c, p_c, p_n, l in zip(heads, cols, pcs, pns, ls):
            vc = vc_ref[pl.ds(h, n_past, stride=DIFF_HEADS), :]
            acc = _dot(p_c, vc.astype(BF16)) + _dot(p_n, vn_ref[:, c])
            o_ref[:, c] = _diff_finish(acc, l, lam, sub_ref[...], lam_init, tq).astype(o_ref.dtype)


def _diff_sample(p3, kc_t, vc, lamp, sub, lam_init):
    b, l, _ = p3.shape
    n_past = kc_t.shape[2]
    hd2 = 2 * DIFF_HD
    d = D_MODEL
    kern = functools.partial(_diff_sample_kernel, lam_init=lam_init)
    return pl.pallas_call(
        kern,
        grid=(b,),
        in_specs=[
            pl.BlockSpec((None, l, d), lambda bi: (bi, 0, T_QA)),
            pl.BlockSpec((None, l, d), lambda bi: (bi, 0, T_KA)),
            pl.BlockSpec((None, l, d), lambda bi: (bi, 0, T_VA)),
            pl.BlockSpec((None, d, n_past), lambda bi: (bi, 0, 0)),
            pl.BlockSpec((None, n_past * DIFF_HEADS, hd2), lambda bi: (bi, 0, 0)),
            pl.BlockSpec(lamp.shape, lambda bi: (0, 0)),
            pl.BlockSpec(sub.shape, lambda bi: (0, 0)),
        ],
        out_specs=pl.BlockSpec((None, l, d), lambda bi: (bi, 0, 0)),
        out_shape=jax.ShapeDtypeStruct((b, l, D_MODEL), BF16),
        compiler_params=pltpu.CompilerParams(
            dimension_semantics=("parallel",), vmem_limit_bytes=VMEM_LIMIT_BYTES),
        name="diff_attn_sample",
    )(p3, p3, p3, kc_t, vc, lamp, sub)


def _bcast_rows(x, period, row):
    r, c = x.shape
    x3 = x.reshape(r // period, period, c)
    return jnp.broadcast_to(x3[:, row:row + 1, :], x3.shape).reshape(r, c)


def _gla_kernel(*refs, rows, has_init):
    if has_init:
        q_ref, k_ref, v_ref, r_ref, g_ref, sub_ref, s0_ref, o_ref, sout_ref, st_scr, kf_scr, b_scr = refs
    else:
        q_ref, k_ref, v_ref, r_ref, g_ref, sub_ref, o_ref, sout_ref, st_scr, kf_scr, b_scr = refs
        s0_ref = None
    step = pl.program_id(2)
    n_chunks = rows // CHUNK
    n_sub = CHUNK // SUB_BLOCK

    @pl.when(step == 0)
    def _():
        if has_init:
            st_scr[...] = s0_ref[...]
        else:
            st_scr[...] = jnp.zeros(st_scr.shape, F32)

    q = q_ref[...].astype(F32)
    k = k_ref[...].astype(F32)
    g = g_ref[...]
    v = v_ref[...]
    chunk_rows = [slice(c * CHUNK, (c + 1) * CHUNK) for c in range(n_chunks)]

    ri = lax.broadcasted_iota(jnp.int32, (CHUNK, CHUNK), 0)
    ci = lax.broadcasted_iota(jnp.int32, (CHUNK, CHUNK), 1)
    tri = jnp.where(ci <= ri, 1.0, 0.0).astype(BF16)
    gw = jnp.concatenate([g[sl] for sl in chunk_rows], axis=1)
    g1 = gw.astype(BF16)
    rem = gw - g1.astype(F32)
    g2 = rem.astype(BF16)
    g3 = (rem - g2.astype(F32)).astype(BF16)
    bw = _dot(tri, g1) + _dot(tri, g2) + _dot(tri, g3)
    b = jnp.concatenate([bw[:, c * GLA_DK:(c + 1) * GLA_DK] for c in range(n_chunks)], axis=0)
    bex = b - g

    b_last = _bcast_rows(b, CHUNK, CHUNK - 1)
    b_blk = _bcast_rows(bex, SUB_BLOCK, 0)
    q_blk = q * jnp.exp(b - b_blk)
    q_chk = q * jnp.exp(b)
    k_end = k * jnp.exp(b_last - b)

    rowc = lax.broadcasted_iota(jnp.int32, (rows, GLA_DK), 0) % CHUNK
    zero = jnp.zeros_like(q)

    lhs_parts, rhs_parts = [], []
    for blk in range(1, n_sub):
        b_ref_blk = _bcast_rows(bex, CHUNK, blk * SUB_BLOCK)
        k_blk = k * jnp.exp(jnp.where(rowc < blk * SUB_BLOCK, b_ref_blk - b, NEG_BIG))
        lhs_parts.append(jnp.where((rowc // SUB_BLOCK) == blk, q_blk, zero))
        rhs_parts.append(k_blk)
    lhs = jnp.concatenate(lhs_parts, axis=-1).astype(BF16)
    rhs = jnp.concatenate(rhs_parts, axis=-1).astype(BF16)

    pad = SUB_BLOCK
    kf_scr[0:pad, :] = jnp.zeros((pad, GLA_DK), F32)
    b_scr[0:pad, :] = jnp.zeros((pad, GLA_DK), F32)
    kf_scr[pad:pad + rows, :] = k
    b_scr[pad:pad + rows, :] = b
    row_sub = rowc % SUB_BLOCK
    terms = []
    for d in range(SUB_BLOCK):
        kd = kf_scr[pad - d:pad - d + rows, :]
        bd = b_scr[pad - d:pad - d + rows, :]
        e = jnp.exp(jnp.where(row_sub >= d, b - bd, NEG_BIG))
        terms.append((q * kd * e).astype(BF16))
    sel_r = lax.broadcasted_iota(jnp.int32, (SUB_BLOCK * GLA_DK, LANES), 0) // GLA_DK
    sel_c = lax.broadcasted_iota(jnp.int32, (SUB_BLOCK * GLA_DK, LANES), 1)
    selector = jnp.where(sel_r + sel_c == SUB_BLOCK - 1, 1.0, 0.0).astype(BF16)
    diag_sums = _dot(jnp.concatenate(terms, axis=-1), selector)

    def skew(x):
        return pltpu.roll(x, LANES - (SUB_BLOCK - 1), 1, stride=1, stride_axis=0)

    atts = [(_dot_nt(lhs[sl], rhs[sl]) + skew(diag_sums[sl])[:, :CHUNK]).astype(BF16) for sl in chunk_rows]
    d_sts = [_dot(k_end[sl].T.astype(BF16), v[sl]) for sl in chunk_rows]
    o_intra = [_dot(att, v[sl]) for att, sl in zip(atts, chunk_rows)]
    decs = [jnp.exp(b[sl].T[:, CHUNK - 1:CHUNK]) for sl in chunk_rows]
    st = st_scr[...]
    outs = []
    for c, sl in enumerate(chunk_rows):
        outs.append(o_intra[c] + _dot(q_chk[sl].astype(BF16), st.astype(BF16)))
        st = st * decs[c] + d_sts[c]
    st_scr[...] = st
    o = jnp.concatenate(outs, axis=0) if n_chunks > 1 else outs[0]
    o_ref[...] = (_rms_rows(o, sub_ref[...]) * r_ref[...].astype(F32)).astype(o_ref.dtype)

    @pl.when(step == pl.num_programs(2) - 1)
    def _():
        sout_ref[...] = st_scr[...]


def _gla(p3, gk3, sub, s0, rows):
    b, s, _ = p3.shape
    has_init = s0 is not None
    kq = D_MODEL // GLA_DK
    kv = D_MODEL // GLA_DV
    in_specs = [
        pl.BlockSpec((None, rows, GLA_DK), lambda bi, h, r: (bi, r, T_QKB * kq + h)),
        pl.BlockSpec((None, rows, GLA_DK), lambda bi, h, r: (bi, r, T_QKB * kq + GLA_HEADS + h)),
        pl.BlockSpec((None, rows, GLA_DV), lambda bi, h, r: (bi, r, T_VB * kv + h)),
        pl.BlockSpec((None, rows, GLA_DV), lambda bi, h, r: (bi, r, T_RB * kv + h)),
        pl.BlockSpec((None, rows, GLA_DK), lambda bi, h, r: (bi, r, h)),
        pl.BlockSpec(sub.shape, lambda bi, h, r: (0, 0)),
    ]
    args = [p3, p3, p3, p3, gk3, sub]
    if has_init:
        in_specs.append(pl.BlockSpec((None, None, GLA_DK, GLA_DV), lambda bi, h, r: (bi, h, 0, 0)))
        args.append(s0)
    kern = functools.partial(_gla_kernel, rows=rows, has_init=has_init)
    return pl.pallas_call(
        kern,
        grid=(b, GLA_HEADS, s // rows),
        in_specs=in_specs,
        out_specs=[
            pl.BlockSpec((None, rows, GLA_DV), lambda bi, h, r: (bi, r, h)),
            pl.BlockSpec((None, None, GLA_DK, GLA_DV), lambda bi, h, r: (bi, h, 0, 0)),
        ],
        out_shape=[
            jax.ShapeDtypeStruct((b, s, GLA_HEADS * GLA_DV), BF16),
            jax.ShapeDtypeStruct((b, GLA_HEADS, GLA_DK, GLA_DV), F32),
        ],
        scratch_shapes=[
            pltpu.VMEM((GLA_DK, GLA_DV), F32),
            pltpu.VMEM((SUB_BLOCK + rows, GLA_DK), F32),
            pltpu.VMEM((SUB_BLOCK + rows, GLA_DK), F32),
        ],
        compiler_params=pltpu.CompilerParams(
            dimension_semantics=("parallel", "parallel", "arbitrary"), vmem_limit_bytes=VMEM_LIMIT_BYTES),
        name="gla",
    )(*args)


def _mem_attn_kernel(q_ref, k_ref, v_ref, o_ref, *, tiled):
    def head(ref, h):
        if not tiled:
            return ref[:, h * MEM_HD:(h + 1) * MEM_HD]
        period = MEM_HEADS * MEM_HD // LANES
        n_tok = ref.shape[0] // period
        return jnp.concatenate([ref[pl.ds(half * MEM_HEADS + h, n_tok, stride=period), :]
                                for half in range(MEM_HD // LANES)], axis=1)

    heads = range(MEM_HEADS)
    scores = [_dot_nt(q_ref[:, h * MEM_HD:(h + 1) * MEM_HD], head(k_ref, h).astype(BF16)) for h in heads]
    probs, sums = [], []
    for s in scores:
        p = jnp.exp(s - jnp.max(s, axis=-1, keepdims=True))
        sums.append(jnp.sum(p, axis=-1, keepdims=True))
        probs.append(p.astype(BF16))
    outs = [_dot(p, head(v_ref, h).astype(BF16)) / l for h, p, l in zip(heads, probs, sums)]
    o_ref[...] = jnp.concatenate(outs, axis=-1).astype(o_ref.dtype)


def _mem_attn(p3, mk, mv, tq):
    b, s, _ = p3.shape
    d = D_MODEL
    kv_block = (None,) + mk.shape[1:]
    return pl.pallas_call(
        functools.partial(_mem_attn_kernel, tiled=mk.shape[2] == LANES),
        grid=(b, s // tq),
        in_specs=[
            pl.BlockSpec((None, tq, d), lambda bi, i: (bi, i, T_QM)),
            pl.BlockSpec(kv_block, lambda bi, i: (bi, 0, 0)),
            pl.BlockSpec(kv_block, lambda bi, i: (bi, 0, 0)),
        ],
        out_specs=pl.BlockSpec((None, tq, d), lambda bi, i: (bi, i, 0)),
        out_shape=jax.ShapeDtypeStruct((b, s, d), BF16),
        compiler_params=pltpu.CompilerParams(
            dimension_semantics=("parallel", "parallel"), vmem_limit_bytes=VMEM_LIMIT_BYTES),
        name="mem_attn",
    )(p3, mk, mv)


def _mix_kernel(x_ref, oa_ref, ob_ref, om_ref, ga_ref, gb_ref, gm_ref, wd_ref, wg_ref, wm_ref, wo_ref,
                y_ref):
    m = (ga_ref[...].astype(F32) * _dot(oa_ref[...], wd_ref[...])
         + gb_ref[...].astype(F32) * _dot(ob_ref[...], wg_ref[...])
         + gm_ref[...].astype(F32) * _dot(om_ref[...], wm_ref[...]))
    y_ref[...] = x_ref[...] + _dot(m.astype(BF16), wo_ref[...])


def _mix(x2, oa, ob, om, p2, wd, wg, wm, wo, tm):
    t = x2.shape[0]
    d = D_MODEL
    row = lambda i: (i, 0)
    full = lambda i: (0, 0)
    wspec = pl.BlockSpec((d, d), full, pipeline_mode=pl.Buffered(1))
    return pl.pallas_call(
        _mix_kernel,
        grid=(t // tm,),
        in_specs=[
            pl.BlockSpec((tm, d), row), pl.BlockSpec((tm, d), row), pl.BlockSpec((tm, d), row),
            pl.BlockSpec((tm, d), row),
            pl.BlockSpec((tm, d), lambda i: (i, T_GATE)),
            pl.BlockSpec((tm, d), lambda i: (i, T_GATE + 1)),
            pl.BlockSpec((tm, d), lambda i: (i, T_GATE + 2)),
            wspec, wspec, wspec, wspec,
        ],
        out_specs=pl.BlockSpec((tm, d), row),
        out_shape=jax.ShapeDtypeStruct((t, d), F32),
        compiler_params=pltpu.CompilerParams(
            dimension_semantics=("parallel",), vmem_limit_bytes=VMEM_LIMIT_BYTES),
        name="mix_out",
    )(x2, oa, ob, om, p2, p2, p2, wd, wg, wm, wo)


FFN_CHUNK = 256


def _ffn_kernel(x_ref, g_ref, wup_ref, cw_ref, cb_ref, wd_ref, cs_ref, y_ref, cso_ref,
                u_scr, carry_scr, gv_scr, *, n_seq, seq_rows, tiles_per_seq):
    i = pl.program_id(0)
    gap = SUBLANES
    stride = seq_rows + gap
    tail = CONV_W - 1
    n_chunks = D_FF // FFN_CHUNK
    x = x_ref[...]
    h = _rms_rows(x, g_ref[...]).astype(BF16)

    first = (i % tiles_per_seq) == 0
    for s in range(n_seq):
        base = s * stride

        @pl.when(first)
        def _():
            u_scr[base:base + gap, :] = jnp.zeros((gap, D_FF), F32)
            u_scr[base + gap - tail:base + gap, :] = cs_ref[s]

        @pl.when(jnp.logical_not(first))
        def _():
            u_scr[base:base + gap, :] = carry_scr[...]

    def up(c):
        cols = slice(c * FFN_CHUNK, (c + 1) * FFN_CHUNK)
        gate_cols = slice(D_FF + c * FFN_CHUNK, D_FF + (c + 1) * FFN_CHUNK)
        return _dot(h, wup_ref[:, cols]), _dot(h, wup_ref[:, gate_cols])

    def gated(c, u, vv):
        cols = slice(c * FFN_CHUNK, (c + 1) * FFN_CHUNK)
        cw = cw_ref[:, cols]
        outs = []
        for s in range(n_seq):
            base = s * stride + gap
            u_scr[base:base + seq_rows, cols] = u[s * seq_rows:(s + 1) * seq_rows]
            conv = cb_ref[:, cols]
            for j in range(CONV_W):
                off = base - tail + j
                conv = conv + cw[j:j + 1, :] * u_scr[off:off + seq_rows, cols]
            outs.append(conv)
        uc = jnp.concatenate(outs, axis=0) if n_seq > 1 else outs[0]
        gelu = 0.5 * uc * (1.0 + jnp.tanh(math.sqrt(2.0 / math.pi) * (uc + 0.044715 * (uc * uc * uc))))
        return (gelu * vv).astype(BF16)

    nxt = up(0)
    for c in range(n_chunks):
        cur = nxt
        if c + 1 < n_chunks:
            nxt = up(c + 1)
        gv_scr[:, c * FFN_CHUNK:(c + 1) * FFN_CHUNK] = gated(c, *cur)
    y_ref[...] = x + _dot(gv_scr[...], wd_ref[...])

    for s in range(n_seq):
        base = s * stride + gap
        cso_ref[s] = u_scr[base + seq_rows - tail:base + seq_rows, :]
    carry_scr[...] = u_scr[seq_rows:seq_rows + gap, :]


def _ffn(x2, g_ffn, w_up, conv_w, conv_b, w_down, conv_state, n_seq, seq_rows, tiles_per_seq):
    t = x2.shape[0]
    d = D_MODEL
    tm = n_seq * seq_rows
    nb = conv_state.shape[0]
    tail = CONV_W - 1
    kern = functools.partial(_ffn_kernel, n_seq=n_seq, seq_rows=seq_rows, tiles_per_seq=tiles_per_seq)
    y, tails = pl.pallas_call(
        kern,
        grid=(t // tm,),
        in_specs=[
            pl.BlockSpec((tm, d), lambda i: (i, 0)),
            _resident((1, d)),
            _resident(w_up.shape),
            _resident(conv_w.shape),
            _resident(conv_b.shape),
            _resident(w_down.shape),
            pl.BlockSpec((n_seq, tail, D_FF), lambda i: (i // tiles_per_seq, 0, 0)),
        ],
        out_specs=[
            pl.BlockSpec((tm, d), lambda i: (i, 0)),
            pl.BlockSpec((n_seq, tail, D_FF), lambda i: (i, 0, 0)),
        ],
        out_shape=[
            jax.ShapeDtypeStruct((t, d), F32),
            jax.ShapeDtypeStruct((nb * tiles_per_seq, tail, D_FF), F32),
        ],
        scratch_shapes=[
            pltpu.VMEM((n_seq * (seq_rows + SUBLANES), D_FF), F32),
            pltpu.VMEM((SUBLANES, D_FF), F32),
            pltpu.VMEM((tm, D_FF), BF16),
        ],
        compiler_params=pltpu.CompilerParams(
            dimension_semantics=("arbitrary",), vmem_limit_bytes=VMEM_LIMIT_BYTES),
        name="conv_ffn",
    )(x2, g_ffn, w_up, conv_w, conv_b, w_down, conv_state)
    return y, tails.reshape(nb, tiles_per_seq, tail, D_FF)[:, -1]


def _mem_cache_rows(c):
    b, n, h, hd = c.shape
    return c.reshape(b, n, h, hd // LANES, LANES).transpose(0, 1, 3, 2, 4).reshape(b, -1, LANES)


def _tile_gain(g, reps):
    return jnp.tile(g.astype(F32), reps).reshape(1, -1)


def _layer_weights(l, g_attn, w_in, w_gk2, b_gk, qn_diff, kn_diff, lam_q1, lam_k1, lam_q2, lam_k2,
                   subln_diff, subln_gla, g_mem, w_mem_kv, qn_mem, kn_mem, w_proj_diff, w_proj_gla,
                   w_proj_mem, w_out, g_ffn, w_up, conv_w, conv_b, w_down):
    d = D_MODEL
    w = w_in[l]
    lr0 = 6 * d
    wa = w[:, :lr0].astype(BF16)
    wb = w[:, lr0 + GK_RANK:].astype(BF16)
    wg1 = jnp.pad(w[:, lr0:lr0 + GK_RANK], ((0, 0), (0, LANES - GK_RANK))).astype(BF16)
    wg2 = jnp.pad(w_gk2[l], ((0, LANES - GK_RANK), (0, 0))).astype(BF16)
    return dict(
        g_attn=g_attn[l].reshape(1, d), wa=wa, wb=wb, wg1=wg1, wg2=wg2, bgk=b_gk[l].reshape(1, -1),
        qn_t=_tile_gain(qn_diff[l], d // DIFF_HD), kn_t=_tile_gain(kn_diff[l], d // DIFF_HD),
        qmn_t=_tile_gain(qn_mem[l], d // MEM_HD), knm_t=_tile_gain(kn_mem[l], d // MEM_HD),
        g64=_group_matrix(MXU_DIM, DIFF_HD), g256=_group_matrix(MEM_HD, MEM_HD),
        lamp=jnp.stack([lam_q1[l], lam_k1[l], lam_q2[l], lam_k2[l]]).astype(F32),
        sub_diff=subln_diff[l].reshape(1, -1), sub_gla=subln_gla[l].reshape(1, -1),
        g_mem=g_mem[l].reshape(1, d), w_mem_kv=w_mem_kv[l].astype(BF16),
        wd=w_proj_diff[l].astype(BF16), wg=w_proj_gla[l].astype(BF16), wm=w_proj_mem[l].astype(BF16),
        wo=w_out[l].astype(BF16), g_ffn=g_ffn[l].reshape(1, d), w_up=w_up[l].astype(BF16),
        conv_w=conv_w[l], conv_b=conv_b[l].reshape(1, -1), w_down=w_down[l].astype(BF16),
    )


def _group(x, wts, lam_init, mem_k, mem_v, past_k, past_v, gla_state, conv_state, prompt):
    b, s, d = x.shape
    t = b * s
    x2 = x.reshape(t, d)
    tm = _pick(t, 512)
    p2, ka, va, gk = _inproj(x2, wts["g_attn"], wts["wa"], wts["wb"], wts["wg1"], wts["wg2"], wts["bgk"],
                             wts["qn_t"], wts["kn_t"], wts["qmn_t"], wts["g64"], wts["g256"],
                             _pick(s, 256) if prompt else _pick(t, 256), s, prompt)
    if prompt:
        ka = ka.reshape(b, DIFF_HEADS, 2, DIFF_HD, s).transpose(0, 4, 1, 2, 3)
    else:
        ka = ka.reshape(b, s, DIFF_HEADS, 2, DIFF_HD)
    p3 = p2.reshape(b, s, N_PTILES * d)
    gk3 = gk.reshape(b, s, GLA_HEADS * GLA_DK)
    if prompt:
        oa = _diff_prompt(p3, wts["lamp"], wts["sub_diff"], lam_init, _pick(s, 1024), 256)
        ob, gla_new = _gla(p3, gk3, wts["sub_gla"], None, _pick(s, 512))
    else:
        oa = _diff_sample(p3, past_k, past_v, wts["lamp"], wts["sub_diff"], lam_init)
        ob, gla_new = _gla(p3, gk3, wts["sub_gla"], gla_state, s)
    om = _mem_attn(p3, mem_k, mem_v, _pick(s, 512))
    x1 = _mix(x2, oa.reshape(t, d), ob.reshape(t, d), om.reshape(t, d), p2,
              wts["wd"], wts["wg"], wts["wm"], wts["wo"], tm)
    if prompt:
        rows = _pick(s, 512)
        y, cs = _ffn(x1, wts["g_ffn"], wts["w_up"], wts["conv_w"], wts["conv_b"], wts["w_down"],
                     conv_state, 1, rows, s // rows)
    else:
        y, cs = _ffn(x1, wts["g_ffn"], wts["w_up"], wts["conv_w"], wts["conv_b"], wts["w_down"],
                     conv_state, b, s, 1)
    return y.reshape(b, s, d), ka, va, gla_new, cs


def kernel(x_prompt, x_sample, mem_prompt, cache_diff_k, cache_diff_v, cache_mem_k, cache_mem_v, state_gla, state_conv, g_attn, w_in, w_gk2, b_gk, qn_diff, kn_diff, lam_q1, lam_k1, lam_q2, lam_k2, subln_diff, subln_gla, g_mem, w_mem_kv, qn_mem, kn_mem, w_proj_diff, w_proj_gla, w_proj_mem, w_out, g_ffn, w_up, conv_w, conv_b, w_down):
    depth = g_attn.shape[0]
    d = D_MODEL
    xp, xs = x_prompt, x_sample
    bp, sp, _ = xp.shape
    bs, ss, _ = xs.shape
    n_mem = mem_prompt.shape[1]
    outs = [[] for _ in range(10)]
    for l in range(depth):
        lam_init = 0.8 - 0.6 * math.exp(-0.3 * l)
        wts = _layer_weights(l, g_attn, w_in, w_gk2, b_gk, qn_diff, kn_diff, lam_q1, lam_k1, lam_q2,
                             lam_k2, subln_diff, subln_gla, g_mem, w_mem_kv, qn_mem, kn_mem,
                             w_proj_diff, w_proj_gla, w_proj_mem, w_out, g_ffn, w_up, conv_w, conv_b,
                             w_down)
        mk, mv = _memkv(mem_prompt.reshape(bp * n_mem, d), wts["g_mem"], wts["w_mem_kv"], wts["knm_t"],
                        wts["g256"])
        mk = mk.reshape(bp, n_mem, d)
        mv = mv.reshape(bp, n_mem, d)
        xp, kp, vp, gp, cp = _group(xp, wts, lam_init, mk, mv, None, None, None,
                                    jnp.zeros((bp, CONV_W - 1, D_FF), F32), True)
        xs, ks_, vs_, gs, cs = _group(
            xs, wts, lam_init, _mem_cache_rows(cache_mem_k[l]), _mem_cache_rows(cache_mem_v[l]),
            cache_diff_k[l].transpose(0, 2, 3, 4, 1).reshape(bs, d, -1), cache_diff_v[l].reshape(bs, -1, 2 * DIFF_HD),
            state_gla[l], state_conv[l], False)
        vals = (kp, vp.reshape(bp, sp, DIFF_HEADS, 2 * DIFF_HD),
                mk.reshape(bp, n_mem, MEM_HEADS, MEM_HD), mv.reshape(bp, n_mem, MEM_HEADS, MEM_HD), gp, cp,
                ks_, vs_.reshape(bs, ss, DIFF_HEADS, 2 * DIFF_HD), gs, cs)
        for o, v in zip(outs, vals):
            o.append(v)
    return (xp, xs) + tuple(jnp.stack(o) for o in outs)
```

```python
import functools
import math

import jax
import jax.numpy as jnp
from jax import lax
from jax.experimental import pallas as pl
from jax.experimental.pallas import tpu as pltpu

F32 = jnp.float32
BF16 = jnp.bfloat16

D_MODEL = 1024
CHUNK = 64
EPS = 1e-6
DIFF_HEADS = 8
DIFF_HD = 64
DIFF_SCALE = DIFF_HD ** -0.5
LOG2E = math.log2(math.e)
GLA_HEADS = 4
GLA_DK = 128
GLA_DV = 256
GLA_SCALE = GLA_DK ** -0.5
GK_RANK = 16
GK_NORM = 16.0
MEM_HEADS = 4
MEM_HD = 256
MEM_SCALE = MEM_HD ** -0.5
D_FF = 2816
CONV_W = 3

LANES = 128
SUBLANES = 8
MXU_DIM = 256
VMEM_LIMIT_BYTES = 48 * 1024 * 1024

T_QA, T_KA, T_VA, T_QKB, T_VB, T_RB, T_QM, T_GATE = 0, 1, 2, 3, 4, 5, 6, 7
N_PTILES = 10
SUB_BLOCK = 8
NEG_BIG = -1e30


def _dot(a, b):
    return jnp.dot(a, b, preferred_element_type=F32)


def _dot_nt(a, b):
    return lax.dot_general(a, b, (((1,), (1,)), ((), ())), preferred_element_type=F32)


def _sigmoid(x):
    return 1.0 / (1.0 + jnp.exp(-x))


def _pick(n, pref):
    t = min(n, pref)
    while n % t:
        t -= 1
    return t


def _rms_rows(x, gain):
    ms = jnp.mean(x * x, axis=-1, keepdims=True)
    return x * lax.rsqrt(ms + EPS) * gain


def _group_rms(y, gmat, gain):
    slab = gmat.shape[0]
    outs = []
    for c in range(y.shape[-1] // slab):
        ys = y[:, c * slab:(c + 1) * slab]
        ms = _dot((ys * ys).astype(BF16), gmat)
        outs.append(ys * lax.rsqrt(ms + EPS))
    return jnp.concatenate(outs, axis=-1) * gain


def _group_matrix(slab, group):
    r = jnp.arange(slab) // group
    return jnp.where(r[:, None] == r[None, :], 1.0 / group, 0.0).astype(BF16)


def _inproj_kernel(x_ref, g_ref, wa_ref, wb_ref, wg1_ref, wg2_ref, bgk_ref, qn_ref, kn_ref, qmn_ref,
                   g64_ref, g256_ref, p_ref, ka_ref, vaf_ref, gk_ref, *, k_transposed):
    d = D_MODEL
    n_a = wa_ref.shape[1] // d
    h = _rms_rows(x_ref[...], g_ref[...]).astype(BF16)

    def project(j):
        if j < n_a:
            return _dot(h, wa_ref[:, j * d:(j + 1) * d])
        return _dot(h, wb_ref[:, (j - n_a) * d:(j - n_a + 1) * d])

    def finish(j, acc):
        if j == T_QA:
            out = _group_rms(acc, g64_ref[...], qn_ref[...]) * (DIFF_SCALE * LOG2E)
        elif j == T_KA:
            out = _group_rms(acc, g64_ref[...], kn_ref[...])
            ka_ref[...] = out.T if k_transposed else out
        elif j == T_VA:
            vaf_ref[...] = acc
            out = acc
        elif j == T_QKB:
            half = d // 2
            out = jnp.concatenate([acc[:, :half] * GLA_SCALE, acc[:, half:]], axis=-1)
            lr = _dot(h, wg1_ref[...])
            z = _dot(lr.astype(BF16), wg2_ref[...]) + bgk_ref[...]
            log_sig = jnp.minimum(z, 0.0) - jnp.log(1.0 + jnp.exp(-jnp.abs(z)))
            gk_ref[...] = log_sig * (1.0 / GK_NORM)
        elif j == T_VB:
            out = acc
        elif j == T_RB:
            out = acc * _sigmoid(acc)
        elif j == T_QM:
            out = _group_rms(acc, g256_ref[...], qmn_ref[...]) * MEM_SCALE
        else:
            out = _sigmoid(acc)
        p_ref[:, j * d:(j + 1) * d] = out.astype(BF16)

    acc_next = project(0)
    for j in range(N_PTILES):
        acc = acc_next
        if j + 1 < N_PTILES:
            acc_next = project(j + 1)
        finish(j, acc)


def _resident(shape):
    return pl.BlockSpec(shape, lambda *_: (0,) * len(shape), pipeline_mode=pl.Buffered(1))


def _inproj(x2, g_attn, wa, wb, wg1, wg2, bgk, qn_t, kn_t, qmn_t, g64, g256, tm, seq_len, k_transposed):
    t = x2.shape[0]
    d = D_MODEL
    row = lambda i: (i, 0)
    if k_transposed:
        per_seq = seq_len // tm
        ka_spec = pl.BlockSpec((None, d, tm), lambda i: (i // per_seq, 0, i % per_seq))
        ka_shape = jax.ShapeDtypeStruct((t // seq_len, d, seq_len), F32)
    else:
        ka_spec = pl.BlockSpec((tm, d), row)
        ka_shape = jax.ShapeDtypeStruct((t, d), F32)
    small = [wg1, wg2, bgk, qn_t, kn_t, qmn_t, g64, g256]
    return pl.pallas_call(
        functools.partial(_inproj_kernel, k_transposed=k_transposed),
        grid=(t // tm,),
        in_specs=[pl.BlockSpec((tm, d), row), _resident((1, d)), _resident(wa.shape), _resident(wb.shape)]
        + [_resident(a.shape) for a in small],
        out_specs=[
            pl.BlockSpec((tm, N_PTILES * d), row),
            ka_spec,
            pl.BlockSpec((tm, d), row),
            pl.BlockSpec((tm, GLA_HEADS * GLA_DK), row),
        ],
        out_shape=[
            jax.ShapeDtypeStruct((t, N_PTILES * d), BF16),
            ka_shape,
            jax.ShapeDtypeStruct((t, d), F32),
            jax.ShapeDtypeStruct((t, GLA_HEADS * GLA_DK), F32),
        ],
        compiler_params=pltpu.CompilerParams(
            dimension_semantics=("parallel",), vmem_limit_bytes=VMEM_LIMIT_BYTES),
        name="inproj",
    )(x2, g_attn, wa, wb, *small)


def _memkv_kernel(m_ref, g_ref, w_ref, kn_ref, g256_ref, k_ref, v_ref):
    j = pl.program_id(0)
    h = _rms_rows(m_ref[...], g_ref[...]).astype(BF16)
    acc = _dot(h, w_ref[...])

    @pl.when(j == 0)
    def _():
        k_ref[...] = _group_rms(acc, g256_ref[...], kn_ref[...])

    @pl.when(j == 1)
    def _():
        v_ref[...] = acc


def _memkv(mem2, g_mem, w_kv, knm_t, g256):
    t = mem2.shape[0]
    d = D_MODEL
    full = lambda j: (0, 0)
    return pl.pallas_call(
        _memkv_kernel,
        grid=(2,),
        in_specs=[
            pl.BlockSpec((t, d), full),
            pl.BlockSpec((1, d), full),
            pl.BlockSpec((d, d), lambda j: (0, j)),
            pl.BlockSpec((1, d), full),
            pl.BlockSpec(g256.shape, full),
        ],
        out_specs=[pl.BlockSpec((t, d), full), pl.BlockSpec((t, d), full)],
        out_shape=[jax.ShapeDtypeStruct((t, d), F32), jax.ShapeDtypeStruct((t, d), F32)],
        compiler_params=pltpu.CompilerParams(
            dimension_semantics=("arbitrary",), vmem_limit_bytes=VMEM_LIMIT_BYTES),
        name="memkv",
    )(mem2, g_mem, w_kv, knm_t, g256)


def _lambda_value(lamp, lam_init):
    a = jnp.sum(lamp[0:1, :] * lamp[1:2, :], axis=-1, keepdims=True)
    b = jnp.sum(lamp[2:3, :] * lamp[3:4, :], axis=-1, keepdims=True)
    return jnp.exp(a) - jnp.exp(b) + lam_init


def _stack_maps(q):
    lane = lax.broadcasted_iota(jnp.int32, q.shape, 1)
    zero = jnp.zeros_like(q)
    return jnp.concatenate([jnp.where(lane < DIFF_HD, q, zero), jnp.where(lane >= DIFF_HD, q, zero)], axis=0)


def _diff_finish(acc, l, lam, sub, lam_init, tq):
    o2 = acc / l
    o = o2[:tq] - lam * o2[tq:]
    return _rms_rows(o, sub) * (1.0 - lam_init)


ONES_ROWS = 16
LOOP_TRIPS = 8
XPOSE_ROWS = 512


def _diff_prompt_kernel(q_ref, qn_ref, k_ref, v_ref, lamp_ref, sub_ref, o_ref, qt_scr, vt_scr, m_scr, acc_scr,
                        st0_scr, st1_scr, p0_scr, p1_scr, alpha0_scr, alpha1_scr, mx0_scr, mx1_scr,
                        qtn_scr, stn_scr, mxn_scr, *more_scr, tq, tk, lam_init):
    qi = pl.program_id(2)
    n_kv = v_ref.shape[0] // tk
    hd2 = v_ref.shape[1]
    halves = tq // tk
    assert halves % 2 == 0
    all_cgs = list(range(2 * halves))

    @pl.when(qi == 0)
    def _():
        for c in range(n_kv):
            vt_scr[c, 0:hd2, :] = v_ref[c * tk:(c + 1) * tk, :].astype(F32).T.astype(BF16)
            vt_scr[c, hd2:hd2 + ONES_ROWS, :] = jnp.ones((ONES_ROWS, tk), BF16)

    def transposed_queries(ref):
        parts = []
        for r0 in range(0, tq, XPOSE_ROWS):
            parts.append(_stack_maps(ref[r0:r0 + XPOSE_ROWS, :]).astype(F32).T.astype(BF16))
        return jnp.concatenate([p[:, :XPOSE_ROWS] for p in parts] + [p[:, XPOSE_ROWS:] for p in parts], axis=1)

    @pl.when(qi == 0)
    def _():
        qt_scr[...] = transposed_queries(q_ref)

    @pl.when(qi > 0)
    def _():
        qt_scr[...] = qtn_scr[...]

    m_scr[...] = jnp.full(m_scr.shape, -jnp.inf, F32)
    acc_scr[...] = jnp.zeros(acc_scr.shape, F32)

    def scores(kb, cgs, qt=qt_scr):
        k = k_ref[pl.ds(pl.multiple_of(kb * tk, tk), tk), :]
        sts = [_dot(k, qt[:, cg * tk:(cg + 1) * tk]) for cg in cgs]
        return sts, [jnp.max(st, axis=0, keepdims=True) for st in sts]

    def softmax(sts, maxes, slot, cgs, masked):
        pts, alphas = [], []
        for st, mx, cg, msk in zip(sts, maxes, cgs, masked):
            cols = slice(cg * tk, (cg + 1) * tk)
            if msk:
                r = lax.broadcasted_iota(jnp.int32, st.shape, 0)
                c = lax.broadcasted_iota(jnp.int32, st.shape, 1)
                st = jnp.where((r // CHUNK) <= (c // CHUNK), st, -jnp.inf)
                mx = jnp.max(st, axis=0, keepdims=True)
            m_prev = m_scr[slot, :, cols]
            m_new = jnp.maximum(m_prev, mx)
            alphas.append(jnp.exp2(m_prev - m_new))
            pts.append(jnp.exp2(st - m_new).astype(BF16))
            m_scr[slot, :, cols] = m_new
        return pts, alphas

    def accumulate(kb, slot, cgs, pts, alphas):
        vt = vt_scr[kb]
        for pt, alpha, cg in zip(pts, alphas, cgs):
            cols = slice(cg * tk, (cg + 1) * tk)
            acc_scr[slot, :, cols] = alpha * acc_scr[slot, :, cols] + _dot(vt, pt)

    st_scr, mx_scr = (st0_scr, st1_scr), (mx0_scr, mx1_scr)
    n_more = LOOP_TRIPS - 2
    p_scr = (p0_scr, p1_scr) + tuple(more_scr[:n_more])
    alpha_scr = (alpha0_scr, alpha1_scr) + tuple(more_scr[n_more:])
    last = LOOP_TRIPS - 1

    def put(scr, buf, cgs, vals):
        for cg, val in zip(cgs, vals):
            scr[buf][cg] = val

    def get(scr, buf, cgs):
        return [scr[buf][cg] for cg in cgs]

    def put_scores(buf, cgs, sts_maxes):
        put(st_scr, buf, cgs, sts_maxes[0])
        put(mx_scr, buf, cgs, sts_maxes[1])

    def trip(t, u, prev=None):
        par = u % 2
        prev = (u - 1) % LOOP_TRIPS if prev is None else prev
        pts, alphas = softmax(get(st_scr, par, all_cgs), get(mx_scr, par, all_cgs), par, all_cgs,
                              [False] * len(all_cgs))
        put(p_scr, u, all_cgs, pts)
        put(alpha_scr, u, all_cgs, alphas)
        put_scores(1 - par, all_cgs, scores(jnp.minimum(t + 1, n_kv - 1), all_cgs))
        accumulate(jnp.maximum(t - 1, 0), 1 - par, all_cgs,
                   get(p_scr, prev, all_cgs), get(alpha_scr, prev, all_cgs))

    def loop_body(i, carry):
        for u in range(LOOP_TRIPS):
            trip(LOOP_TRIPS * i + u, u)
        return carry

    n_trips = qi * halves

    @pl.when(qi == 0)
    def _():
        put_scores(0, all_cgs, scores(0, all_cgs))

    @pl.when(qi > 0)
    def _():
        put_scores(0, all_cgs, ([stn_scr[cg] for cg in all_cgs], [mxn_scr[cg] for cg in all_cgs]))

    put(p_scr, last, all_cgs, [jnp.zeros((tk, tk), BF16)] * len(all_cgs))
    put(alpha_scr, last, all_cgs, [jnp.ones((1, tk), F32)] * len(all_cgs))
    lax.fori_loop(0, n_trips // LOOP_TRIPS, loop_body, 0)
    half_body = LOOP_TRIPS // 2
    assert half_body % 2 == 0 and halves % half_body == 0

    if halves % LOOP_TRIPS:
        @pl.when(n_trips % LOOP_TRIPS == half_body)
        def _():
            for u in range(half_body, LOOP_TRIPS):
                trip(n_trips - LOOP_TRIPS + u, u, prev=last if u == half_body else None)

    kb0 = n_trips

    def visible(j):
        return [cg for cg in all_cgs if cg % halves >= j]

    later = scores(kb0 + 1, visible(1))
    qtn_scr[...] = transposed_queries(qn_ref)
    nxt = scores(0, all_cgs, qt=qtn_scr)
    for cg in all_cgs:
        stn_scr[cg] = nxt[0][cg]
        mxn_scr[cg] = nxt[1][cg]
    accumulate(jnp.maximum(kb0 - 1, 0), 1, all_cgs, get(p_scr, last, all_cgs), get(alpha_scr, last, all_cgs))
    cur = (get(st_scr, 0, all_cgs), get(mx_scr, 0, all_cgs))
    for j in range(halves):
        cgs_j = visible(j)
        pts, alphas = softmax(cur[0], cur[1], j % 2, cgs_j, [cg % halves == j for cg in cgs_j])
        cur = later
        if j + 2 < halves:
            later = scores(kb0 + j + 2, visible(j + 2))
        accumulate(kb0 + j, j % 2, cgs_j, pts, alphas)

    m_all = jnp.maximum(m_scr[0], m_scr[1])
    acc_all = jnp.exp2(m_scr[0] - m_all) * acc_scr[0] + jnp.exp2(m_scr[1] - m_all) * acc_scr[1]

    lam = _lambda_value(lamp_ref[...], lam_init)
    o2t = acc_all[0:hd2, :] / acc_all[hd2:hd2 + 1, :]
    ot = o2t[:, :tq] - lam * o2t[:, tq:]
    for r0 in range(0, tq, XPOSE_ROWS):
        o = ot[:, r0:r0 + XPOSE_ROWS].T
        o_ref[r0:r0 + XPOSE_ROWS, :] = (_rms_rows(o, sub_ref[...]) * (1.0 - lam_init)).astype(o_ref.dtype)


def _diff_prompt(p3, lamp, sub, lam_init, tq, tk):
    b, s, _ = p3.shape
    hd2 = 2 * DIFF_HD
    kern = functools.partial(_diff_prompt_kernel, tq=tq, tk=tk, lam_init=lam_init)
    per_d = D_MODEL // hd2
    n_cg = 2 * tq // tk
    return pl.pallas_call(
        kern,
        grid=(b, DIFF_HEADS, s // tq),
        in_specs=[
            pl.BlockSpec((None, tq, hd2), lambda bi, h, qi: (bi, qi, T_QA * per_d + h)),
            pl.BlockSpec((None, tq, hd2), lambda bi, h, qi: (bi, jnp.minimum(qi + 1, s // tq - 1), T_QA * per_d + h)),
            pl.BlockSpec((None, s, hd2), lambda bi, h, qi: (bi, 0, T_KA * per_d + h)),
            pl.BlockSpec((None, s, hd2), lambda bi, h, qi: (bi, 0, T_VA * per_d + h)),
            pl.BlockSpec(lamp.shape, lambda bi, h, qi: (0, 0)),
            pl.BlockSpec(sub.shape, lambda bi, h, qi: (0, 0)),
        ],
        out_specs=pl.BlockSpec((None, tq, hd2), lambda bi, h, qi: (bi, qi, h)),
        out_shape=jax.ShapeDtypeStruct((b, s, D_MODEL), BF16),
        scratch_shapes=[
            pltpu.VMEM((hd2, 2 * tq), BF16),
            pltpu.VMEM((s // tk, hd2 + ONES_ROWS, tk), BF16),
            pltpu.VMEM((2, 1, 2 * tq), F32),
            pltpu.VMEM((2, hd2 + ONES_ROWS, 2 * tq), F32),
            pltpu.VMEM((n_cg, tk, tk), F32),
            pltpu.VMEM((n_cg, tk, tk), F32),
            pltpu.VMEM((n_cg, tk, tk), BF16),
            pltpu.VMEM((n_cg, tk, tk), BF16),
            pltpu.VMEM((n_cg, 1, tk), F32),
            pltpu.VMEM((n_cg, 1, tk), F32),
            pltpu.VMEM((n_cg, 1, tk), F32),
            pltpu.VMEM((n_cg, 1, tk), F32),
            pltpu.VMEM((hd2, 2 * tq), BF16),
            pltpu.VMEM((n_cg, tk, tk), F32),
            pltpu.VMEM((n_cg, 1, tk), F32),
        ] + [pltpu.VMEM((n_cg, tk, tk), BF16)] * (LOOP_TRIPS - 2) + [pltpu.VMEM((n_cg, 1, tk), F32)] * (LOOP_TRIPS - 2),
        compiler_params=pltpu.CompilerParams(
            dimension_semantics=("parallel", "parallel", "arbitrary"), vmem_limit_bytes=VMEM_LIMIT_BYTES),
        name="diff_attn_prompt",
    )(p3, p3, p3, p3, lamp, sub)


SAMPLE_HEAD_GROUP = 4


def _diff_sample_kernel(q_ref, kn_ref, vn_ref, kc_ref, vc_ref, lamp_ref, sub_ref, o_ref, *, lam_init):
    tq = q_ref.shape[0]
    hd2 = 2 * DIFF_HD
    n_past = kc_ref.shape[1]
    lam = _lambda_value(lamp_ref[...], lam_init)
    for h0 in range(0, DIFF_HEADS, SAMPLE_HEAD_GROUP):
        heads = range(h0, h0 + SAMPLE_HEAD_GROUP)
        cols = [slice(h * hd2, (h + 1) * hd2) for h in heads]
        q2s = [_stack_maps(q_ref[:, c]) for c in cols]
        s_cs = [_dot(q2, kc_ref[c, :].astype(BF16)) for q2, c in zip(q2s, cols)]
        s_ns = [_dot_nt(q2, kn_ref[:, c]) for q2, c in zip(q2s, cols)]
        pcs, pns, ls = [], [], []
        for s_c, s_n in zip(s_cs, s_ns):
            m = jnp.maximum(jnp.max(s_c, axis=-1, keepdims=True), jnp.max(s_n, axis=-1, keepdims=True))
            p_c = jnp.exp2(s_c - m)
            p_n = jnp.exp2(s_n - m)
            ls.append(jnp.sum(p_c, axis=-1, keepdims=True) + jnp.sum(p_n, axis=-1, keepdims=True))
            pcs.append(p_c.astype(BF16))
            pns.append(p_n.astype(BF16))
        for h, c, p_c, p_n, l in zip(heads, cols, pcs, pns, ls):
            vc = vc_ref[pl.ds(h, n_past, stride=DIFF_HEADS), :]
            acc = _dot(p_c, vc.astype(BF16)) + _dot(p_n, vn_ref[:, c])
            o_ref[:, c] = _diff_finish(acc, l, lam, sub_ref[...], lam_init, tq).astype(o_ref.dtype)


def _diff_sample(p3, kc_t, vc, lamp, sub, lam_init):
    b, l, _ = p3.shape
    n_past = kc_t.shape[2]
    hd2 = 2 * DIFF_HD
    d = D_MODEL
    kern = functools.partial(_diff_sample_kernel, lam_init=lam_init)
    return pl.pallas_call(
        kern,
        grid=(b,),
        in_specs=[
            pl.BlockSpec((None, l, d), lambda bi: (bi, 0, T_QA)),
            pl.BlockSpec((None, l, d), lambda bi: (bi, 0, T_KA)),
            pl.BlockSpec((None, l, d), lambda bi: (bi, 0, T_VA)),
            pl.BlockSpec((None, d, n_past), lambda bi: (bi, 0, 0)),
            pl.BlockSpec((None, n_past * DIFF_HEADS, hd2), lambda bi: (bi, 0, 0)),
            pl.BlockSpec(lamp.shape, lambda bi: (0, 0)),
            pl.BlockSpec(sub.shape, lambda bi: (0, 0)),
        ],
        out_specs=pl.BlockSpec((None, l, d), lambda bi: (bi, 0, 0)),
        out_shape=jax.ShapeDtypeStruct((b, l, D_MODEL), BF16),
        compiler_params=pltpu.CompilerParams(
            dimension_semantics=("parallel",), vmem_limit_bytes=VMEM_LIMIT_BYTES),
        name="diff_attn_sample",
    )(p3, p3, p3, kc_t, vc, lamp, sub)


def _bcast_rows(x, period, row):
    r, c = x.shape
    x3 = x.reshape(r // period, period, c)
    return jnp.broadcast_to(x3[:, row:row + 1, :], x3.shape).reshape(r, c)


def _gla_kernel(*refs, rows, has_init):
    if has_init:
        q_ref, k_ref, v_ref, r_ref, g_ref, sub_ref, s0_ref, o_ref, sout_ref, st_scr, kf_scr, b_scr = refs
    else:
        q_ref, k_ref, v_ref, r_ref, g_ref, sub_ref, o_ref, sout_ref, st_scr, kf_scr, b_scr = refs
        s0_ref = None
    step = pl.program_id(2)
    n_chunks = rows // CHUNK
    n_sub = CHUNK // SUB_BLOCK

    @pl.when(step == 0)
    def _():
        if has_init:
            st_scr[...] = s0_ref[...]
        else:
            st_scr[...] = jnp.zeros(st_scr.shape, F32)

    q = q_ref[...].astype(F32)
    k = k_ref[...].astype(F32)
    g = g_ref[...]
    v = v_ref[...]
    chunk_rows = [slice(c * CHUNK, (c + 1) * CHUNK) for c in range(n_chunks)]

    ri = lax.broadcasted_iota(jnp.int32, (CHUNK, CHUNK), 0)
    ci = lax.broadcasted_iota(jnp.int32, (CHUNK, CHUNK), 1)
    tri = jnp.where(ci <= ri, 1.0, 0.0).astype(BF16)
    gw = jnp.concatenate([g[sl] for sl in chunk_rows], axis=1)
    g1 = gw.astype(BF16)
    rem = gw - g1.astype(F32)
    g2 = rem.astype(BF16)
    g3 = (rem - g2.astype(F32)).astype(BF16)
    bw = _dot(tri, g1) + _dot(tri, g2) + _dot(tri, g3)
    b = jnp.concatenate([bw[:, c * GLA_DK:(c + 1) * GLA_DK] for c in range(n_chunks)], axis=0)
    bex = b - g

    b_last = _bcast_rows(b, CHUNK, CHUNK - 1)
    b_blk = _bcast_rows(bex, SUB_BLOCK, 0)
    q_blk = q * jnp.exp(b - b_blk)
    q_chk = q * jnp.exp(b)
    k_end = k * jnp.exp(b_last - b)

    rowc = lax.broadcasted_iota(jnp.int32, (rows, GLA_DK), 0) % CHUNK
    zero = jnp.zeros_like(q)

    lhs_parts, rhs_parts = [], []
    for blk in range(1, n_sub):
        b_ref_blk = _bcast_rows(bex, CHUNK, blk * SUB_BLOCK)
        k_blk = k * jnp.exp(jnp.where(rowc < blk * SUB_BLOCK, b_ref_blk - b, NEG_BIG))
        lhs_parts.append(jnp.where((rowc // SUB_BLOCK) == blk, q_blk, zero))
        rhs_parts.append(k_blk)
    lhs = jnp.concatenate(lhs_parts, axis=-1).astype(BF16)
    rhs = jnp.concatenate(rhs_parts, axis=-1).astype(BF16)

    pad = SUB_BLOCK
    kf_scr[0:pad, :] = jnp.zeros((pad, GLA_DK), F32)
    b_scr[0:pad, :] = jnp.zeros((pad, GLA_DK), F32)
    kf_scr[pad:pad + rows, :] = k
    b_scr[pad:pad + rows, :] = b
    row_sub = rowc % SUB_BLOCK
    terms = []
    for d in range(SUB_BLOCK):
        kd = kf_scr[pad - d:pad - d + rows, :]
        bd = b_scr[pad - d:pad - d + rows, :]
        e = jnp.exp(jnp.where(row_sub >= d, b - bd, NEG_BIG))
        terms.append((q * kd * e).astype(BF16))
    sel_r = lax.broadcasted_iota(jnp.int32, (SUB_BLOCK * GLA_DK, LANES), 0) // GLA_DK
    sel_c = lax.broadcasted_iota(jnp.int32, (SUB_BLOCK * GLA_DK, LANES), 1)
    selector = jnp.where(sel_r + sel_c == SUB_BLOCK - 1, 1.0, 0.0).astype(BF16)
    diag_sums = _dot(jnp.concatenate(terms, axis=-1), selector)

    def skew(x):
        return pltpu.roll(x, LANES - (SUB_BLOCK - 1), 1, stride=1, stride_axis=0)

    atts = [(_dot_nt(lhs[sl], rhs[sl]) + skew(diag_sums[sl])[:, :CHUNK]).astype(BF16) for sl in chunk_rows]
    d_sts = [_dot(k_end[sl].T.astype(BF16), v[sl]) for sl in chunk_rows]
    o_intra = [_dot(att, v[sl]) for att, sl in zip(atts, chunk_rows)]
    decs = [jnp.exp(b[sl].T[:, CHUNK - 1:CHUNK]) for sl in chunk_rows]
    st = st_scr[...]
    outs = []
    for c, sl in enumerate(chunk_rows):
        outs.append(o_intra[c] + _dot(q_chk[sl].astype(BF16), st.astype(BF16)))
        st = st * decs[c] + d_sts[c]
    st_scr[...] = st
    o = jnp.concatenate(outs, axis=0) if n_chunks > 1 else outs[0]
    o_ref[...] = (_rms_rows(o, sub_ref[...]) * r_ref[...].astype(F32)).astype(o_ref.dtype)

    @pl.when(step == pl.num_programs(2) - 1)
    def _():
        sout_ref[...] = st_scr[...]


def _gla(p3, gk3, sub, s0, rows):
    b, s, _ = p3.shape
    has_init = s0 is not None
    kq = D_MODEL // GLA_DK
    kv = D_MODEL // GLA_DV
    in_specs = [
        pl.BlockSpec((None, rows, GLA_DK), lambda bi, h, r: (bi, r, T_QKB * kq + h)),
        pl.BlockSpec((None, rows, GLA_DK), lambda bi, h, r: (bi, r, T_QKB * kq + GLA_HEADS + h)),
        pl.BlockSpec((None, rows, GLA_DV), lambda bi, h, r: (bi, r, T_VB * kv + h)),
        pl.BlockSpec((None, rows, GLA_DV), lambda bi, h, r: (bi, r, T_RB * kv + h)),
        pl.BlockSpec((None, rows, GLA_DK), lambda bi, h, r: (bi, r, h)),
        pl.BlockSpec(sub.shape, lambda bi, h, r: (0, 0)),
    ]
    args = [p3, p3, p3, p3, gk3, sub]
    if has_init:
        in_specs.append(pl.BlockSpec((None, None, GLA_DK, GLA_DV), lambda bi, h, r: (bi, h, 0, 0)))
        args.append(s0)
    kern = functools.partial(_gla_kernel, rows=rows, has_init=has_init)
    return pl.pallas_call(
        kern,
        grid=(b, GLA_HEADS, s // rows),
        in_specs=in_specs,
        out_specs=[
            pl.BlockSpec((None, rows, GLA_DV), lambda bi, h, r: (bi, r, h)),
            pl.BlockSpec((None, None, GLA_DK, GLA_DV), lambda bi, h, r: (bi, h, 0, 0)),
        ],
        out_shape=[
            jax.ShapeDtypeStruct((b, s, GLA_HEADS * GLA_DV), BF16),
            jax.ShapeDtypeStruct((b, GLA_HEADS, GLA_DK, GLA_DV), F32),
        ],
        scratch_shapes=[
            pltpu.VMEM((GLA_DK, GLA_DV), F32),
            pltpu.VMEM((SUB_BLOCK + rows, GLA_DK), F32),
            pltpu.VMEM((SUB_BLOCK + rows, GLA_DK), F32),
        ],
        compiler_params=pltpu.CompilerParams(
            dimension_semantics=("parallel", "parallel", "arbitrary"), vmem_limit_bytes=VMEM_LIMIT_BYTES),
        name="gla",
    )(*args)


def _mem_attn_kernel(q_ref, k_ref, v_ref, o_ref, *, tiled):
    def head(ref, h):
        if not tiled:
            return ref[:, h * MEM_HD:(h + 1) * MEM_HD]
        period = MEM_HEADS * MEM_HD // LANES
        n_tok = ref.shape[0] // period
        return jnp.concatenate([ref[pl.ds(half * MEM_HEADS + h, n_tok, stride=period), :]
                                for half in range(MEM_HD // LANES)], axis=1)

    heads = range(MEM_HEADS)
    scores = [_dot_nt(q_ref[:, h * MEM_HD:(h + 1) * MEM_HD], head(k_ref, h).astype(BF16)) for h in heads]
    probs, sums = [], []
    for s in scores:
        p = jnp.exp(s - jnp.max(s, axis=-1, keepdims=True))
        sums.append(jnp.sum(p, axis=-1, keepdims=True))
        probs.append(p.astype(BF16))
    outs = [_dot(p, head(v_ref, h).astype(BF16)) / l for h, p, l in zip(heads, probs, sums)]
    o_ref[...] = jnp.concatenate(outs, axis=-1).astype(o_ref.dtype)


def _mem_attn(p3, mk, mv, tq):
    b, s, _ = p3.shape
    d = D_MODEL
    kv_block = (None,) + mk.shape[1:]
    return pl.pallas_call(
        functools.partial(_mem_attn_kernel, tiled=mk.shape[2] == LANES),
        grid=(b, s // tq),
        in_specs=[
            pl.BlockSpec((None, tq, d), lambda bi, i: (bi, i, T_QM)),
            pl.BlockSpec(kv_block, lambda bi, i: (bi, 0, 0)),
            pl.BlockSpec(kv_block, lambda bi, i: (bi, 0, 0)),
        ],
        out_specs=pl.BlockSpec((None, tq, d), lambda bi, i: (bi, i, 0)),
        out_shape=jax.ShapeDtypeStruct((b, s, d), BF16),
        compiler_params=pltpu.CompilerParams(
            dimension_semantics=("parallel", "parallel"), vmem_limit_bytes=VMEM_LIMIT_BYTES),
        name="mem_attn",
    )(p3, mk, mv)


def _mix_kernel(x_ref, oa_ref, ob_ref, om_ref, ga_ref, gb_ref, gm_ref, wd_ref, wg_ref, wm_ref, wo_ref,
                y_ref):
    m = (ga_ref[...].astype(F32) * _dot(oa_ref[...], wd_ref[...])
         + gb_ref[...].astype(F32) * _dot(ob_ref[...], wg_ref[...])
         + gm_ref[...].astype(F32) * _dot(om_ref[...], wm_ref[...]))
    y_ref[...] = x_ref[...] + _dot(m.astype(BF16), wo_ref[...])


def _mix(x2, oa, ob, om, p2, wd, wg, wm, wo, tm):
    t = x2.shape[0]
    d = D_MODEL
    row = lambda i: (i, 0)
    full = lambda i: (0, 0)
    wspec = pl.BlockSpec((d, d), full, pipeline_mode=pl.Buffered(1))
    return pl.pallas_call(
        _mix_kernel,
        grid=(t // tm,),
        in_specs=[
            pl.BlockSpec((tm, d), row), pl.BlockSpec((tm, d), row), pl.BlockSpec((tm, d), row),
            pl.BlockSpec((tm, d), row),
            pl.BlockSpec((tm, d), lambda i: (i, T_GATE)),
            pl.BlockSpec((tm, d), lambda i: (i, T_GATE + 1)),
            pl.BlockSpec((tm, d), lambda i: (i, T_GATE + 2)),
            wspec, wspec, wspec, wspec,
        ],
        out_specs=pl.BlockSpec((tm, d), row),
        out_shape=jax.ShapeDtypeStruct((t, d), F32),
        compiler_params=pltpu.CompilerParams(
            dimension_semantics=("parallel",), vmem_limit_bytes=VMEM_LIMIT_BYTES),
        name="mix_out",
    )(x2, oa, ob, om, p2, p2, p2, wd, wg, wm, wo)


FFN_CHUNK = 256


def _ffn_kernel(x_ref, g_ref, wup_ref, cw_ref, cb_ref, wd_ref, cs_ref, y_ref, cso_ref,
                u_scr, carry_scr, gv_scr, *, n_seq, seq_rows, tiles_per_seq):
    i = pl.program_id(0)
    gap = SUBLANES
    stride = seq_rows + gap
    tail = CONV_W - 1
    n_chunks = D_FF // FFN_CHUNK
    x = x_ref[...]
    h = _rms_rows(x, g_ref[...]).astype(BF16)

    first = (i % tiles_per_seq) == 0
    for s in range(n_seq):
        base = s * stride

        @pl.when(first)
        def _():
            u_scr[base:base + gap, :] = jnp.zeros((gap, D_FF), F32)
            u_scr[base + gap - tail:base + gap, :] = cs_ref[s]

        @pl.when(jnp.logical_not(first))
        def _():
            u_scr[base:base + gap, :] = carry_scr[...]

    def up(c):
        cols = slice(c * FFN_CHUNK, (c + 1) * FFN_CHUNK)
        gate_cols = slice(D_FF + c * FFN_CHUNK, D_FF + (c + 1) * FFN_CHUNK)
        return _dot(h, wup_ref[:, cols]), _dot(h, wup_ref[:, gate_cols])

    def gated(c, u, vv):
        cols = slice(c * FFN_CHUNK, (c + 1) * FFN_CHUNK)
        cw = cw_ref[:, cols]
        outs = []
        for s in range(n_seq):
            base = s * stride + gap
            u_scr[base:base + seq_rows, cols] = u[s * seq_rows:(s + 1) * seq_rows]
            conv = cb_ref[:, cols]
            for j in range(CONV_W):
                off = base - tail + j
                conv = conv + cw[j:j + 1, :] * u_scr[off:off + seq_rows, cols]
            outs.append(conv)
        uc = jnp.concatenate(outs, axis=0) if n_seq > 1 else outs[0]
        gelu = 0.5 * uc * (1.0 + jnp.tanh(math.sqrt(2.0 / math.pi) * (uc + 0.044715 * (uc * uc * uc))))
        return (gelu * vv).astype(BF16)

    nxt = up(0)
    for c in range(n_chunks):
        cur = nxt
        if c + 1 < n_chunks:
            nxt = up(c + 1)
        gv_scr[:, c * FFN_CHUNK:(c + 1) * FFN_CHUNK] = gated(c, *cur)
    y_ref[...] = x + _dot(gv_scr[...], wd_ref[...])

    for s in range(n_seq):
        base = s * stride + gap
        cso_ref[s] = u_scr[base + seq_rows - tail:base + seq_rows, :]
    carry_scr[...] = u_scr[seq_rows:seq_rows + gap, :]


def _ffn(x2, g_ffn, w_up, conv_w, conv_b, w_down, conv_state, n_seq, seq_rows, tiles_per_seq):
    t = x2.shape[0]
    d = D_MODEL
    tm = n_seq * seq_rows
    nb = conv_state.shape[0]
    tail = CONV_W - 1
    kern = functools.partial(_ffn_kernel, n_seq=n_seq, seq_rows=seq_rows, tiles_per_seq=tiles_per_seq)
    y, tails = pl.pallas_call(
        kern,
        grid=(t // tm,),
        in_specs=[
            pl.BlockSpec((tm, d), lambda i: (i, 0)),
            _resident((1, d)),
            _resident(w_up.shape),
            _resident(conv_w.shape),
            _resident(conv_b.shape),
            _resident(w_down.shape),
            pl.BlockSpec((n_seq, tail, D_FF), lambda i: (i // tiles_per_seq, 0, 0)),
        ],
        out_specs=[
            pl.BlockSpec((tm, d), lambda i: (i, 0)),
            pl.BlockSpec((n_seq, tail, D_FF), lambda i: (i, 0, 0)),
        ],
        out_shape=[
            jax.ShapeDtypeStruct((t, d), F32),
            jax.ShapeDtypeStruct((nb * tiles_per_seq, tail, D_FF), F32),
        ],
        scratch_shapes=[
            pltpu.VMEM((n_seq * (seq_rows + SUBLANES), D_FF), F32),
            pltpu.VMEM((SUBLANES, D_FF), F32),
            pltpu.VMEM((tm, D_FF), BF16),
        ],
        compiler_params=pltpu.CompilerParams(
            dimension_semantics=("arbitrary",), vmem_limit_bytes=VMEM_LIMIT_BYTES),
        name="conv_ffn",
    )(x2, g_ffn, w_up, conv_w, conv_b, w_down, conv_state)
    return y, tails.reshape(nb, tiles_per_seq, tail, D_FF)[:, -1]


def _mem_cache_rows(c):
    b, n, h, hd = c.shape
    return c.reshape(b, n, h, hd // LANES, LANES).transpose(0, 1, 3, 2, 4).reshape(b, -1, LANES)


def _tile_gain(g, reps):
    return jnp.tile(g.astype(F32), reps).reshape(1, -1)


def _layer_weights(l, g_attn, w_in, w_gk2, b_gk, qn_diff, kn_diff, lam_q1, lam_k1, lam_q2, lam_k2,
                   subln_diff, subln_gla, g_mem, w_mem_kv, qn_mem, kn_mem, w_proj_diff, w_proj_gla,
                   w_proj_mem, w_out, g_ffn, w_up, conv_w, conv_b, w_down):
    d = D_MODEL
    w = w_in[l]
    lr0 = 6 * d
    wa = w[:, :lr0].astype(BF16)
    wb = w[:, lr0 + GK_RANK:].astype(BF16)
    wg1 = jnp.pad(w[:, lr0:lr0 + GK_RANK], ((0, 0), (0, LANES - GK_RANK))).astype(BF16)
    wg2 = jnp.pad(w_gk2[l], ((0, LANES - GK_RANK), (0, 0))).astype(BF16)
    return dict(
        g_attn=g_attn[l].reshape(1, d), wa=wa, wb=wb, wg1=wg1, wg2=wg2, bgk=b_gk[l].reshape(1, -1),
        qn_t=_tile_gain(qn_diff[l], d // DIFF_HD), kn_t=_tile_gain(kn_diff[l], d // DIFF_HD),
        qmn_t=_tile_gain(qn_mem[l], d // MEM_HD), knm_t=_tile_gain(kn_mem[l], d // MEM_HD),
        g64=_group_matrix(MXU_DIM, DIFF_HD), g256=_group_matrix(MEM_HD, MEM_HD),
        lamp=jnp.stack([lam_q1[l], lam_k1[l], lam_q2[l], lam_k2[l]]).astype(F32),
        sub_diff=subln_diff[l].reshape(1, -1), sub_gla=subln_gla[l].reshape(1, -1),
        g_mem=g_mem[l].reshape(1, d), w_mem_kv=w_mem_kv[l].astype(BF16),
        wd=w_proj_diff[l].astype(BF16), wg=w_proj_gla[l].astype(BF16), wm=w_proj_mem[l].astype(BF16),
        wo=w_out[l].astype(BF16), g_ffn=g_ffn[l].reshape(1, d), w_up=w_up[l].astype(BF16),
        conv_w=conv_w[l], conv_b=conv_b[l].reshape(1, -1), w_down=w_down[l].astype(BF16),
    )


def _group(x, wts, lam_init, mem_k, mem_v, past_k, past_v, gla_state, conv_state, prompt):
    b, s, d = x.shape
    t = b * s
    x2 = x.reshape(t, d)
    tm = _pick(t, 512)
    p2, ka, va, gk = _inproj(x2, wts["g_attn"], wts["wa"], wts["wb"], wts["wg1"], wts["wg2"], wts["bgk"],
                             wts["qn_t"], wts["kn_t"], wts["qmn_t"], wts["g64"], wts["g256"],
                             _pick(s, 256) if prompt else _pick(t, 256), s, prompt)
    if prompt:
        ka = ka.reshape(b, DIFF_HEADS, 2, DIFF_HD, s).transpose(0, 4, 1, 2, 3)
    else:
        ka = ka.reshape(b, s, DIFF_HEADS, 2, DIFF_HD)
    p3 = p2.reshape(b, s, N_PTILES * d)
    gk3 = gk.reshape(b, s, GLA_HEADS * GLA_DK)
    if prompt:
        oa = _diff_prompt(p3, wts["lamp"], wts["sub_diff"], lam_init, _pick(s, 1024), 256)
        ob, gla_new = _gla(p3, gk3, wts["sub_gla"], None, _pick(s, 1024))
    else:
        oa = _diff_sample(p3, past_k, past_v, wts["lamp"], wts["sub_diff"], lam_init)
        ob, gla_new = _gla(p3, gk3, wts["sub_gla"], gla_state, s)
    om = _mem_attn(p3, mem_k, mem_v, _pick(s, 512))
    x1 = _mix(x2, oa.reshape(t, d), ob.reshape(t, d), om.reshape(t, d), p2,
              wts["wd"], wts["wg"], wts["wm"], wts["wo"], tm)
    if prompt:
        rows = _pick(s, 512)
        y, cs = _ffn(x1, wts["g_ffn"], wts["w_up"], wts["conv_w"], wts["conv_b"], wts["w_down"],
                     conv_state, 1, rows, s // rows)
    else:
        y, cs = _ffn(x1, wts["g_ffn"], wts["w_up"], wts["conv_w"], wts["conv_b"], wts["w_down"],
                     conv_state, b, s, 1)
    return y.reshape(b, s, d), ka, va, gla_new, cs


def kernel(x_prompt, x_sample, mem_prompt, cache_diff_k, cache_diff_v, cache_mem_k, cache_mem_v, state_gla, state_conv, g_attn, w_in, w_gk2, b_gk, qn_diff, kn_diff, lam_q1, lam_k1, lam_q2, lam_k2, subln_diff, subln_gla, g_mem, w_mem_kv, qn_mem, kn_mem, w_proj_diff, w_proj_gla, w_proj_mem, w_out, g_ffn, w_up, conv_w, conv_b, w_down):
    depth = g_attn.shape[0]
    d = D_MODEL
    xp, xs = x_prompt, x_sample
    bp, sp, _ = xp.shape
    bs, ss, _ = xs.shape
    n_mem = mem_prompt.shape[1]
    outs = [[] for _ in range(10)]
    for l in range(depth):
        lam_init = 0.8 - 0.6 * math.exp(-0.3 * l)
        wts = _layer_weights(l, g_attn, w_in, w_gk2, b_gk, qn_diff, kn_diff, lam_q1, lam_k1, lam_q2,
                             lam_k2, subln_diff, subln_gla, g_mem, w_mem_kv, qn_mem, kn_mem,
                             w_proj_diff, w_proj_gla, w_proj_mem, w_out, g_ffn, w_up, conv_w, conv_b,
                             w_down)
        mk, mv = _memkv(mem_prompt.reshape(bp * n_mem, d), wts["g_mem"], wts["w_mem_kv"], wts["knm_t"],
                        wts["g256"])
        mk = mk.reshape(bp, n_mem, d)
        mv = mv.reshape(bp, n_mem, d)
        xp, kp, vp, gp, cp = _group(xp, wts, lam_init, mk, mv, None, None, None,
                                    jnp.zeros((bp, CONV_W - 1, D_FF), F32), True)
        xs, ks_, vs_, gs, cs = _group(
            xs, wts, lam_init, _mem_cache_rows(cache_mem_k[l]), _mem_cache_rows(cache_mem_v[l]),
            cache_diff_k[l].transpose(0, 2, 3, 4, 1).reshape(bs, d, -1), cache_diff_v[l].reshape(bs, -1, 2 * DIFF_HD),
            state_gla[l], state_conv[l], False)
        vals = (kp, vp.reshape(bp, sp, DIFF_HEADS, 2 * DIFF_HD),
                mk.reshape(bp, n_mem, MEM_HEADS, MEM_HD), mv.reshape(bp, n_mem, MEM_HEADS, MEM_HD), gp, cp,
                ks_, vs_.reshape(bs, ss, DIFF_HEADS, 2 * DIFF_HD), gs, cs)
        for o, v in zip(outs, vals):
            o.append(v)
    return (xp, xs) + tuple(jnp.stack(o) for o in outs)
```

```python
import functools
import math

import jax
import jax.numpy as jnp
from jax import lax
from jax.experimental import pallas as pl
from jax.experimental.pallas import tpu as pltpu

F32 = jnp.float32
BF16 = jnp.bfloat16

D_MODEL = 1024
CHUNK = 64
EPS = 1e-6
DIFF_HEADS = 8
DIFF_HD = 64
DIFF_SCALE = DIFF_HD ** -0.5
LOG2E = math.log2(math.e)
GLA_HEADS = 4
GLA_DK = 128
GLA_DV = 256
GLA_SCALE = GLA_DK ** -0.5
GK_RANK = 16
GK_NORM = 16.0
MEM_HEADS = 4
MEM_HD = 256
MEM_SCALE = MEM_HD ** -0.5
D_FF = 2816
CONV_W = 3

LANES = 128
SUBLANES = 8
MXU_DIM = 256
VMEM_LIMIT_BYTES = 48 * 1024 * 1024

T_QA, T_KA, T_VA, T_QKB, T_VB, T_RB, T_QM, T_GATE = 0, 1, 2, 3, 4, 5, 6, 7
N_PTILES = 10
SUB_BLOCK = 8
NEG_BIG = -1e30


def _dot(a, b):
    return jnp.dot(a, b, preferred_element_type=F32)


def _dot_nt(a, b):
    return lax.dot_general(a, b, (((1,), (1,)), ((), ())), preferred_element_type=F32)


def _sigmoid(x):
    return 1.0 / (1.0 + jnp.exp(-x))


def _pick(n, pref):
    t = min(n, pref)
    while n % t:
        t -= 1
    return t


def _rms_rows(x, gain):
    ms = jnp.mean(x * x, axis=-1, keepdims=True)
    return x * lax.rsqrt(ms + EPS) * gain


def _group_rms(y, gmat, gain):
    slab = gmat.shape[0]
    outs = []
    for c in range(y.shape[-1] // slab):
        ys = y[:, c * slab:(c + 1) * slab]
        ms = _dot((ys * ys).astype(BF16), gmat)
        outs.append(ys * lax.rsqrt(ms + EPS))
    return jnp.concatenate(outs, axis=-1) * gain


def _group_matrix(slab, group):
    r = jnp.arange(slab) // group
    return jnp.where(r[:, None] == r[None, :], 1.0 / group, 0.0).astype(BF16)


def _inproj_kernel(x_ref, g_ref, wa_ref, wb_ref, wg1_ref, wg2_ref, bgk_ref, qn_ref, kn_ref, qmn_ref,
                   g64_ref, g256_ref, p_ref, ka_ref, vaf_ref, gk_ref, *, k_transposed):
    d = D_MODEL
    n_a = wa_ref.shape[1] // d
    h = _rms_rows(x_ref[...], g_ref[...]).astype(BF16)

    def project(j):
        if j < n_a:
            return _dot(h, wa_ref[:, j * d:(j + 1) * d])
        return _dot(h, wb_ref[:, (j - n_a) * d:(j - n_a + 1) * d])

    def finish(j, acc):
        if j == T_QA:
            out = _group_rms(acc, g64_ref[...], qn_ref[...]) * (DIFF_SCALE * LOG2E)
        elif j == T_KA:
            out = _group_rms(acc, g64_ref[...], kn_ref[...])
            ka_ref[...] = out.T if k_transposed else out
        elif j == T_VA:
            vaf_ref[...] = acc
            out = acc
        elif j == T_QKB:
            half = d // 2
            out = jnp.concatenate([acc[:, :half] * GLA_SCALE, acc[:, half:]], axis=-1)
            lr = _dot(h, wg1_ref[...])
            z = _dot(lr.astype(BF16), wg2_ref[...]) + bgk_ref[...]
            log_sig = jnp.minimum(z, 0.0) - jnp.log(1.0 + jnp.exp(-jnp.abs(z)))
            gk_ref[...] = log_sig * (1.0 / GK_NORM)
        elif j == T_VB:
            out = acc
        elif j == T_RB:
            out = acc * _sigmoid(acc)
        elif j == T_QM:
            out = _group_rms(acc, g256_ref[...], qmn_ref[...]) * MEM_SCALE
        else:
            out = _sigmoid(acc)
        p_ref[:, j * d:(j + 1) * d] = out.astype(BF16)

    acc_next = project(0)
    for j in range(N_PTILES):
        acc = acc_next
        if j + 1 < N_PTILES:
            acc_next = project(j + 1)
        finish(j, acc)


def _resident(shape):
    return pl.BlockSpec(shape, lambda *_: (0,) * len(shape), pipeline_mode=pl.Buffered(1))


def _inproj(x2, g_attn, wa, wb, wg1, wg2, bgk, qn_t, kn_t, qmn_t, g64, g256, tm, seq_len, k_transposed):
    t = x2.shape[0]
    d = D_MODEL
    row = lambda i: (i, 0)
    if k_transposed:
        per_seq = seq_len // tm
        ka_spec = pl.BlockSpec((None, d, tm), lambda i: (i // per_seq, 0, i % per_seq))
        ka_shape = jax.ShapeDtypeStruct((t // seq_len, d, seq_len), F32)
    else:
        ka_spec = pl.BlockSpec((tm, d), row)
        ka_shape = jax.ShapeDtypeStruct((t, d), F32)
    small = [wg1, wg2, bgk, qn_t, kn_t, qmn_t, g64, g256]
    return pl.pallas_call(
        functools.partial(_inproj_kernel, k_transposed=k_transposed),
        grid=(t // tm,),
        in_specs=[pl.BlockSpec((tm, d), row), _resident((1, d)), _resident(wa.shape), _resident(wb.shape)]
        + [_resident(a.shape) for a in small],
        out_specs=[
            pl.BlockSpec((tm, N_PTILES * d), row),
            ka_spec,
            pl.BlockSpec((tm, d), row),
            pl.BlockSpec((tm, GLA_HEADS * GLA_DK), row),
        ],
        out_shape=[
            jax.ShapeDtypeStruct((t, N_PTILES * d), BF16),
            ka_shape,
            jax.ShapeDtypeStruct((t, d), F32),
            jax.ShapeDtypeStruct((t, GLA_HEADS * GLA_DK), F32),
        ],
        compiler_params=pltpu.CompilerParams(
            dimension_semantics=("parallel",), vmem_limit_bytes=VMEM_LIMIT_BYTES),
        name="inproj",
    )(x2, g_attn, wa, wb, *small)


def _memkv_kernel(m_ref, g_ref, w_ref, kn_ref, g256_ref, k_ref, v_ref):
    j = pl.program_id(0)
    h = _rms_rows(m_ref[...], g_ref[...]).astype(BF16)
    acc = _dot(h, w_ref[...])

    @pl.when(j == 0)
    def _():
        k_ref[...] = _group_rms(acc, g256_ref[...], kn_ref[...])

    @pl.when(j == 1)
    def _():
        v_ref[...] = acc


def _memkv(mem2, g_mem, w_kv, knm_t, g256):
    t = mem2.shape[0]
    d = D_MODEL
    full = lambda j: (0, 0)
    return pl.pallas_call(
        _memkv_kernel,
        grid=(2,),
        in_specs=[
            pl.BlockSpec((t, d), full),
            pl.BlockSpec((1, d), full),
            pl.BlockSpec((d, d), lambda j: (0, j)),
            pl.BlockSpec((1, d), full),
            pl.BlockSpec(g256.shape, full),
        ],
        out_specs=[pl.BlockSpec((t, d), full), pl.BlockSpec((t, d), full)],
        out_shape=[jax.ShapeDtypeStruct((t, d), F32), jax.ShapeDtypeStruct((t, d), F32)],
        compiler_params=pltpu.CompilerParams(
            dimension_semantics=("arbitrary",), vmem_limit_bytes=VMEM_LIMIT_BYTES),
        name="memkv",
    )(mem2, g_mem, w_kv, knm_t, g256)


def _lambda_value(lamp, lam_init):
    a = jnp.sum(lamp[0:1, :] * lamp[1:2, :], axis=-1, keepdims=True)
    b = jnp.sum(lamp[2:3, :] * lamp[3:4, :], axis=-1, keepdims=True)
    return jnp.exp(a) - jnp.exp(b) + lam_init


def _stack_maps(q):
    lane = lax.broadcasted_iota(jnp.int32, q.shape, 1)
    zero = jnp.zeros_like(q)
    return jnp.concatenate([jnp.where(lane < DIFF_HD, q, zero), jnp.where(lane >= DIFF_HD, q, zero)], axis=0)


def _diff_finish(acc, l, lam, sub, lam_init, tq):
    o2 = acc / l
    o = o2[:tq] - lam * o2[tq:]
    return _rms_rows(o, sub) * (1.0 - lam_init)


ONES_ROWS = 16
LOOP_TRIPS = 8
XPOSE_ROWS = 512


def _diff_prompt_kernel(q_ref, qn_ref, k_ref, v_ref, lamp_ref, sub_ref, o_ref, qt_scr, vt_scr, m_scr, acc_scr,
                        st0_scr, st1_scr, p0_scr, p1_scr, alpha0_scr, alpha1_scr, mx0_scr, mx1_scr,
                        qtn_scr, stn_scr, mxn_scr, *more_scr, tq, tk, lam_init):
    qi = pl.program_id(2)
    n_kv = v_ref.shape[0] // tk
    hd2 = v_ref.shape[1]
    halves = tq // tk
    assert halves % 2 == 0
    all_cgs = list(range(2 * halves))

    @pl.when(qi == 0)
    def _():
        for c in range(n_kv):
            vt_scr[c, 0:hd2, :] = v_ref[c * tk:(c + 1) * tk, :].astype(F32).T.astype(BF16)
            vt_scr[c, hd2:hd2 + ONES_ROWS, :] = jnp.ones((ONES_ROWS, tk), BF16)

    def transposed_queries(ref):
        parts = []
        for r0 in range(0, tq, XPOSE_ROWS):
            parts.append(_stack_maps(ref[r0:r0 + XPOSE_ROWS, :]).astype(F32).T.astype(BF16))
        return jnp.concatenate([p[:, :XPOSE_ROWS] for p in parts] + [p[:, XPOSE_ROWS:] for p in parts], axis=1)

    @pl.when(qi == 0)
    def _():
        qt_scr[...] = transposed_queries(q_ref)

    @pl.when(qi > 0)
    def _():
        qt_scr[...] = qtn_scr[...]

    m_scr[...] = jnp.full(m_scr.shape, -jnp.inf, F32)
    acc_scr[...] = jnp.zeros(acc_scr.shape, F32)

    def scores(kb, cgs, qt=qt_scr):
        k = k_ref[pl.ds(pl.multiple_of(kb * tk, tk), tk), :]
        sts = [_dot(k, qt[:, cg * tk:(cg + 1) * tk]) for cg in cgs]
        return sts, [jnp.max(st, axis=0, keepdims=True) for st in sts]

    def softmax(sts, maxes, slot, cgs, masked):
        pts, alphas = [], []
        for st, mx, cg, msk in zip(sts, maxes, cgs, masked):
            cols = slice(cg * tk, (cg + 1) * tk)
            if msk:
                r = lax.broadcasted_iota(jnp.int32, st.shape, 0)
                c = lax.broadcasted_iota(jnp.int32, st.shape, 1)
                st = jnp.where((r // CHUNK) <= (c // CHUNK), st, -jnp.inf)
                mx = jnp.max(st, axis=0, keepdims=True)
            m_prev = m_scr[slot, :, cols]
            m_new = jnp.maximum(m_prev, mx)
            alphas.append(jnp.exp2(m_prev - m_new))
            pts.append(jnp.exp2(st - m_new).astype(BF16))
            m_scr[slot, :, cols] = m_new
        return pts, alphas

    def accumulate(kb, slot, cgs, pts, alphas):
        vt = vt_scr[kb]
        for pt, alpha, cg in zip(pts, alphas, cgs):
            cols = slice(cg * tk, (cg + 1) * tk)
            acc_scr[slot, :, cols] = alpha * acc_scr[slot, :, cols] + _dot(vt, pt)

    st_scr, mx_scr = (st0_scr, st1_scr), (mx0_scr, mx1_scr)
    n_more = LOOP_TRIPS - 2
    p_scr = (p0_scr, p1_scr) + tuple(more_scr[:n_more])
    alpha_scr = (alpha0_scr, alpha1_scr) + tuple(more_scr[n_more:])
    last = LOOP_TRIPS - 1

    def put(scr, buf, cgs, vals):
        for cg, val in zip(cgs, vals):
            scr[buf][cg] = val

    def get(scr, buf, cgs):
        return [scr[buf][cg] for cg in cgs]

    def put_scores(buf, cgs, sts_maxes):
        put(st_scr, buf, cgs, sts_maxes[0])
        put(mx_scr, buf, cgs, sts_maxes[1])

    def trip(t, u, prev=None):
        par = u % 2
        prev = (u - 1) % LOOP_TRIPS if prev is None else prev
        pts, alphas = softmax(get(st_scr, par, all_cgs), get(mx_scr, par, all_cgs), par, all_cgs,
                              [False] * len(all_cgs))
        put(p_scr, u, all_cgs, pts)
        put(alpha_scr, u, all_cgs, alphas)
        put_scores(1 - par, all_cgs, scores(jnp.minimum(t + 1, n_kv - 1), all_cgs))
        accumulate(jnp.maximum(t - 1, 0), 1 - par, all_cgs,
                   get(p_scr, prev, all_cgs), get(alpha_scr, prev, all_cgs))

    def loop_body(i, carry):
        for u in range(LOOP_TRIPS):
            trip(LOOP_TRIPS * i + u, u)
        return carry

    n_trips = qi * halves

    @pl.when(qi == 0)
    def _():
        put_scores(0, all_cgs, scores(0, all_cgs))

    @pl.when(qi > 0)
    def _():
        put_scores(0, all_cgs, ([stn_scr[cg] for cg in all_cgs], [mxn_scr[cg] for cg in all_cgs]))

    put(p_scr, last, all_cgs, [jnp.zeros((tk, tk), BF16)] * len(all_cgs))
    put(alpha_scr, last, all_cgs, [jnp.ones((1, tk), F32)] * len(all_cgs))
    lax.fori_loop(0, n_trips // LOOP_TRIPS, loop_body, 0)
    half_body = LOOP_TRIPS // 2
    assert half_body % 2 == 0 and halves % half_body == 0

    if halves % LOOP_TRIPS:
        @pl.when(n_trips % LOOP_TRIPS == half_body)
        def _():
            for u in range(half_body, LOOP_TRIPS):
                trip(n_trips - LOOP_TRIPS + u, u, prev=last if u == half_body else None)

    kb0 = n_trips

    def visible(j):
        return [cg for cg in all_cgs if cg % halves >= j]

    later = scores(kb0 + 1, visible(1))
    qtn_scr[...] = transposed_queries(qn_ref)
    nxt = scores(0, all_cgs, qt=qtn_scr)
    for cg in all_cgs:
        stn_scr[cg] = nxt[0][cg]
        mxn_scr[cg] = nxt[1][cg]
    accumulate(jnp.maximum(kb0 - 1, 0), 1, all_cgs, get(p_scr, last, all_cgs), get(alpha_scr, last, all_cgs))
    cur = (get(st_scr, 0, all_cgs), get(mx_scr, 0, all_cgs))
    for j in range(halves):
        cgs_j = visible(j)
        pts, alphas = softmax(cur[0], cur[1], j % 2, cgs_j, [cg % halves == j for cg in cgs_j])
        cur = later
        if j + 2 < halves:
            later = scores(kb0 + j + 2, visible(j + 2))
        accumulate(kb0 + j, j % 2, cgs_j, pts, alphas)

    m_all = jnp.maximum(m_scr[0], m_scr[1])
    acc_all = jnp.exp2(m_scr[0] - m_all) * acc_scr[0] + jnp.exp2(m_scr[1] - m_all) * acc_scr[1]

    lam = _lambda_value(lamp_ref[...], lam_init)
    o2t = acc_all[0:hd2, :] / acc_all[hd2:hd2 + 1, :]
    ot = o2t[:, :tq] - lam * o2t[:, tq:]
    for r0 in range(0, tq, XPOSE_ROWS):
        o = ot[:, r0:r0 + XPOSE_ROWS].T
        o_ref[r0:r0 + XPOSE_ROWS, :] = (_rms_rows(o, sub_ref[...]) * (1.0 - lam_init)).astype(o_ref.dtype)


def _diff_prompt(p3, lamp, sub, lam_init, tq, tk):
    b, s, _ = p3.shape
    hd2 = 2 * DIFF_HD
    kern = functools.partial(_diff_prompt_kernel, tq=tq, tk=tk, lam_init=lam_init)
    per_d = D_MODEL // hd2
    n_cg = 2 * tq // tk
    return pl.pallas_call(
        kern,
        grid=(b, DIFF_HEADS, s // tq),
        in_specs=[
            pl.BlockSpec((None, tq, hd2), lambda bi, h, qi: (bi, qi, T_QA * per_d + h)),
            pl.BlockSpec((None, tq, hd2), lambda bi, h, qi: (bi, jnp.minimum(qi + 1, s // tq - 1), T_QA * per_d + h)),
            pl.BlockSpec((None, s, hd2), lambda bi, h, qi: (bi, 0, T_KA * per_d + h)),
            pl.BlockSpec((None, s, hd2), lambda bi, h, qi: (bi, 0, T_VA * per_d + h)),
            pl.BlockSpec(lamp.shape, lambda bi, h, qi: (0, 0)),
            pl.BlockSpec(sub.shape, lambda bi, h, qi: (0, 0)),
        ],
        out_specs=pl.BlockSpec((None, tq, hd2), lambda bi, h, qi: (bi, qi, h)),
        out_shape=jax.ShapeDtypeStruct((b, s, D_MODEL), BF16),
        scratch_shapes=[
            pltpu.VMEM((hd2, 2 * tq), BF16),
            pltpu.VMEM((s // tk, hd2 + ONES_ROWS, tk), BF16),
            pltpu.VMEM((2, 1, 2 * tq), F32),
            pltpu.VMEM((2, hd2 + ONES_ROWS, 2 * tq), F32),
            pltpu.VMEM((n_cg, tk, tk), F32),
            pltpu.VMEM((n_cg, tk, tk), F32),
            pltpu.VMEM((n_cg, tk, tk), BF16),
            pltpu.VMEM((n_cg, tk, tk), BF16),
            pltpu.VMEM((n_cg, 1, tk), F32),
            pltpu.VMEM((n_cg, 1, tk), F32),
            pltpu.VMEM((n_cg, 1, tk), F32),
            pltpu.VMEM((n_cg, 1, tk), F32),
            pltpu.VMEM((hd2, 2 * tq), BF16),
            pltpu.VMEM((n_cg, tk, tk), F32),
            pltpu.VMEM((n_cg, 1, tk), F32),
        ] + [pltpu.VMEM((n_cg, tk, tk), BF16)] * (LOOP_TRIPS - 2) + [pltpu.VMEM((n_cg, 1, tk), F32)] * (LOOP_TRIPS - 2),
        compiler_params=pltpu.CompilerParams(
            dimension_semantics=("parallel", "parallel", "arbitrary"), vmem_limit_bytes=VMEM_LIMIT_BYTES),
        name="diff_attn_prompt",
    )(p3, p3, p3, p3, lamp, sub)


SAMPLE_HEAD_GROUP = 4


def _diff_sample_kernel(q_ref, kn_ref, vn_ref, kc_ref, vc_ref, lamp_ref, sub_ref, o_ref, *, lam_init):
    tq = q_ref.shape[0]
    hd2 = 2 * DIFF_HD
    n_past = kc_ref.shape[1]
    lam = _lambda_value(lamp_ref[...], lam_init)
    first_head = pl.program_id(1) * SAMPLE_HEAD_GROUP
    for _ in range(1):
        heads = [first_head + i for i in range(SAMPLE_HEAD_GROUP)]
        cols = [slice(i * hd2, (i + 1) * hd2) for i in range(SAMPLE_HEAD_GROUP)]
        q2s = [_stack_maps(q_ref[:, c]) for c in cols]
        s_cs = [_dot(q2, kc_ref[c, :].astype(BF16)) for q2, c in zip(q2s, cols)]
        s_ns = [_dot_nt(q2, kn_ref[:, c]) for q2, c in zip(q2s, cols)]
        pcs, pns, ls = [], [], []
        for s_c, s_n in zip(s_cs, s_ns):
            m = jnp.maximum(jnp.max(s_c, axis=-1, keepdims=True), jnp.max(s_n, axis=-1, keepdims=True))
            p_c = jnp.exp2(s_c - m)
            p_n = jnp.exp2(s_n - m)
            ls.append(jnp.sum(p_c, axis=-1, keepdims=True) + jnp.sum(p_n, axis=-1, keepdims=True))
            pcs.append(p_c.astype(BF16))
            pns.append(p_n.astype(BF16))
        for h, c, p_c, p_n, l in zip(heads, cols, pcs, pns, ls):
            vc = vc_ref[pl.ds(h, n_past, stride=DIFF_HEADS), :]
            acc = _dot(p_c, vc.astype(BF16)) + _dot(p_n, vn_ref[:, c])
            o_ref[:, c] = _diff_finish(acc, l, lam, sub_ref[...], lam_init, tq).astype(o_ref.dtype)


def _diff_sample(p3, kc_t, vc, lamp, sub, lam_init):
    b, l, _ = p3.shape
    n_past = kc_t.shape[2]
    hd2 = 2 * DIFF_HD
    n_groups = DIFF_HEADS // SAMPLE_HEAD_GROUP
    gw = SAMPLE_HEAD_GROUP * hd2
    kern = functools.partial(_diff_sample_kernel, lam_init=lam_init)
    return pl.pallas_call(
        kern,
        grid=(b, n_groups),
        in_specs=[
            pl.BlockSpec((None, l, gw), lambda bi, g: (bi, 0, T_QA * n_groups + g)),
            pl.BlockSpec((None, l, gw), lambda bi, g: (bi, 0, T_KA * n_groups + g)),
            pl.BlockSpec((None, l, gw), lambda bi, g: (bi, 0, T_VA * n_groups + g)),
            pl.BlockSpec((None, gw, n_past), lambda bi, g: (bi, g, 0)),
            pl.BlockSpec((None, n_past * DIFF_HEADS, hd2), lambda bi, g: (bi, 0, 0)),
            pl.BlockSpec(lamp.shape, lambda bi, g: (0, 0)),
            pl.BlockSpec(sub.shape, lambda bi, g: (0, 0)),
        ],
        out_specs=pl.BlockSpec((None, l, gw), lambda bi, g: (bi, 0, g)),
        out_shape=jax.ShapeDtypeStruct((b, l, D_MODEL), BF16),
        compiler_params=pltpu.CompilerParams(
            dimension_semantics=("parallel", "arbitrary"), vmem_limit_bytes=VMEM_LIMIT_BYTES),
        name="diff_attn_sample",
    )(p3, p3, p3, kc_t, vc, lamp, sub)


def _bcast_rows(x, period, row):
    r, c = x.shape
    x3 = x.reshape(r // period, period, c)
    return jnp.broadcast_to(x3[:, row:row + 1, :], x3.shape).reshape(r, c)


def _gla_kernel(*refs, rows, has_init):
    if has_init:
        q_ref, k_ref, v_ref, r_ref, g_ref, sub_ref, s0_ref, o_ref, sout_ref, st_scr, kf_scr, b_scr = refs
    else:
        q_ref, k_ref, v_ref, r_ref, g_ref, sub_ref, o_ref, sout_ref, st_scr, kf_scr, b_scr = refs
        s0_ref = None
    step = pl.program_id(2)
    n_chunks = rows // CHUNK
    n_sub = CHUNK // SUB_BLOCK

    @pl.when(step == 0)
    def _():
        if has_init:
            st_scr[...] = s0_ref[...]
        else:
            st_scr[...] = jnp.zeros(st_scr.shape, F32)

    q = q_ref[...].astype(F32)
    k = k_ref[...].astype(F32)
    g = g_ref[...]
    v = v_ref[...]
    chunk_rows = [slice(c * CHUNK, (c + 1) * CHUNK) for c in range(n_chunks)]

    ri = lax.broadcasted_iota(jnp.int32, (CHUNK, CHUNK), 0)
    ci = lax.broadcasted_iota(jnp.int32, (CHUNK, CHUNK), 1)
    tri = jnp.where(ci <= ri, 1.0, 0.0).astype(BF16)
    gw = jnp.concatenate([g[sl] for sl in chunk_rows], axis=1)
    g1 = gw.astype(BF16)
    rem = gw - g1.astype(F32)
    g2 = rem.astype(BF16)
    g3 = (rem - g2.astype(F32)).astype(BF16)
    bw = _dot(tri, g1) + _dot(tri, g2) + _dot(tri, g3)
    b = jnp.concatenate([bw[:, c * GLA_DK:(c + 1) * GLA_DK] for c in range(n_chunks)], axis=0)
    bex = b - g

    b_last = _bcast_rows(b, CHUNK, CHUNK - 1)
    b_blk = _bcast_rows(bex, SUB_BLOCK, 0)
    q_blk = q * jnp.exp(b - b_blk)
    q_chk = q * jnp.exp(b)
    k_end = k * jnp.exp(b_last - b)

    rowc = lax.broadcasted_iota(jnp.int32, (rows, GLA_DK), 0) % CHUNK
    zero = jnp.zeros_like(q)

    lhs_parts, rhs_parts = [], []
    for blk in range(1, n_sub):
        b_ref_blk = _bcast_rows(bex, CHUNK, blk * SUB_BLOCK)
        k_blk = k * jnp.exp(jnp.where(rowc < blk * SUB_BLOCK, b_ref_blk - b, NEG_BIG))
        lhs_parts.append(jnp.where((rowc // SUB_BLOCK) == blk, q_blk, zero))
        rhs_parts.append(k_blk)
    lhs = jnp.concatenate(lhs_parts, axis=-1).astype(BF16)
    rhs = jnp.concatenate(rhs_parts, axis=-1).astype(BF16)

    pad = SUB_BLOCK
    kf_scr[0:pad, :] = jnp.zeros((pad, GLA_DK), F32)
    b_scr[0:pad, :] = jnp.zeros((pad, GLA_DK), F32)
    kf_scr[pad:pad + rows, :] = k
    b_scr[pad:pad + rows, :] = b
    row_sub = rowc % SUB_BLOCK
    terms = []
    for d in range(SUB_BLOCK):
        kd = kf_scr[pad - d:pad - d + rows, :]
        bd = b_scr[pad - d:pad - d + rows, :]
        e = jnp.exp(jnp.where(row_sub >= d, b - bd, NEG_BIG))
        terms.append((q * kd * e).astype(BF16))
    sel_r = lax.broadcasted_iota(jnp.int32, (SUB_BLOCK * GLA_DK, LANES), 0) // GLA_DK
    sel_c = lax.broadcasted_iota(jnp.int32, (SUB_BLOCK * GLA_DK, LANES), 1)
    selector = jnp.where(sel_r + sel_c == SUB_BLOCK - 1, 1.0, 0.0).astype(BF16)
    diag_sums = _dot(jnp.concatenate(terms, axis=-1), selector)

    def skew(x):
        return pltpu.roll(x, LANES - (SUB_BLOCK - 1), 1, stride=1, stride_axis=0)

    atts = [(_dot_nt(lhs[sl], rhs[sl]) + skew(diag_sums[sl])[:, :CHUNK]).astype(BF16) for sl in chunk_rows]
    d_sts = [_dot(k_end[sl].T.astype(BF16), v[sl]) for sl in chunk_rows]
    o_intra = [_dot(att, v[sl]) for att, sl in zip(atts, chunk_rows)]
    decs = [jnp.exp(b[sl].T[:, CHUNK - 1:CHUNK]) for sl in chunk_rows]
    st = st_scr[...]
    outs = []
    for c, sl in enumerate(chunk_rows):
        outs.append(o_intra[c] + _dot(q_chk[sl].astype(BF16), st.astype(BF16)))
        st = st * decs[c] + d_sts[c]
    st_scr[...] = st
    o = jnp.concatenate(outs, axis=0) if n_chunks > 1 else outs[0]
    o_ref[...] = (_rms_rows(o, sub_ref[...]) * r_ref[...].astype(F32)).astype(o_ref.dtype)

    @pl.when(step == pl.num_programs(2) - 1)
    def _():
        sout_ref[...] = st_scr[...]


def _gla(p3, gk3, sub, s0, rows):
    b, s, _ = p3.shape
    has_init = s0 is not None
    kq = D_MODEL // GLA_DK
    kv = D_MODEL // GLA_DV
    in_specs = [
        pl.BlockSpec((None, rows, GLA_DK), lambda bi, h, r: (bi, r, T_QKB * kq + h)),
        pl.BlockSpec((None, rows, GLA_DK), lambda bi, h, r: (bi, r, T_QKB * kq + GLA_HEADS + h)),
        pl.BlockSpec((None, rows, GLA_DV), lambda bi, h, r: (bi, r, T_VB * kv + h)),
        pl.BlockSpec((None, rows, GLA_DV), lambda bi, h, r: (bi, r, T_RB * kv + h)),
        pl.BlockSpec((None, rows, GLA_DK), lambda bi, h, r: (bi, r, h)),
        pl.BlockSpec(sub.shape, lambda bi, h, r: (0, 0)),
    ]
    args = [p3, p3, p3, p3, gk3, sub]
    if has_init:
        in_specs.append(pl.BlockSpec((None, None, GLA_DK, GLA_DV), lambda bi, h, r: (bi, h, 0, 0)))
        args.append(s0)
    kern = functools.partial(_gla_kernel, rows=rows, has_init=has_init)
    return pl.pallas_call(
        kern,
        grid=(b, GLA_HEADS, s // rows),
        in_specs=in_specs,
        out_specs=[
            pl.BlockSpec((None, rows, GLA_DV), lambda bi, h, r: (bi, r, h)),
            pl.BlockSpec((None, None, GLA_DK, GLA_DV), lambda bi, h, r: (bi, h, 0, 0)),
        ],
        out_shape=[
            jax.ShapeDtypeStruct((b, s, GLA_HEADS * GLA_DV), BF16),
            jax.ShapeDtypeStruct((b, GLA_HEADS, GLA_DK, GLA_DV), F32),
        ],
        scratch_shapes=[
            pltpu.VMEM((GLA_DK, GLA_DV), F32),
            pltpu.VMEM((SUB_BLOCK + rows, GLA_DK), F32),
            pltpu.VMEM((SUB_BLOCK + rows, GLA_DK), F32),
        ],
        compiler_params=pltpu.CompilerParams(
            dimension_semantics=("parallel", "parallel", "arbitrary"), vmem_limit_bytes=VMEM_LIMIT_BYTES),
        name="gla",
    )(*args)


def _mem_attn_kernel(q_ref, k_ref, v_ref, o_ref, *, tiled):
    def head(ref, h):
        if not tiled:
            return ref[:, h * MEM_HD:(h + 1) * MEM_HD]
        period = MEM_HEADS * MEM_HD // LANES
        n_tok = ref.shape[0] // period
        return jnp.concatenate([ref[pl.ds(half * MEM_HEADS + h, n_tok, stride=period), :]
                                for half in range(MEM_HD // LANES)], axis=1)

    heads = range(MEM_HEADS)
    scores = [_dot_nt(q_ref[:, h * MEM_HD:(h + 1) * MEM_HD], head(k_ref, h).astype(BF16)) for h in heads]
    probs, sums = [], []
    for s in scores:
        p = jnp.exp(s - jnp.max(s, axis=-1, keepdims=True))
        sums.append(jnp.sum(p, axis=-1, keepdims=True))
        probs.append(p.astype(BF16))
    outs = [_dot(p, head(v_ref, h).astype(BF16)) / l for h, p, l in zip(heads, probs, sums)]
    o_ref[...] = jnp.concatenate(outs, axis=-1).astype(o_ref.dtype)


def _mem_attn(p3, mk, mv, tq):
    b, s, _ = p3.shape
    d = D_MODEL
    kv_block = (None,) + mk.shape[1:]
    return pl.pallas_call(
        functools.partial(_mem_attn_kernel, tiled=mk.shape[2] == LANES),
        grid=(b, s // tq),
        in_specs=[
            pl.BlockSpec((None, tq, d), lambda bi, i: (bi, i, T_QM)),
            pl.BlockSpec(kv_block, lambda bi, i: (bi, 0, 0)),
            pl.BlockSpec(kv_block, lambda bi, i: (bi, 0, 0)),
        ],
        out_specs=pl.BlockSpec((None, tq, d), lambda bi, i: (bi, i, 0)),
        out_shape=jax.ShapeDtypeStruct((b, s, d), BF16),
        compiler_params=pltpu.CompilerParams(
            dimension_semantics=("parallel", "parallel"), vmem_limit_bytes=VMEM_LIMIT_BYTES),
        name="mem_attn",
    )(p3, mk, mv)


def _mix_kernel(x_ref, oa_ref, ob_ref, om_ref, ga_ref, gb_ref, gm_ref, wd_ref, wg_ref, wm_ref, wo_ref,
                y_ref):
    m = (ga_ref[...].astype(F32) * _dot(oa_ref[...], wd_ref[...])
         + gb_ref[...].astype(F32) * _dot(ob_ref[...], wg_ref[...])
         + gm_ref[...].astype(F32) * _dot(om_ref[...], wm_ref[...]))
    y_ref[...] = x_ref[...] + _dot(m.astype(BF16), wo_ref[...])


def _mix(x2, oa, ob, om, p2, wd, wg, wm, wo, tm):
    t = x2.shape[0]
    d = D_MODEL
    row = lambda i: (i, 0)
    full = lambda i: (0, 0)
    wspec = pl.BlockSpec((d, d), full, pipeline_mode=pl.Buffered(1))
    return pl.pallas_call(
        _mix_kernel,
        grid=(t // tm,),
        in_specs=[
            pl.BlockSpec((tm, d), row), pl.BlockSpec((tm, d), row), pl.BlockSpec((tm, d), row),
            pl.BlockSpec((tm, d), row),
            pl.BlockSpec((tm, d), lambda i: (i, T_GATE)),
            pl.BlockSpec((tm, d), lambda i: (i, T_GATE + 1)),
            pl.BlockSpec((tm, d), lambda i: (i, T_GATE + 2)),
            wspec, wspec, wspec, wspec,
        ],
        out_specs=pl.BlockSpec((tm, d), row),
        out_shape=jax.ShapeDtypeStruct((t, d), F32),
        compiler_params=pltpu.CompilerParams(
            dimension_semantics=("parallel",), vmem_limit_bytes=VMEM_LIMIT_BYTES),
        name="mix_out",
    )(x2, oa, ob, om, p2, p2, p2, wd, wg, wm, wo)


FFN_CHUNK = 256


def _ffn_kernel(x_ref, g_ref, wup_ref, cw_ref, cb_ref, wd_ref, cs_ref, y_ref, cso_ref,
                u_scr, carry_scr, gv_scr, *, n_seq, seq_rows, tiles_per_seq):
    i = pl.program_id(0)
    gap = SUBLANES
    stride = seq_rows + gap
    tail = CONV_W - 1
    n_chunks = D_FF // FFN_CHUNK
    x = x_ref[...]
    h = _rms_rows(x, g_ref[...]).astype(BF16)

    first = (i % tiles_per_seq) == 0
    for s in range(n_seq):
        base = s * stride

        @pl.when(first)
        def _():
            u_scr[base:base + gap, :] = jnp.zeros((gap, D_FF), F32)
            u_scr[base + gap - tail:base + gap, :] = cs_ref[s]

        @pl.when(jnp.logical_not(first))
        def _():
            u_scr[base:base + gap, :] = carry_scr[...]

    def up(c):
        cols = slice(c * FFN_CHUNK, (c + 1) * FFN_CHUNK)
        gate_cols = slice(D_FF + c * FFN_CHUNK, D_FF + (c + 1) * FFN_CHUNK)
        return _dot(h, wup_ref[:, cols]), _dot(h, wup_ref[:, gate_cols])

    def gated(c, u, vv):
        cols = slice(c * FFN_CHUNK, (c + 1) * FFN_CHUNK)
        cw = cw_ref[:, cols]
        outs = []
        for s in range(n_seq):
            base = s * stride + gap
            u_scr[base:base + seq_rows, cols] = u[s * seq_rows:(s + 1) * seq_rows]
            conv = cb_ref[:, cols]
            for j in range(CONV_W):
                off = base - tail + j
                conv = conv + cw[j:j + 1, :] * u_scr[off:off + seq_rows, cols]
            outs.append(conv)
        uc = jnp.concatenate(outs, axis=0) if n_seq > 1 else outs[0]
        gelu = 0.5 * uc * (1.0 + jnp.tanh(math.sqrt(2.0 / math.pi) * (uc + 0.044715 * (uc * uc * uc))))
        return (gelu * vv).astype(BF16)

    nxt = up(0)
    for c in range(n_chunks):
        cur = nxt
        if c + 1 < n_chunks:
            nxt = up(c + 1)
        gv_scr[:, c * FFN_CHUNK:(c + 1) * FFN_CHUNK] = gated(c, *cur)
    y_ref[...] = x + _dot(gv_scr[...], wd_ref[...])

    for s in range(n_seq):
        base = s * stride + gap
        cso_ref[s] = u_scr[base + seq_rows - tail:base + seq_rows, :]
    carry_scr[...] = u_scr[seq_rows:seq_rows + gap, :]


def _ffn(x2, g_ffn, w_up, conv_w, conv_b, w_down, conv_state, n_seq, seq_rows, tiles_per_seq):
    t = x2.shape[0]
    d = D_MODEL
    tm = n_seq * seq_rows
    nb = conv_state.shape[0]
    tail = CONV_W - 1
    kern = functools.partial(_ffn_kernel, n_seq=n_seq, seq_rows=seq_rows, tiles_per_seq=tiles_per_seq)
    y, tails = pl.pallas_call(
        kern,
        grid=(t // tm,),
        in_specs=[
            pl.BlockSpec((tm, d), lambda i: (i, 0)),
            _resident((1, d)),
            _resident(w_up.shape),
            _resident(conv_w.shape),
            _resident(conv_b.shape),
            _resident(w_down.shape),
            pl.BlockSpec((n_seq, tail, D_FF), lambda i: (i // tiles_per_seq, 0, 0)),
        ],
        out_specs=[
            pl.BlockSpec((tm, d), lambda i: (i, 0)),
            pl.BlockSpec((n_seq, tail, D_FF), lambda i: (i, 0, 0)),
        ],
        out_shape=[
            jax.ShapeDtypeStruct((t, d), F32),
            jax.ShapeDtypeStruct((nb * tiles_per_seq, tail, D_FF), F32),
        ],
        scratch_shapes=[
            pltpu.VMEM((n_seq * (seq_rows + SUBLANES), D_FF), F32),
            pltpu.VMEM((SUBLANES, D_FF), F32),
            pltpu.VMEM((tm, D_FF), BF16),
        ],
        compiler_params=pltpu.CompilerParams(
            dimension_semantics=("arbitrary",), vmem_limit_bytes=VMEM_LIMIT_BYTES),
        name="conv_ffn",
    )(x2, g_ffn, w_up, conv_w, conv_b, w_down, conv_state)
    return y, tails.reshape(nb, tiles_per_seq, tail, D_FF)[:, -1]


def _mem_cache_rows(c):
    b, n, h, hd = c.shape
    return c.reshape(b, n, h, hd // LANES, LANES).transpose(0, 1, 3, 2, 4).reshape(b, -1, LANES)


def _tile_gain(g, reps):
    return jnp.tile(g.astype(F32), reps).reshape(1, -1)


def _layer_weights(l, g_attn, w_in, w_gk2, b_gk, qn_diff, kn_diff, lam_q1, lam_k1, lam_q2, lam_k2,
                   subln_diff, subln_gla, g_mem, w_mem_kv, qn_mem, kn_mem, w_proj_diff, w_proj_gla,
                   w_proj_mem, w_out, g_ffn, w_up, conv_w, conv_b, w_down):
    d = D_MODEL
    w = w_in[l]
    lr0 = 6 * d
    wa = w[:, :lr0].astype(BF16)
    wb = w[:, lr0 + GK_RANK:].astype(BF16)
    wg1 = jnp.pad(w[:, lr0:lr0 + GK_RANK], ((0, 0), (0, LANES - GK_RANK))).astype(BF16)
    wg2 = jnp.pad(w_gk2[l], ((0, LANES - GK_RANK), (0, 0))).astype(BF16)
    return dict(
        g_attn=g_attn[l].reshape(1, d), wa=wa, wb=wb, wg1=wg1, wg2=wg2, bgk=b_gk[l].reshape(1, -1),
        qn_t=_tile_gain(qn_diff[l], d // DIFF_HD), kn_t=_tile_gain(kn_diff[l], d // DIFF_HD),
        qmn_t=_tile_gain(qn_mem[l], d // MEM_HD), knm_t=_tile_gain(kn_mem[l], d // MEM_HD),
        g64=_group_matrix(MXU_DIM, DIFF_HD), g256=_group_matrix(MEM_HD, MEM_HD),
        lamp=jnp.stack([lam_q1[l], lam_k1[l], lam_q2[l], lam_k2[l]]).astype(F32),
        sub_diff=subln_diff[l].reshape(1, -1), sub_gla=subln_gla[l].reshape(1, -1),
        g_mem=g_mem[l].reshape(1, d), w_mem_kv=w_mem_kv[l].astype(BF16),
        wd=w_proj_diff[l].astype(BF16), wg=w_proj_gla[l].astype(BF16), wm=w_proj_mem[l].astype(BF16),
        wo=w_out[l].astype(BF16), g_ffn=g_ffn[l].reshape(1, d), w_up=w_up[l].astype(BF16),
        conv_w=conv_w[l], conv_b=conv_b[l].reshape(1, -1), w_down=w_down[l].astype(BF16),
    )


def _group(x, wts, lam_init, mem_k, mem_v, past_k, past_v, gla_state, conv_state, prompt):
    b, s, d = x.shape
    t = b * s
    x2 = x.reshape(t, d)
    tm = _pick(t, 512)
    p2, ka, va, gk = _inproj(x2, wts["g_attn"], wts["wa"], wts["wb"], wts["wg1"], wts["wg2"], wts["bgk"],
                             wts["qn_t"], wts["kn_t"], wts["qmn_t"], wts["g64"], wts["g256"],
                             _pick(s, 256) if prompt else _pick(t, 256), s, prompt)
    if prompt:
        ka = ka.reshape(b, DIFF_HEADS, 2, DIFF_HD, s).transpose(0, 4, 1, 2, 3)
    else:
        ka = ka.reshape(b, s, DIFF_HEADS, 2, DIFF_HD)
    p3 = p2.reshape(b, s, N_PTILES * d)
    gk3 = gk.reshape(b, s, GLA_HEADS * GLA_DK)
    if prompt:
        oa = _diff_prompt(p3, wts["lamp"], wts["sub_diff"], lam_init, _pick(s, 1024), 256)
        ob, gla_new = _gla(p3, gk3, wts["sub_gla"], None, _pick(s, 1024))
    else:
        oa = _diff_sample(p3, past_k, past_v, wts["lamp"], wts["sub_diff"], lam_init)
        ob, gla_new = _gla(p3, gk3, wts["sub_gla"], gla_state, s)
    om = _mem_attn(p3, mem_k, mem_v, _pick(s, 512))
    x1 = _mix(x2, oa.reshape(t, d), ob.reshape(t, d), om.reshape(t, d), p2,
              wts["wd"], wts["wg"], wts["wm"], wts["wo"], tm)
    if prompt:
        rows = _pick(s, 512)
        y, cs = _ffn(x1, wts["g_ffn"], wts["w_up"], wts["conv_w"], wts["conv_b"], wts["w_down"],
                     conv_state, 1, rows, s // rows)
    else:
        y, cs = _ffn(x1, wts["g_ffn"], wts["w_up"], wts["conv_w"], wts["conv_b"], wts["w_down"],
                     conv_state, b, s, 1)
    return y.reshape(b, s, d), ka, va, gla_new, cs


def kernel(x_prompt, x_sample, mem_prompt, cache_diff_k, cache_diff_v, cache_mem_k, cache_mem_v, state_gla, state_conv, g_attn, w_in, w_gk2, b_gk, qn_diff, kn_diff, lam_q1, lam_k1, lam_q2, lam_k2, subln_diff, subln_gla, g_mem, w_mem_kv, qn_mem, kn_mem, w_proj_diff, w_proj_gla, w_proj_mem, w_out, g_ffn, w_up, conv_w, conv_b, w_down):
    depth = g_attn.shape[0]
    d = D_MODEL
    xp, xs = x_prompt, x_sample
    bp, sp, _ = xp.shape
    bs, ss, _ = xs.shape
    n_mem = mem_prompt.shape[1]
    outs = [[] for _ in range(10)]
    for l in range(depth):
        lam_init = 0.8 - 0.6 * math.exp(-0.3 * l)
        wts = _layer_weights(l, g_attn, w_in, w_gk2, b_gk, qn_diff, kn_diff, lam_q1, lam_k1, lam_q2,
                             lam_k2, subln_diff, subln_gla, g_mem, w_mem_kv, qn_mem, kn_mem,
                             w_proj_diff, w_proj_gla, w_proj_mem, w_out, g_ffn, w_up, conv_w, conv_b,
                             w_down)
        mk, mv = _memkv(mem_prompt.reshape(bp * n_mem, d), wts["g_mem"], wts["w_mem_kv"], wts["knm_t"],
                        wts["g256"])
        mk = mk.reshape(bp, n_mem, d)
        mv = mv.reshape(bp, n_mem, d)
        xp, kp, vp, gp, cp = _group(xp, wts, lam_init, mk, mv, None, None, None,
                                    jnp.zeros((bp, CONV_W - 1, D_FF), F32), True)
        xs, ks_, vs_, gs, cs = _group(
            xs, wts, lam_init, _mem_cache_rows(cache_mem_k[l]), _mem_cache_rows(cache_mem_v[l]),
            cache_diff_k[l].transpose(0, 2, 3, 4, 1).reshape(bs, d, -1), cache_diff_v[l].reshape(bs, -1, 2 * DIFF_HD),
            state_gla[l], state_conv[l], False)
        vals = (kp, vp.reshape(bp, sp, DIFF_HEADS, 2 * DIFF_HD),
                mk.reshape(bp, n_mem, MEM_HEADS, MEM_HD), mv.reshape(bp, n_mem, MEM_HEADS, MEM_HD), gp, cp,
                ks_, vs_.reshape(bs, ss, DIFF_HEADS, 2 * DIFF_HD), gs, cs)
        for o, v in zip(outs, vals):
            o.append(v)
    return (xp, xs) + tuple(jnp.stack(o) for o in outs)
```
